```python
import math
import jax, jax.numpy as jnp
from jax import lax
import numpy as np

D_MODEL = 2048
BATCH = 2
SEQ = 4096
DEPTH = 1
DEC_BATCH = 128
DEC_SEQ = 8
PAST_LEN = 2048
PAGE_SIZE = 128

SB_HEADS = 8
SB_HEAD_DIM = 128
SB_WIDTH = SB_HEADS * SB_HEAD_DIM
SB_SCALE = 1.0 / math.sqrt(SB_HEAD_DIM)
DF_HEADS = 8
DF_HEAD_DIM = 64
DF_V_DIM = 2 * DF_HEAD_DIM
DF_QK_WIDTH = DF_HEADS * 2 * DF_HEAD_DIM
DF_V_WIDTH = DF_HEADS * DF_V_DIM
DF_SCALE = 1.0 / math.sqrt(DF_HEAD_DIM)
D_FF = 4 * D_MODEL
N_BUCKETS = 32
MAX_DISTANCE = 128
Q_BLOCK = 128
EPS = 1e-6
IN_OFFSETS = [SB_WIDTH, 2 * SB_WIDTH, 3 * SB_WIDTH,
              3 * SB_WIDTH + DF_QK_WIDTH,
              3 * SB_WIDTH + 2 * DF_QK_WIDTH,
              3 * SB_WIDTH + 2 * DF_QK_WIDTH + DF_V_WIDTH,
              3 * SB_WIDTH + 2 * DF_QK_WIDTH + DF_V_WIDTH + D_MODEL]
IN_WIDTH = IN_OFFSETS[-1] + D_MODEL

kernel_name = "stickbreak_diffattn_gated_hybrid_step"


def rmsnorm(x, g):
    xf = x.astype(jnp.float32)
    y = xf * lax.rsqrt(jnp.mean(xf * xf, axis=-1, keepdims=True) + EPS)
    return (y * g.astype(jnp.float32)).astype(x.dtype)


def rel_bucket(rel):
    n = jnp.maximum(rel, 0)
    max_exact = N_BUCKETS // 2
    nf = jnp.maximum(n, 1).astype(jnp.float32)
    large = max_exact + (jnp.log(nf / max_exact) / math.log(MAX_DISTANCE / max_exact)
                         * (N_BUCKETS - max_exact)).astype(jnp.int32)
    large = jnp.minimum(large, N_BUCKETS - 1)
    return jnp.where(n < max_exact, n, large)


def rel_bias_for(rel_bias, qpos, kpos):
    b = rel_bucket(qpos[:, None] - kpos[None, :])
    return jnp.transpose(rel_bias[b], (2, 0, 1)).astype(jnp.float32)


def sb_weights(z, mask):
    log_not = jnp.where(mask, -jax.nn.softplus(z), 0.0)
    suffix = lax.cumsum(log_not, axis=z.ndim - 1, reverse=True) - log_not
    log_a = jax.nn.log_sigmoid(z) + suffix
    return jnp.where(mask, jnp.exp(log_a), 0.0)


def diff_combine(logits, mask, lam):
    p = jax.nn.softmax(jnp.where(mask, logits, -jnp.inf), axis=-1)
    return p[..., 0, :, :] - lam * p[..., 1, :, :]


def project(h, w_in, g_q, g_k):
    proj = h @ w_in
    qa, ka, va, qb, kb, vb, ga, gb = jnp.split(proj, IN_OFFSETS, axis=-1)
    lead = h.shape[:-1]
    qa = qa.reshape(*lead, SB_HEADS, SB_HEAD_DIM)
    ka = ka.reshape(*lead, SB_HEADS, SB_HEAD_DIM)
    va = va.reshape(*lead, SB_HEADS, SB_HEAD_DIM)
    qb = rmsnorm(qb.reshape(*lead, DF_HEADS, 2, DF_HEAD_DIM), g_q)
    kb = rmsnorm(kb.reshape(*lead, DF_HEADS, 2, DF_HEAD_DIM), g_k)
    vb = vb.reshape(*lead, DF_HEADS, DF_V_DIM)
    return qa, ka, va, qb, kb, vb, ga, gb


def prompt_attention(qa, ka, va, qb, kb, vb, rel_bias, lam):
    b, s = qa.shape[:2]
    n_blk = s // Q_BLOCK
    kpos = jnp.arange(s, dtype=jnp.int32)

    def block(i):
        start = i * Q_BLOCK
        qpos = start + jnp.arange(Q_BLOCK, dtype=jnp.int32)
        qa_b = lax.dynamic_slice_in_dim(qa, start, Q_BLOCK, axis=1)
        qb_b = lax.dynamic_slice_in_dim(qb, start, Q_BLOCK, axis=1)
        z = jnp.einsum('bqhd,bkhd->bhqk', qa_b, ka).astype(jnp.float32) * SB_SCALE
        a = sb_weights(z, kpos[None, :] < qpos[:, None])
        oa = jnp.einsum('bhqk,bkhd->bqhd', a.astype(va.dtype), va)
        logits = (jnp.einsum('bqhcd,bkhcd->bhcqk', qb_b, kb).astype(jnp.float32) * DF_SCALE
                  + rel_bias_for(rel_bias, qpos, kpos)[None, :, None])
        w = diff_combine(logits, kpos[None, :] <= qpos[:, None], lam)
        ob = jnp.einsum('bhqk,bkhd->bqhd', w.astype(vb.dtype), vb)
        return oa, ob

    oa, ob = lax.map(block, jnp.arange(n_blk))
    oa = jnp.moveaxis(oa, 0, 1).reshape(b, s, SB_HEADS, SB_HEAD_DIM)
    ob = jnp.moveaxis(ob, 0, 1).reshape(b, s, DF_HEADS, DF_V_DIM)
    return oa, ob


def sample_attention(qa, ka, va, qb, kb, vb, cache_sb_k, cache_sb_v, cache_df_k, cache_df_v,
                     layer, page_table, rel_bias, lam):
    t = qa.shape[1]
    n_pages = page_table.shape[1]
    past = n_pages * PAGE_SIZE
    qpos = past + jnp.arange(t, dtype=jnp.int32)
    kpos = jnp.arange(past + t, dtype=jnp.int32)
    sb_mask = kpos[None, :] < qpos[:, None]
    df_mask = kpos[None, :] <= qpos[:, None]
    bias = rel_bias_for(rel_bias, qpos, kpos)

    def one(args):
        qa_s, ka_s, va_s, qb_s, kb_s, vb_s, pages = args
        kap = cache_sb_k[layer, pages].reshape(past, SB_HEADS, SB_HEAD_DIM)
        vap = cache_sb_v[layer, pages].reshape(past, SB_HEADS, SB_HEAD_DIM)
        z = jnp.concatenate([jnp.einsum('qhd,khd->hqk', qa_s, kap),
                             jnp.einsum('qhd,khd->hqk', qa_s, ka_s)], axis=-1)
        a = sb_weights(z.astype(jnp.float32) * SB_SCALE, sb_mask).astype(va_s.dtype)
        oa = (jnp.einsum('hqk,khd->qhd', a[..., :past], vap)
              + jnp.einsum('hqk,khd->qhd', a[..., past:], va_s))
        kbp = cache_df_k[layer, pages].reshape(past, DF_HEADS, 2, DF_HEAD_DIM)
        vbp = cache_df_v[layer, pages].reshape(past, DF_HEADS, DF_V_DIM)
        logits = jnp.concatenate([jnp.einsum('qhcd,khcd->hcqk', qb_s, kbp),
                                  jnp.einsum('qhcd,khcd->hcqk', qb_s, kb_s)], axis=-1)
        logits = logits.astype(jnp.float32) * DF_SCALE + bias[:, None]
        w = diff_combine(logits, df_mask, lam).astype(vb_s.dtype)
        ob = (jnp.einsum('hqk,khd->qhd', w[..., :past], vbp)
              + jnp.einsum('hqk,khd->qhd', w[..., past:], vb_s))
        return oa, ob

    return lax.map(one, (qa, ka, va, qb, kb, vb, page_table))


def merge_and_mlp(x, oa, ob, ga, gb, lam_init, subln_g, w_br_sb, w_br_df, w_out, norm2, w_up, w_down):
    lead = x.shape[:-1]
    ob = rmsnorm(ob, subln_g) * (1.0 - lam_init)
    ya = oa.reshape(*lead, SB_WIDTH) @ w_br_sb
    yb = ob.reshape(*lead, DF_V_WIDTH) @ w_br_df
    m = jax.nn.sigmoid(ga) * ya + jax.nn.sigmoid(gb) * yb
    x = x + m @ w_out
    u = jax.nn.relu(rmsnorm(x, norm2) @ w_up)
    return x + (u * u) @ w_down


def setup_inputs(seed: int = 0) -> dict:
    key = jax.random.key(seed)
    ks = jax.random.split(key, 24)
    n_pages = PAST_LEN // PAGE_SIZE
    n_used = DEC_BATCH * n_pages
    n_pool = n_used + max(1, n_used // 4)
    cshape = (DEPTH, n_pool, PAGE_SIZE, SB_HEADS, SB_HEAD_DIM)
    dshape = (DEPTH, n_pool, PAGE_SIZE, DF_HEADS, DF_V_DIM)
    page_table = jax.random.permutation(ks[6], n_pool)[:n_used].reshape(DEC_BATCH, n_pages).astype(jnp.int32)
    nrm = jax.random.normal
    f32 = jnp.float32
    return {
        "x_prompt": nrm(ks[0], (BATCH, SEQ, D_MODEL), f32),
        "x_sample": nrm(ks[1], (DEC_BATCH, DEC_SEQ, D_MODEL), f32),
        "cache_sb_k": nrm(ks[2], cshape, f32),
        "cache_sb_v": nrm(ks[3], cshape, f32),
        "cache_df_k": nrm(ks[4], dshape, f32),
        "cache_df_v": nrm(ks[5], dshape, f32),
        "page_table": page_table,
        "rel_bias": 0.5 * nrm(ks[7], (N_BUCKETS, DF_HEADS), f32),
        "norm1": 1.0 + 0.05 * nrm(ks[8], (DEPTH, D_MODEL), f32),
        "w_in": nrm(ks[9], (DEPTH, D_MODEL, IN_WIDTH), f32) * D_MODEL ** -0.5,
        "q_norm": 1.0 + 0.05 * nrm(ks[10], (DEPTH, DF_HEAD_DIM), f32),
        "k_norm": 1.0 + 0.05 * nrm(ks[11], (DEPTH, DF_HEAD_DIM), f32),
        "lambda_q1": 0.1 * nrm(ks[12], (DEPTH, DF_HEAD_DIM), f32),
        "lambda_k1": 0.1 * nrm(ks[13], (DEPTH, DF_HEAD_DIM), f32),
        "lambda_q2": 0.1 * nrm(ks[14], (DEPTH, DF_HEAD_DIM), f32),
        "lambda_k2": 0.1 * nrm(ks[15], (DEPTH, DF_HEAD_DIM), f32),
        "subln": 1.0 + 0.05 * nrm(ks[16], (DEPTH, DF_V_DIM), f32),
        "w_branch_sb": nrm(ks[17], (DEPTH, SB_WIDTH, D_MODEL), f32) * SB_WIDTH ** -0.5,
        "w_branch_df": nrm(ks[18], (DEPTH, DF_V_WIDTH, D_MODEL), f32) * DF_V_WIDTH ** -0.5,
        "w_out": nrm(ks[19], (DEPTH, D_MODEL, D_MODEL), f32) * D_MODEL ** -0.5,
        "norm2": 1.0 + 0.05 * nrm(ks[20], (DEPTH, D_MODEL), f32),
        "w_up": nrm(ks[21], (DEPTH, D_MODEL, D_FF), f32) * D_MODEL ** -0.5,
        "w_down": nrm(ks[22], (DEPTH, D_FF, D_MODEL), f32) * D_FF ** -0.5,
    }


def reference(x_prompt, x_sample, cache_sb_k, cache_sb_v, cache_df_k, cache_df_v, page_table,
              rel_bias, norm1, w_in, q_norm, k_norm, lambda_q1, lambda_k1, lambda_q2, lambda_k2,
              subln, w_branch_sb, w_branch_df, w_out, norm2, w_up, w_down):
    xp, xs = x_prompt, x_sample
    p_sbk, p_sbv, p_dfk, p_dfv = [], [], [], []
    s_sbk, s_sbv, s_dfk, s_dfv = [], [], [], []
    for l in range(DEPTH):
        lam_init = 0.8 - 0.6 * math.exp(-0.3 * l)
        f = jnp.float32
        lam = (jnp.exp(jnp.sum(lambda_q1[l].astype(f) * lambda_k1[l].astype(f)))
               - jnp.exp(jnp.sum(lambda_q2[l].astype(f) * lambda_k2[l].astype(f))) + lam_init)

        qa, ka, va, qb, kb, vb, ga, gb = project(rmsnorm(xp, norm1[l]), w_in[l], q_norm[l], k_norm[l])
        oa, ob = prompt_attention(qa, ka, va, qb, kb, vb, rel_bias, lam)
        xp = merge_and_mlp(xp, oa, ob, ga, gb, lam_init, subln[l], w_branch_sb[l], w_branch_df[l],
                           w_out[l], norm2[l], w_up[l], w_down[l])
        p_sbk.append(ka); p_sbv.append(va)
        p_dfk.append(kb.reshape(*kb.shape[:3], DF_V_DIM)); p_dfv.append(vb)

        qa, ka, va, qb, kb, vb, ga, gb = project(rmsnorm(xs, norm1[l]), w_in[l], q_norm[l], k_norm[l])
        oa, ob = sample_attention(qa, ka, va, qb, kb, vb, cache_sb_k, cache_sb_v, cache_df_k,
                                  cache_df_v, l, page_table, rel_bias, lam)
        xs = merge_and_mlp(xs, oa, ob, ga, gb, lam_init, subln[l], w_branch_sb[l], w_branch_df[l],
                           w_out[l], norm2[l], w_up[l], w_down[l])
        s_sbk.append(ka); s_sbv.append(va)
        s_dfk.append(kb.reshape(*kb.shape[:3], DF_V_DIM)); s_dfv.append(vb)

    return (xp, xs,
            jnp.stack(p_sbk), jnp.stack(p_sbv), jnp.stack(p_dfk), jnp.stack(p_dfv),
            jnp.stack(s_sbk), jnp.stack(s_sbv), jnp.stack(s_dfk), jnp.stack(s_dfv))
```

```python
import functools
import math

import numpy as np
import jax
import jax.numpy as jnp
from jax import lax
from jax.experimental import pallas as pl
from jax.experimental.pallas import tpu as pltpu

F32 = jnp.float32
BF16 = jnp.bfloat16

D_MODEL = 2048
N_HEADS = 8
HEAD_DIM = 128
DF_HALF = 64
WIDTH = N_HEADS * HEAD_DIM
SB_SCALE = 1.0 / math.sqrt(HEAD_DIM)
DF_SCALE = 1.0 / math.sqrt(DF_HALF)
D_FF = 4 * D_MODEL
N_BUCKETS = 32
MAX_EXACT = N_BUCKETS // 2
MAX_DISTANCE = 128
EPS = 1e-6
PAGE_SIZE = 128
NEG_BIG = -1e30

COL_QA, COL_KA, COL_VA, COL_QB, COL_KB, COL_VB, COL_GA, COL_GB = 0, 1, 2, 3, 4, 5, 6, 8

ATT_T = 256
PAGES_PER_STEP = 4
QCOLS = 128

VMEM_LIMIT = 56 * 1024 * 1024


def _cparams(sem):
    return pltpu.CompilerParams(dimension_semantics=sem, vmem_limit_bytes=VMEM_LIMIT)


def _tokens_by_width(ref, n_tok):
    return jnp.concatenate([ref[pl.ds(h, n_tok, stride=N_HEADS), :] for h in range(N_HEADS)], axis=1)


def _softplus(z):
    return jnp.maximum(z, 0.0) + jnp.log1p(jnp.exp(-jnp.abs(z)))


def _rmsnorm_kernel(x_ref, g_ref, o_ref):
    x = x_ref[...]
    ms = jnp.mean(x * x, axis=-1, keepdims=True)
    o_ref[...] = (x * lax.rsqrt(ms + EPS) * g_ref[...]).astype(o_ref.dtype)


def _rmsnorm_bf16(x, g, tm):
    m, d = x.shape
    return pl.pallas_call(
        _rmsnorm_kernel,
        grid=(m // tm,),
        in_specs=[pl.BlockSpec((tm, d), lambda i: (i, 0)),
                  pl.BlockSpec((1, d), lambda i: (0, 0))],
        out_specs=pl.BlockSpec((tm, d), lambda i: (i, 0)),
        out_shape=jax.ShapeDtypeStruct((m, d), BF16),
        compiler_params=_cparams(("parallel",)),
        name="rmsnorm_bf16",
    )(x, g.reshape(1, d))


def _proj_kernel(*refs, qknorm, scale, want_f32, want_bf16):
    h_ref, w_ref = refs[0], refs[1]
    pos = 2
    if qknorm:
        gain_ref, pmat_ref = refs[2], refs[3]
        pos = 4
    outs = refs[pos:]
    y = jnp.dot(h_ref[...], w_ref[...], preferred_element_type=F32)
    if qknorm:
        tn = y.shape[1]
        pieces = []
        for c in range(tn // 256):
            yb = y[:, c * 256:(c + 1) * 256]
            ms = jnp.dot((yb * yb).astype(BF16), pmat_ref[...], preferred_element_type=F32)
            pieces.append(yb * lax.rsqrt(ms + EPS))
        y = jnp.concatenate(pieces, axis=1) * gain_ref[...]
    k = 0
    if want_f32:
        for hd in range(N_HEADS):
            outs[k][pl.ds(hd, y.shape[0], stride=N_HEADS), :] = y[:, hd * HEAD_DIM:(hd + 1) * HEAD_DIM]
        k += 1
    if want_bf16:
        outs[k][...] = (y * scale).astype(BF16) if scale != 1.0 else y.astype(BF16)


def _proj(h, w, col0, ncol, *, tm, qknorm=False, gain=None, scale=1.0, want_f32=False, want_bf16=True):
    m, kdim = h.shape
    tn = WIDTH
    in_specs = [pl.BlockSpec((tm, kdim), lambda n, i: (i, 0)),
                pl.BlockSpec((kdim, tn), lambda n, i: (0, col0 + n))]
    args = [h, w]
    if qknorm:
        pmat = np.kron(np.eye(256 // DF_HALF), np.full((DF_HALF, DF_HALF), 1.0 / DF_HALF)).astype(np.float32)
        in_specs += [pl.BlockSpec((1, tn), lambda n, i: (0, 0)),
                     pl.BlockSpec((256, 256), lambda n, i: (0, 0))]
        args += [jnp.tile(gain.astype(F32), tn // DF_HALF).reshape(1, tn), jnp.asarray(pmat, dtype=BF16)]
    out_specs, out_shape = [], []
    if want_f32:
        assert ncol == 1
        out_specs.append(pl.BlockSpec((tm * N_HEADS, HEAD_DIM), lambda n, i: (i, 0)))
        out_shape.append(jax.ShapeDtypeStruct((m * N_HEADS, HEAD_DIM), F32))
    if want_bf16:
        out_specs.append(pl.BlockSpec((tm, tn), lambda n, i: (i, n)))
        out_shape.append(jax.ShapeDtypeStruct((m, ncol * tn), BF16))
    res = pl.pallas_call(
        functools.partial(_proj_kernel, qknorm=qknorm, scale=scale, want_f32=want_f32, want_bf16=want_bf16),
        grid=(ncol, m // tm),
        in_specs=in_specs,
        out_specs=out_specs,
        out_shape=out_shape,
        compiler_params=_cparams(("parallel", "parallel")),
        name="in_proj",
    )(*args)
    return res


def _bucket(rel):
    n = jnp.maximum(rel, 0)
    nf = jnp.maximum(n, 1).astype(F32)
    large = MAX_EXACT + (jnp.log(nf / MAX_EXACT) / math.log(MAX_DISTANCE / MAX_EXACT)
                         * (N_BUCKETS - MAX_EXACT)).astype(jnp.int32)
    large = jnp.minimum(large, N_BUCKETS - 1)
    return jnp.where(n < MAX_EXACT, n, large)


def _bias_prompt_kernel(rb_ref, o_ref, *, t):
    h = pl.program_id(0)
    rows = lax.broadcasted_iota(jnp.int32, (t, t), 0)
    cols = lax.broadcasted_iota(jnp.int32, (t, t), 1)
    far = rb_ref[(N_BUCKETS - 1) * N_HEADS + h]
    for d in range(2):
        b = _bucket(rows - cols + d * t)
        acc = jnp.zeros((t, t), F32)
        for k in range(N_BUCKETS - 1):
            acc = jnp.where(b == k, rb_ref[k * N_HEADS + h] - far, acc)
        o_ref[0, d] = acc


def _bias_prompt(rel_bias, t):
    return pl.pallas_call(
        functools.partial(_bias_prompt_kernel, t=t),
        grid=(N_HEADS,),
        in_specs=[pl.BlockSpec(memory_space=pltpu.SMEM)],
        out_specs=pl.BlockSpec((1, 2, t, t), lambda h: (h, 0, 0, 0)),
        out_shape=jax.ShapeDtypeStruct((N_HEADS, 2, t, t), F32),
        compiler_params=_cparams(("parallel",)),
        name="bias_prompt",
    )(rel_bias.reshape(-1))


def _bias_sample_kernel(rb_ref, last_ref, new_ref, *, past, n_new):
    def tile(rows, key0):
        lane = lax.broadcasted_iota(jnp.int32, (rows, QCOLS), 1)
        key = lax.broadcasted_iota(jnp.int32, (rows, QCOLS), 0) + key0
        head = lax.shift_right_logical(lane, (2 * n_new).bit_length() - 1)
        rel = past + (lane & (n_new - 1)) - key
        b = _bucket(rel)
        acc = jnp.zeros((rows, QCOLS), F32)
        for hh in range(N_HEADS):
            far = rb_ref[(N_BUCKETS - 1) * N_HEADS + hh]
            for k in range(N_BUCKETS - 1):
                acc = jnp.where((b == k) & (head == hh), rb_ref[k * N_HEADS + hh] - far, acc)
        return acc
    last_ref[...] = tile(PAGE_SIZE, past - PAGE_SIZE)
    new_ref[...] = tile(n_new, past)


def _bias_sample(rel_bias, past, n_new):
    return pl.pallas_call(
        functools.partial(_bias_sample_kernel, past=past, n_new=n_new),
        in_specs=[pl.BlockSpec(memory_space=pltpu.SMEM)],
        out_specs=[pl.BlockSpec((PAGE_SIZE, QCOLS), lambda: (0, 0)),
                   pl.BlockSpec((n_new, QCOLS), lambda: (0, 0))],
        out_shape=[jax.ShapeDtypeStruct((PAGE_SIZE, QCOLS), F32),
                   jax.ShapeDtypeStruct((n_new, QCOLS), F32)],
        name="bias_sample",
    )(rel_bias.reshape(-1))


def _lambda(lq1, lk1, lq2, lk2, lam_init):
    s1 = jnp.sum(lq1 * lk1, axis=-1, keepdims=True)
    s2 = jnp.sum(lq2 * lk2, axis=-1, keepdims=True)
    return jnp.exp(s1) - jnp.exp(s2) + lam_init


def _sb_prompt_kernel(q_ref, k_ref, v_ref, o_ref, *, t):
    qi = pl.program_id(2)
    q = q_ref[...]
    rows = lax.broadcasted_iota(jnp.int32, (t, t), 0)
    cols = lax.broadcasted_iota(jnp.int32, (t, t), 1)
    upper = jnp.where(rows > cols, 1.0, 0.0).astype(BF16)
    causal = cols < rows

    def block(j, carry, masked):
        acc, c = carry
        start = pl.multiple_of(j * t, t)
        kb = k_ref[pl.ds(start, t), :]
        vb = v_ref[pl.ds(start, t), :]
        z = lax.dot_general(q, kb, (((1,), (1,)), ((), ())), preferred_element_type=F32)
        sp = _softplus(z)
        if masked:
            sp = jnp.where(causal, sp, 0.0)
        sp_hi = sp.astype(BF16)
        sp_lo = (sp - sp_hi.astype(F32)).astype(BF16)
        suffix = (jnp.dot(sp_hi, upper, preferred_element_type=F32)
                  + jnp.dot(sp_lo, upper, preferred_element_type=F32))
        log_a = z - sp - suffix - c
        if masked:
            log_a = jnp.where(causal, log_a, NEG_BIG)
        a = jnp.exp(log_a)
        acc = acc + jnp.dot(a.astype(BF16), vb, preferred_element_type=F32)
        c = c + jnp.sum(sp, axis=-1, keepdims=True)
        return acc, c

    carry = (jnp.zeros((t, HEAD_DIM), F32), jnp.zeros((t, 1), F32))
    carry = block(qi, carry, True)
    carry = lax.fori_loop(0, qi, lambda i, cr: block(qi - 1 - i, cr, False), carry)
    o_ref[...] = carry[0].astype(o_ref.dtype)


def _sb_prompt(q, k, v, batch, seq, t):
    nq = seq // t
    return pl.pallas_call(
        functools.partial(_sb_prompt_kernel, t=t),
        grid=(batch, N_HEADS, nq),
        in_specs=[pl.BlockSpec((t, HEAD_DIM), lambda b, h, i: (b * nq + i, h)),
                  pl.BlockSpec((seq, HEAD_DIM), lambda b, h, i: (b, h)),
                  pl.BlockSpec((seq, HEAD_DIM), lambda b, h, i: (b, h))],
        out_specs=pl.BlockSpec((t, HEAD_DIM), lambda b, h, i: (b * nq + i, h)),
        out_shape=jax.ShapeDtypeStruct((batch * seq, WIDTH), BF16),
        compiler_params=_cparams(("parallel", "parallel", "parallel")),
        name="sb_prompt",
    )(q, k, v)


def _df_prompt_kernel(q_ref, k_ref, v_ref, bias_ref, lq1, lk1, lq2, lk2, g_ref, o_ref,
                      m_sc, l_sc, acc_sc, *, t, lam_init):
    qi = pl.program_id(2)
    q = q_ref[...]
    lane = lax.broadcasted_iota(jnp.int32, (t, HEAD_DIM), 1)
    zero = jnp.zeros_like(q)
    qs = jnp.concatenate([jnp.where(lane < DF_HALF, q, zero), jnp.where(lane >= DF_HALF, q, zero)], axis=0)
    rows = lax.broadcasted_iota(jnp.int32, (t, t), 0)
    cols = lax.broadcasted_iota(jnp.int32, (t, t), 1)
    causal = cols <= rows

    def block(j, bias, masked, first):
        start = pl.multiple_of(j * t, t)
        kb = k_ref[pl.ds(start, t), :]
        vb = v_ref[pl.ds(start, t), :]
        z = lax.dot_general(qs, kb, (((1,), (1,)), ((), ())), preferred_element_type=F32)
        if bias is not None:
            if masked:
                bias = jnp.where(causal, bias, NEG_BIG)
            z = z + jnp.concatenate([bias, bias], axis=0)
        zmax = jnp.max(z, axis=-1, keepdims=True)
        if first:
            m_new = zmax
        else:
            m_old = m_sc[...]
            m_new = jnp.maximum(m_old, zmax)
            alpha = jnp.exp(m_old - m_new)
        p = jnp.exp(z - m_new)
        psum = jnp.sum(p, axis=-1, keepdims=True)
        pv = jnp.dot(p.astype(BF16), vb, preferred_element_type=F32)
        if first:
            l_sc[...] = psum
            acc_sc[...] = pv
        else:
            l_sc[...] = alpha * l_sc[...] + psum
            acc_sc[...] = alpha * acc_sc[...] + pv
        m_sc[...] = m_new

    block(qi, bias_ref[0, 0], True, True)

    @pl.when(qi >= 1)
    def _():
        block(qi - 1, bias_ref[0, 1], False, False)

    def far(i, carry):
        block(qi - 2 - i, None, False, False)
        return carry
    lax.fori_loop(0, jnp.maximum(qi - 1, 0), far, 0)

    lam = _lambda(lq1[...], lk1[...], lq2[...], lk2[...], lam_init)
    o = acc_sc[...] / l_sc[...]
    o = o[:t] - lam * o[t:]
    ms = jnp.mean(o * o, axis=-1, keepdims=True)
    o_ref[...] = (o * lax.rsqrt(ms + EPS) * g_ref[...] * (1.0 - lam_init)).astype(o_ref.dtype)


def _df_prompt(q, k, v, bias, lams, subln, lam_init, batch, seq, t):
    nq = seq // t
    vec64 = pl.BlockSpec((1, DF_HALF), lambda b, h, i: (0, 0))
    return pl.pallas_call(
        functools.partial(_df_prompt_kernel, t=t, lam_init=lam_init),
        grid=(batch, N_HEADS, nq),
        in_specs=[pl.BlockSpec((t, HEAD_DIM), lambda b, h, i: (b * nq + i, h)),
                  pl.BlockSpec((seq, HEAD_DIM), lambda b, h, i: (b, h)),
                  pl.BlockSpec((seq, HEAD_DIM), lambda b, h, i: (b, h)),
                  pl.BlockSpec((1, 2, t, t), lambda b, h, i: (h, 0, 0, 0)),
                  vec64, vec64, vec64, vec64,
                  pl.BlockSpec((1, HEAD_DIM), lambda b, h, i: (0, 0))],
        out_specs=pl.BlockSpec((t, HEAD_DIM), lambda b, h, i: (b * nq + i, h)),
        out_shape=jax.ShapeDtypeStruct((batch * seq, WIDTH), BF16),
        scratch_shapes=[pltpu.VMEM((2 * t, 1), F32), pltpu.VMEM((2 * t, 1), F32),
                        pltpu.VMEM((2 * t, HEAD_DIM), F32)],
        compiler_params=_cparams(("parallel", "parallel", "arbitrary")),
        name="df_prompt",
    )(q, k, v, bias, *lams, subln)


def _sample_attn_kernel(pt_ref, qsb_ref, qdf_ref, kan_ref, van_ref, kbn_ref, vbn_ref,
                        blast_ref, bnew_ref, lq1, lk1, lq2, lk2, g_ref, *rest,
                        n_pages, n_new, lam_init):
    g = PAGES_PER_STEP
    ksb = rest[0:g]
    kdf = rest[g:2 * g]
    vsb = rest[2 * g:3 * g]
    vdf = rest[3 * g:4 * g]
    oa_ref, ob_ref = rest[4 * g], rest[4 * g + 1]
    zsb_sc, zdf_sc, psb_sc, pdf_sc, accsb_sc, accdf_sc = rest[4 * g + 2:]
    s = pl.program_id(1)
    n_kv = n_pages // g
    past = n_pages * PAGE_SIZE
    chunk = g * PAGE_SIZE

    def pages_bf16(refs):
        return jnp.concatenate([_tokens_by_width(r, PAGE_SIZE).astype(BF16) for r in refs], axis=0)

    def new_tokens(ref):
        return _tokens_by_width(ref, n_new)

    @pl.when(s < n_kv)
    def _scores():
        start = pl.multiple_of(s * chunk, chunk)
        zsb_sc[pl.ds(start, chunk), :] = jnp.dot(pages_bf16(ksb), qsb_ref[0], preferred_element_type=F32)
        zdf_sc[pl.ds(start, chunk), :] = jnp.dot(pages_bf16(kdf), qdf_ref[0], preferred_element_type=F32)

    def pad_rows_bf16(x):
        return jnp.concatenate([x, jnp.zeros_like(x)], axis=0).astype(BF16)

    @pl.when(s == n_kv - 1)
    def _weights():
        lane_n = lax.broadcasted_iota(jnp.int32, (n_new, QCOLS), 1)
        key_n = lax.broadcasted_iota(jnp.int32, (n_new, QCOLS), 0)
        qpos_n = lane_n & (n_new - 1)
        z_new = jnp.dot(pad_rows_bf16(new_tokens(kbn_ref)), qdf_ref[0], preferred_element_type=F32)[:n_new]
        z_new = z_new + bnew_ref[...]
        z_new = jnp.where(key_n <= qpos_n, z_new, NEG_BIG)
        last0 = past - PAGE_SIZE
        zdf_sc[last0:past, :] = zdf_sc[last0:past, :] + blast_ref[...]
        zp = zdf_sc[...]
        mx = jnp.maximum(jnp.max(zp, axis=0, keepdims=True), jnp.max(z_new, axis=0, keepdims=True))
        e_new = jnp.exp(z_new - mx)
        ep = jnp.exp(zp - mx)
        inv = 1.0 / (jnp.sum(ep, axis=0, keepdims=True) + jnp.sum(e_new, axis=0, keepdims=True))
        pdf_sc[...] = (ep * inv).astype(BF16)
        accdf_sc[...] = lax.dot_general(pad_rows_bf16(e_new * inv), pad_rows_bf16(new_tokens(vbn_ref)),
                                        (((0,), (0,)), ((), ())), preferred_element_type=F32)
        zs_new = jnp.dot(pad_rows_bf16(new_tokens(kan_ref)), qsb_ref[0], preferred_element_type=F32)[:n_new]
        strict = key_n < qpos_n
        sp_new = jnp.where(strict, _softplus(zs_new), 0.0)
        carry = jnp.zeros((1, QCOLS), F32)
        suffix_rows = [None] * n_new
        for i in range(n_new - 1, -1, -1):
            suffix_rows[i] = carry
            carry = carry + sp_new[i:i + 1]
        suffix_new = jnp.concatenate(suffix_rows, axis=0)
        a_new = jnp.exp(jnp.where(strict, zs_new - sp_new - suffix_new, NEG_BIG))
        accsb_sc[...] = lax.dot_general(pad_rows_bf16(a_new), pad_rows_bf16(new_tokens(van_ref)),
                                        (((0,), (0,)), ((), ())), preferred_element_type=F32)
        tc = 256
        rr = lax.broadcasted_iota(jnp.int32, (tc, tc), 0)
        cc = lax.broadcasted_iota(jnp.int32, (tc, tc), 1)
        upper = jnp.where(cc > rr, 1.0, 0.0).astype(BF16)
        for ci in range(past // tc - 1, -1, -1):
            zc = zsb_sc[ci * tc:(ci + 1) * tc, :]
            sp = _softplus(zc)
            sp_hi = sp.astype(BF16)
            sp_lo = (sp - sp_hi.astype(F32)).astype(BF16)
            suffix = (jnp.dot(upper, sp_hi, preferred_element_type=F32)
                      + jnp.dot(upper, sp_lo, preferred_element_type=F32))
            psb_sc[ci * tc:(ci + 1) * tc, :] = jnp.exp(zc - sp - suffix - carry).astype(BF16)
            carry = carry + jnp.sum(sp, axis=0, keepdims=True)

    @pl.when(s >= n_kv)
    def _values():
        start = pl.multiple_of((s - n_kv) * chunk, chunk)
        accsb_sc[...] += lax.dot_general(psb_sc[pl.ds(start, chunk), :], pages_bf16(vsb),
                                         (((0,), (0,)), ((), ())), preferred_element_type=F32)
        accdf_sc[...] += lax.dot_general(pdf_sc[pl.ds(start, chunk), :], pages_bf16(vdf),
                                         (((0,), (0,)), ((), ())), preferred_element_type=F32)

    @pl.when(s == 2 * n_kv - 1)
    def _finish():
        lam = _lambda(lq1[...], lk1[...], lq2[...], lk2[...], lam_init)
        for h in range(N_HEADS):
            cols = slice(h * HEAD_DIM, (h + 1) * HEAD_DIM)
            oa_ref[:, cols] = accsb_sc[h * 2 * n_new:h * 2 * n_new + n_new, cols]
            o1 = accdf_sc[h * 2 * n_new:h * 2 * n_new + n_new, cols]
            o2 = accdf_sc[h * 2 * n_new + n_new:(h + 1) * 2 * n_new, cols]
            o = o1 - lam * o2
            ms = jnp.mean(o * o, axis=-1, keepdims=True)
            ob_ref[:, cols] = o * lax.rsqrt(ms + EPS) * g_ref[...] * (1.0 - lam_init)


def _sample_attn(page_table, qsb, qdf, ka_n, va_n, kb_n, vb_n, blast, bnew, lams, subln,
                 c_sbk, c_sbv, c_dfk, c_dfv, layer, lam_init):
    n_seq, n_pages = page_table.shape
    n_new = ka_n.shape[0] // (n_seq * N_HEADS)
    assert 2 * N_HEADS * n_new == QCOLS
    g = PAGES_PER_STEP
    n_kv = n_pages // g
    past = n_pages * PAGE_SIZE

    def cache_spec(gi, is_value):
        def imap(b, s, pt):
            step = jnp.maximum(s - n_kv, 0) if is_value else jnp.minimum(s, n_kv - 1)
            return (layer, pt[b * n_pages + step * g + gi], 0, 0)
        return pl.BlockSpec((None, None, PAGE_SIZE * N_HEADS, HEAD_DIM), imap)

    vec64 = pl.BlockSpec((1, DF_HALF), lambda b, s, pt: (0, 0))
    new_spec = pl.BlockSpec((n_new * N_HEADS, HEAD_DIM), lambda b, s, pt: (b, 0))
    out_spec = pl.BlockSpec((n_new, WIDTH), lambda b, s, pt: (b, 0))
    in_specs = [pl.BlockSpec((1, WIDTH, QCOLS), lambda b, s, pt: (b, 0, 0)),
                pl.BlockSpec((1, WIDTH, QCOLS), lambda b, s, pt: (b, 0, 0)),
                new_spec, new_spec, new_spec, new_spec,
                pl.BlockSpec((PAGE_SIZE, QCOLS), lambda b, s, pt: (0, 0)),
                pl.BlockSpec((n_new, QCOLS), lambda b, s, pt: (0, 0)),
                vec64, vec64, vec64, vec64,
                pl.BlockSpec((1, HEAD_DIM), lambda b, s, pt: (0, 0))]
    in_specs += [cache_spec(gi, False) for gi in range(g)] * 2
    in_specs += [cache_spec(gi, True) for gi in range(g)] * 2
    caches = [c_sbk] * g + [c_dfk] * g + [c_sbv] * g + [c_dfv] * g
    grid_spec = pltpu.PrefetchScalarGridSpec(
        num_scalar_prefetch=1,
        grid=(n_seq, 2 * n_kv),
        in_specs=in_specs,
        out_specs=[out_spec, out_spec],
        scratch_shapes=[pltpu.VMEM((past, QCOLS), F32), pltpu.VMEM((past, QCOLS), F32),
                        pltpu.VMEM((past, QCOLS), BF16), pltpu.VMEM((past, QCOLS), BF16),
                        pltpu.VMEM((QCOLS, WIDTH), F32), pltpu.VMEM((QCOLS, WIDTH), F32)])
    return pl.pallas_call(
        functools.partial(_sample_attn_kernel, n_pages=n_pages, n_new=n_new, lam_init=lam_init),
        grid_spec=grid_spec,
        out_shape=[jax.ShapeDtypeStruct((n_seq * n_new, WIDTH), F32)] * 2,
        compiler_params=_cparams(("parallel", "arbitrary")),
        name="sample_attn",
    )(page_table.reshape(-1), qsb, qdf, ka_n, va_n, kb_n, vb_n, blast, bnew, *lams, subln, *caches)


def _block_diag_queries(q, n_seq, n_new, halves):
    qh = q.reshape(n_seq, n_new, N_HEADS, halves, HEAD_DIM // halves)
    eye_h = jnp.eye(N_HEADS, dtype=q.dtype)
    eye_c = jnp.eye(halves, 2, dtype=q.dtype)
    m = jnp.einsum("bqhcd,hg,ce->bhcdgeq", qh, eye_h, eye_c)
    return m.reshape(n_seq, WIDTH, QCOLS)


def _merge_kernel(x_ref, oa_ref, ob_ref, ga_ref, gb_ref, wsb_ref, wdf_ref, wout_ref, g2_ref, x1_ref, hn_ref):
    ya = jnp.dot(oa_ref[...].astype(BF16), wsb_ref[...], preferred_element_type=F32)
    yb = jnp.dot(ob_ref[...].astype(BF16), wdf_ref[...], preferred_element_type=F32)
    m = jax.nn.sigmoid(ga_ref[...].astype(F32)) * ya + jax.nn.sigmoid(gb_ref[...].astype(F32)) * yb
    x1 = x_ref[...] + jnp.dot(m.astype(BF16), wout_ref[...], preferred_element_type=F32)
    x1_ref[...] = x1
    ms = jnp.mean(x1 * x1, axis=-1, keepdims=True)
    hn_ref[...] = (x1 * lax.rsqrt(ms + EPS) * g2_ref[...]).astype(BF16)


def _merge(x, oa, ob, gates, wsb, wdf, wout, norm2, tm):
    m, d = x.shape
    const = lambda shape: pl.BlockSpec(shape, lambda i: (0, 0), pipeline_mode=pl.Buffered(1))
    return pl.pallas_call(
        _merge_kernel,
        grid=(m // tm,),
        in_specs=[pl.BlockSpec((tm, d), lambda i: (i, 0)),
                  pl.BlockSpec((tm, WIDTH), lambda i: (i, 0)),
                  pl.BlockSpec((tm, WIDTH), lambda i: (i, 0)),
                  pl.BlockSpec((tm, d), lambda i: (i, 0)),
                  pl.BlockSpec((tm, d), lambda i: (i, 1)),
                  const((WIDTH, d)), const((WIDTH, d)), const((d, d)),
                  pl.BlockSpec((1, d), lambda i: (0, 0))],
        out_specs=[pl.BlockSpec((tm, d), lambda i: (i, 0)), pl.BlockSpec((tm, d), lambda i: (i, 0))],
        out_shape=[jax.ShapeDtypeStruct((m, d), F32), jax.ShapeDtypeStruct((m, d), BF16)],
        compiler_params=_cparams(("parallel",)),
        name="merge",
    )(x, oa, ob, gates, gates, wsb, wdf, wout, norm2.reshape(1, d))


def _mlp_kernel(x1_ref, hn_ref, wup_ref, wdn_ref, o_ref):
    f = pl.program_id(1)

    @pl.when(f == 0)
    def _():
        o_ref[...] = x1_ref[...]

    u = jnp.maximum(jnp.dot(hn_ref[...], wup_ref[...], preferred_element_type=F32), 0.0)
    o_ref[...] += jnp.dot((u * u).astype(BF16), wdn_ref[...], preferred_element_type=F32)


def _mlp(x1, hn, wup, wdn, tm, tf):
    m, d = x1.shape
    dff = wup.shape[1]
    return pl.pallas_call(
        _mlp_kernel,
        grid=(m // tm, dff // tf),
        in_specs=[pl.BlockSpec((tm, d), lambda i, f: (i, 0)),
                  pl.BlockSpec((tm, d), lambda i, f: (i, 0)),
                  pl.BlockSpec((d, tf), lambda i, f: (0, f)),
                  pl.BlockSpec((tf, d), lambda i, f: (f, 0))],
        out_specs=pl.BlockSpec((tm, d), lambda i, f: (i, 0)),
        out_shape=jax.ShapeDtypeStruct((m, d), F32),
        compiler_params=_cparams(("parallel", "arbitrary")),
        name="mlp",
    )(x1, hn, wup, wdn)


def _project_group(x, norm1, w_in, q_norm, k_norm, tm):
    h = _rmsnorm_bf16(x, norm1, min(tm, 512))
    (qa,) = _proj(h, w_in, COL_QA, 1, tm=tm, scale=SB_SCALE)
    ka, ka16 = _proj(h, w_in, COL_KA, 1, tm=tm, want_f32=True)
    va, va16 = _proj(h, w_in, COL_VA, 1, tm=tm, want_f32=True)
    (qb,) = _proj(h, w_in, COL_QB, 1, tm=tm, qknorm=True, gain=q_norm, scale=DF_SCALE)
    kb, kb16 = _proj(h, w_in, COL_KB, 1, tm=tm, qknorm=True, gain=k_norm, want_f32=True)
    vb, vb16 = _proj(h, w_in, COL_VB, 1, tm=tm, want_f32=True)
    (gates,) = _proj(h, w_in, COL_GA, 4, tm=tm)
    return dict(qa=qa, ka=ka, ka16=ka16, va=va, va16=va16, qb=qb, kb=kb, kb16=kb16, vb=vb, vb16=vb16, gates=gates)


def kernel(x_prompt, x_sample, cache_sb_k, cache_sb_v, cache_df_k, cache_df_v, page_table, rel_bias, norm1, w_in, q_norm, k_norm, lambda_q1, lambda_k1, lambda_q2, lambda_k2, subln, w_branch_sb, w_branch_df, w_out, norm2, w_up, w_down):
    depth = norm1.shape[0]
    batch, seq, d = x_prompt.shape
    n_seq, n_new, _ = x_sample.shape
    n_pages = page_table.shape[1]
    past = n_pages * PAGE_SIZE
    xp = x_prompt.reshape(batch * seq, d)
    xs = x_sample.reshape(n_seq * n_new, d)
    bias_p = _bias_prompt(rel_bias, ATT_T)
    bias_last, bias_new = _bias_sample(rel_bias, past, n_new)
    n_pool = cache_sb_k.shape[1]
    leaves = [[] for _ in range(8)]
    for l in range(depth):
        lam_init = 0.8 - 0.6 * math.exp(-0.3 * l)
        w_in16 = w_in[l].astype(BF16)
        wsb16 = w_branch_sb[l].astype(BF16)
        wdf16 = w_branch_df[l].astype(BF16)
        wout16 = w_out[l].astype(BF16)
        wup16 = w_up[l].astype(BF16)
        wdn16 = w_down[l].astype(BF16)
        lams = [v[l].reshape(1, DF_HALF).astype(F32) for v in (lambda_q1, lambda_k1, lambda_q2, lambda_k2)]
        sub = subln[l].reshape(1, HEAD_DIM).astype(F32)

        p = _project_group(xp, norm1[l], w_in16, q_norm[l], k_norm[l], 1024)
        oa = _sb_prompt(p["qa"], p["ka16"], p["va16"], batch, seq, ATT_T)
        ob = _df_prompt(p["qb"], p["kb16"], p["vb16"], bias_p, lams, sub, lam_init, batch, seq, ATT_T)
        x1, hn = _merge(xp, oa, ob, p["gates"], wsb16, wdf16, wout16, norm2[l], 256)
        xp = _mlp(x1, hn, wup16, wdn16, 512, 1024)
        for i, name in enumerate(("ka", "va", "kb", "vb")):
            leaves[i].append(p[name].reshape(batch, seq, N_HEADS, HEAD_DIM))

        s = _project_group(xs, norm1[l], w_in16, q_norm[l], k_norm[l], 1024)
        qsb = _block_diag_queries(s["qa"], n_seq, n_new, 1)
        qdf = _block_diag_queries(s["qb"], n_seq, n_new, 2)
        shape4 = (depth, n_pool, PAGE_SIZE * N_HEADS, HEAD_DIM)
        oa, ob = _sample_attn(page_table, qsb, qdf, s["ka"], s["va"], s["kb"], s["vb"], bias_last, bias_new,
                              lams, sub, cache_sb_k.reshape(shape4), cache_sb_v.reshape(shape4),
                              cache_df_k.reshape(shape4), cache_df_v.reshape(shape4), l, lam_init)
        x1, hn = _merge(xs, oa, ob, s["gates"], wsb16, wdf16, wout16, norm2[l], 256)
        xs = _mlp(x1, hn, wup16, wdn16, 512, 1024)
        for i, name in enumerate(("ka", "va", "kb", "vb")):
            leaves[4 + i].append(s[name].reshape(n_seq, n_new, N_HEADS, HEAD_DIM))

    return (xp.reshape(batch, seq, d), xs.reshape(n_seq, n_new, d)) + tuple(jnp.stack(v) for v in leaves)
```

```python
import functools
import math

import numpy as np
import jax
import jax.numpy as jnp
from jax import lax
from jax.experimental import pallas as pl
from jax.experimental.pallas import tpu as pltpu

F32 = jnp.float32
BF16 = jnp.bfloat16

D_MODEL = 2048
N_HEADS = 8
HEAD_DIM = 128
DF_HALF = 64
WIDTH = N_HEADS * HEAD_DIM
SB_SCALE = 1.0 / math.sqrt(HEAD_DIM)
DF_SCALE = 1.0 / math.sqrt(DF_HALF)
D_FF = 4 * D_MODEL
N_BUCKETS = 32
MAX_EXACT = N_BUCKETS // 2
MAX_DISTANCE = 128
EPS = 1e-6
PAGE_SIZE = 128
NEG_BIG = -1e30

COL_QA, COL_KA, COL_VA, COL_QB, COL_KB, COL_VB, COL_GA, COL_GB = 0, 1, 2, 3, 4, 5, 6, 8

ATT_T = 512
PAGES_PER_STEP = 8
QCOLS = 128

VMEM_LIMIT = 56 * 1024 * 1024


def _cparams(sem):
    return pltpu.CompilerParams(dimension_semantics=sem, vmem_limit_bytes=VMEM_LIMIT)


def _tokens_by_width(ref, n_tok):
    return jnp.concatenate([ref[pl.ds(h, n_tok, stride=N_HEADS), :] for h in range(N_HEADS)], axis=1)


def _softplus(z):
    return jnp.maximum(z, 0.0) + jnp.log(1.0 + jnp.exp(-jnp.abs(z)))


def _rmsnorm_kernel(x_ref, g_ref, o_ref):
    x = x_ref[...]
    ms = jnp.mean(x * x, axis=-1, keepdims=True)
    o_ref[...] = (x * lax.rsqrt(ms + EPS) * g_ref[...]).astype(o_ref.dtype)


def _rmsnorm_bf16(x, g, tm):
    m, d = x.shape
    return pl.pallas_call(
        _rmsnorm_kernel,
        grid=(m // tm,),
        in_specs=[pl.BlockSpec((tm, d), lambda i: (i, 0)),
                  pl.BlockSpec((1, d), lambda i: (0, 0))],
        out_specs=pl.BlockSpec((tm, d), lambda i: (i, 0)),
        out_shape=jax.ShapeDtypeStruct((m, d), BF16),
        compiler_params=_cparams(("parallel",)),
        name="rmsnorm_bf16",
    )(x, g.reshape(1, d))


def _proj_kernel(*refs, qknorm, scale, want_f32, want_bf16):
    h_ref, w_ref = refs[0], refs[1]
    pos = 2
    if qknorm:
        gain_ref, pmat_ref = refs[2], refs[3]
        pos = 4
    outs = refs[pos:]
    y = jnp.dot(h_ref[...], w_ref[...], preferred_element_type=F32)
    if qknorm:
        tn = y.shape[1]
        pieces = []
        for c in range(tn // 256):
            yb = y[:, c * 256:(c + 1) * 256]
            ms = jnp.dot((yb * yb).astype(BF16), pmat_ref[...], preferred_element_type=F32)
            pieces.append(yb * lax.rsqrt(ms + EPS))
        y = jnp.concatenate(pieces, axis=1) * gain_ref[...]
    k = 0
    if want_f32:
        for hd in range(N_HEADS):
            outs[k][pl.ds(hd, y.shape[0], stride=N_HEADS), :] = y[:, hd * HEAD_DIM:(hd + 1) * HEAD_DIM]
        k += 1
    if want_bf16:
        outs[k][...] = (y * scale).astype(BF16) if scale != 1.0 else y.astype(BF16)


def _proj(h, w, col0, ncol, *, tm, qknorm=False, gain=None, scale=1.0, want_f32=False, want_bf16=True):
    m, kdim = h.shape
    tn = WIDTH
    in_specs = [pl.BlockSpec((tm, kdim), lambda n, i: (i, 0)),
                pl.BlockSpec((kdim, tn), lambda n, i: (0, col0 + n))]
    args = [h, w]
    if qknorm:
        pmat = np.kron(np.eye(256 // DF_HALF), np.full((DF_HALF, DF_HALF), 1.0 / DF_HALF)).astype(np.float32)
        in_specs += [pl.BlockSpec((1, tn), lambda n, i: (0, 0)),
                     pl.BlockSpec((256, 256), lambda n, i: (0, 0))]
        args += [jnp.tile(gain.astype(F32), tn // DF_HALF).reshape(1, tn), jnp.asarray(pmat, dtype=BF16)]
    out_specs, out_shape = [], []
    if want_f32:
        assert ncol == 1
        out_specs.append(pl.BlockSpec((tm * N_HEADS, HEAD_DIM), lambda n, i: (i, 0)))
        out_shape.append(jax.ShapeDtypeStruct((m * N_HEADS, HEAD_DIM), F32))
    if want_bf16:
        out_specs.append(pl.BlockSpec((tm, tn), lambda n, i: (i, n)))
        out_shape.append(jax.ShapeDtypeStruct((m, ncol * tn), BF16))
    res = pl.pallas_call(
        functools.partial(_proj_kernel, qknorm=qknorm, scale=scale, want_f32=want_f32, want_bf16=want_bf16),
        grid=(ncol, m // tm),
        in_specs=in_specs,
        out_specs=out_specs,
        out_shape=out_shape,
        compiler_params=_cparams(("parallel", "parallel")),
        name="in_proj",
    )(*args)
    return res


def _bucket(rel):
    n = jnp.maximum(rel, 0)
    nf = jnp.maximum(n, 1).astype(F32)
    large = MAX_EXACT + (jnp.log(nf / MAX_EXACT) / math.log(MAX_DISTANCE / MAX_EXACT)
                         * (N_BUCKETS - MAX_EXACT)).astype(jnp.int32)
    large = jnp.minimum(large, N_BUCKETS - 1)
    return jnp.where(n < MAX_EXACT, n, large)


def _bias_prompt_kernel(rb_ref, o_ref, *, t):
    h = pl.program_id(0)
    rows = lax.broadcasted_iota(jnp.int32, (t, t), 0)
    cols = lax.broadcasted_iota(jnp.int32, (t, t), 1)
    far = rb_ref[(N_BUCKETS - 1) * N_HEADS + h]
    for d in range(2):
        b = _bucket(rows - cols + d * t)
        acc = jnp.zeros((t, t), F32)
        for k in range(N_BUCKETS - 1):
            acc = jnp.where(b == k, rb_ref[k * N_HEADS + h] - far, acc)
        if d == 0:
            acc = jnp.where(cols <= rows, acc, NEG_BIG)
        o_ref[0, d] = acc


def _bias_prompt(rel_bias, t):
    return pl.pallas_call(
        functools.partial(_bias_prompt_kernel, t=t),
        grid=(N_HEADS,),
        in_specs=[pl.BlockSpec(memory_space=pltpu.SMEM)],
        out_specs=pl.BlockSpec((1, 2, t, t), lambda h: (h, 0, 0, 0)),
        out_shape=jax.ShapeDtypeStruct((N_HEADS, 2, t, t), F32),
        compiler_params=_cparams(("parallel",)),
        name="bias_prompt",
    )(rel_bias.reshape(-1))


def _bias_sample_kernel(rb_ref, last_ref, new_ref, *, past, n_new):
    def tile(rows, key0):
        lane = lax.broadcasted_iota(jnp.int32, (rows, QCOLS), 1)
        key = lax.broadcasted_iota(jnp.int32, (rows, QCOLS), 0) + key0
        head = lax.shift_right_logical(lane, (2 * n_new).bit_length() - 1)
        rel = past + (lane & (n_new - 1)) - key
        b = _bucket(rel)
        acc = jnp.zeros((rows, QCOLS), F32)
        for hh in range(N_HEADS):
            far = rb_ref[(N_BUCKETS - 1) * N_HEADS + hh]
            for k in range(N_BUCKETS - 1):
                acc = jnp.where((b == k) & (head == hh), rb_ref[k * N_HEADS + hh] - far, acc)
        return acc
    last_ref[...] = tile(PAGE_SIZE, past - PAGE_SIZE)
    new_ref[...] = tile(n_new, past)


def _bias_sample(rel_bias, past, n_new):
    return pl.pallas_call(
        functools.partial(_bias_sample_kernel, past=past, n_new=n_new),
        in_specs=[pl.BlockSpec(memory_space=pltpu.SMEM)],
        out_specs=[pl.BlockSpec((PAGE_SIZE, QCOLS), lambda: (0, 0)),
                   pl.BlockSpec((n_new, QCOLS), lambda: (0, 0))],
        out_shape=[jax.ShapeDtypeStruct((PAGE_SIZE, QCOLS), F32),
                   jax.ShapeDtypeStruct((n_new, QCOLS), F32)],
        name="bias_sample",
    )(rel_bias.reshape(-1))


def _lambda(lq1, lk1, lq2, lk2, lam_init):
    s1 = jnp.sum(lq1 * lk1, axis=-1, keepdims=True)
    s2 = jnp.sum(lq2 * lk2, axis=-1, keepdims=True)
    return jnp.exp(s1) - jnp.exp(s2) + lam_init


def _sb_prompt_kernel(q_ref, k_ref, v_ref, upper_ref, o_ref, *, t):
    qi = pl.program_id(2)
    q = q_ref[...]
    rows = lax.broadcasted_iota(jnp.int32, (t, t), 0)
    cols = lax.broadcasted_iota(jnp.int32, (t, t), 1)
    causal = cols < rows

    def block(j, carry, masked):
        acc, c = carry
        start = pl.multiple_of(j * t, t)
        kb = k_ref[pl.ds(start, t), :]
        vb = v_ref[pl.ds(start, t), :]
        z = lax.dot_general(q, kb, (((1,), (1,)), ((), ())), preferred_element_type=F32)
        sp = _softplus(z)
        if masked:
            sp = jnp.where(causal, sp, 0.0)
        suffix = jnp.dot(sp.astype(BF16), upper_ref[...], preferred_element_type=F32)
        log_a = z - sp - suffix - c
        if masked:
            log_a = jnp.where(causal, log_a, NEG_BIG)
        a = jnp.exp(log_a)
        acc = acc + jnp.dot(a.astype(BF16), vb, preferred_element_type=F32)
        c = c + jnp.sum(sp, axis=-1, keepdims=True)
        return acc, c

    carry = (jnp.zeros((t, HEAD_DIM), F32), jnp.zeros((t, 1), F32))
    carry = block(qi, carry, True)
    carry = lax.fori_loop(0, qi, lambda i, cr: block(qi - 1 - i, cr, False), carry)
    o_ref[...] = carry[0].astype(o_ref.dtype)


def _sb_prompt(q, k, v, batch, seq, t):
    nq = seq // t
    upper = jnp.asarray(np.tril(np.ones((t, t), np.float32), -1), dtype=BF16)
    return pl.pallas_call(
        functools.partial(_sb_prompt_kernel, t=t),
        grid=(batch, N_HEADS, nq),
        in_specs=[pl.BlockSpec((t, HEAD_DIM), lambda b, h, i: (b * nq + i, h)),
                  pl.BlockSpec((seq, HEAD_DIM), lambda b, h, i: (b, h)),
                  pl.BlockSpec((seq, HEAD_DIM), lambda b, h, i: (b, h)),
                  pl.BlockSpec((t, t), lambda b, h, i: (0, 0))],
        out_specs=pl.BlockSpec((t, HEAD_DIM), lambda b, h, i: (b * nq + i, h)),
        out_shape=jax.ShapeDtypeStruct((batch * seq, WIDTH), BF16),
        compiler_params=_cparams(("parallel", "parallel", "parallel")),
        name="sb_prompt",
    )(q, k, v, upper)


def _df_prompt_kernel(q_ref, k_ref, v_ref, bias_ref, lq1, lk1, lq2, lk2, g_ref, o_ref,
                      vext_sc, m_sc, l_sc, acc_sc, *, t, lam_init):
    qi = pl.program_id(2)
    nc = t // HEAD_DIM

    @pl.when(qi == 0)
    def _():
        vext_sc[:, :HEAD_DIM] = v_ref[...]
        vext_sc[:, HEAD_DIM:] = jnp.ones((vext_sc.shape[0], HEAD_DIM), BF16)

    q = q_ref[...]
    lane = lax.broadcasted_iota(jnp.int32, (t, HEAD_DIM), 1)
    zero = jnp.zeros_like(q)
    qs = jnp.concatenate([jnp.where(lane < DF_HALF, q, zero), jnp.where(lane >= DF_HALF, q, zero)], axis=0)

    def block(j, bias, first):
        start = pl.multiple_of(j * t, t)
        kb = k_ref[pl.ds(start, t), :]
        vb = vext_sc[pl.ds(start, t), :]
        z = lax.dot_general(qs, kb, (((1,), (1,)), ((), ())), preferred_element_type=F32)
        ps = []
        m_news = []
        for r in range(2):
            tiles = [z[r * t:(r + 1) * t, c * HEAD_DIM:(c + 1) * HEAD_DIM] for c in range(nc)]
            if bias is not None:
                tiles = [tl + bias[:, c * HEAD_DIM:(c + 1) * HEAD_DIM] for c, tl in enumerate(tiles)]
            part = functools.reduce(jnp.maximum, tiles)
            zmax = jnp.max(part, axis=-1, keepdims=True)
            if first:
                m_new = jnp.broadcast_to(zmax, (t, HEAD_DIM))
            else:
                m_new = jnp.maximum(m_sc[r * t:(r + 1) * t, :], zmax)
            m_news.append(m_new)
            ps.append(jnp.concatenate([jnp.exp(tl - m_new) for tl in tiles], axis=1).astype(BF16))
        m_new = jnp.concatenate(m_news, axis=0)
        pv = jnp.dot(jnp.concatenate(ps, axis=0), vb, preferred_element_type=F32)
        if first:
            l_sc[...] = pv[:, HEAD_DIM:]
            acc_sc[...] = pv[:, :HEAD_DIM]
        else:
            alpha = jnp.exp(m_sc[...] - m_new)
            l_sc[...] = alpha * l_sc[...] + pv[:, HEAD_DIM:]
            acc_sc[...] = alpha * acc_sc[...] + pv[:, :HEAD_DIM]
        m_sc[...] = m_new

    block(qi, bias_ref[0, 0], True)

    @pl.when(qi >= 1)
    def _():
        block(qi - 1, bias_ref[0, 1], False)

    def far(i, carry):
        block(qi - 2 - i, None, False)
        return carry
    lax.fori_loop(0, jnp.maximum(qi - 1, 0), far, 0)

    lam = _lambda(lq1[...], lk1[...], lq2[...], lk2[...], lam_init)
    o = acc_sc[...] / l_sc[...]
    o = o[:t] - lam * o[t:]
    ms = jnp.mean(o * o, axis=-1, keepdims=True)
    o_ref[...] = (o * lax.rsqrt(ms + EPS) * g_ref[...] * (1.0 - lam_init)).astype(o_ref.dtype)


def _df_prompt(q, k, v, bias, lams, subln, lam_init, batch, seq, t):
    nq = seq // t
    vec64 = pl.BlockSpec((1, DF_HALF), lambda b, h, i: (0, 0))
    return pl.pallas_call(
        functools.partial(_df_prompt_kernel, t=t, lam_init=lam_init),
        grid=(batch, N_HEADS, nq),
        in_specs=[pl.BlockSpec((t, HEAD_DIM), lambda b, h, i: (b * nq + i, h)),
                  pl.BlockSpec((seq, HEAD_DIM), lambda b, h, i: (b, h)),
                  pl.BlockSpec((seq, HEAD_DIM), lambda b, h, i: (b, h)),
                  pl.BlockSpec((1, 2, t, t), lambda b, h, i: (h, 0, 0, 0)),
                  vec64, vec64, vec64, vec64,
                  pl.BlockSpec((1, HEAD_DIM), lambda b, h, i: (0, 0))],
        out_specs=pl.BlockSpec((t, HEAD_DIM), lambda b, h, i: (b * nq + i, h)),
        out_shape=jax.ShapeDtypeStruct((batch * seq, WIDTH), BF16),
        scratch_shapes=[pltpu.VMEM((seq, 2 * HEAD_DIM), BF16),
                        pltpu.VMEM((2 * t, HEAD_DIM), F32), pltpu.VMEM((2 * t, HEAD_DIM), F32),
                        pltpu.VMEM((2 * t, HEAD_DIM), F32)],
        compiler_params=_cparams(("parallel", "parallel", "arbitrary")),
        name="df_prompt",
    )(q, k, v, bias, *lams, subln)


def _sample_attn_kernel(pt_ref, qsb_ref, qdf_ref, kan_ref, van_ref, kbn_ref, vbn_ref,
                        blast_ref, bnew_ref, lq1, lk1, lq2, lk2, g_ref, *rest,
                        n_pages, n_new, lam_init):
    g = PAGES_PER_STEP
    ksb = rest[0:g]
    kdf = rest[g:2 * g]
    vsb = rest[2 * g:3 * g]
    vdf = rest[3 * g:4 * g]
    oa_ref, ob_ref = rest[4 * g], rest[4 * g + 1]
    zsb_sc, zdf_sc, psb_sc, pdf_sc, accsb_sc, accdf_sc = rest[4 * g + 2:]
    s = pl.program_id(1)
    n_kv = n_pages // g
    past = n_pages * PAGE_SIZE
    chunk = g * PAGE_SIZE

    def pages_bf16(refs):
        return jnp.concatenate([_tokens_by_width(r, PAGE_SIZE).astype(BF16) for r in refs], axis=0)

    def new_tokens(ref):
        return _tokens_by_width(ref, n_new)

    @pl.when(s < n_kv)
    def _scores():
        start = pl.multiple_of(s * chunk, chunk)
        zsb_sc[pl.ds(start, chunk), :] = jnp.dot(pages_bf16(ksb), qsb_ref[0], preferred_element_type=F32)
        zdf_sc[pl.ds(start, chunk), :] = jnp.dot(pages_bf16(kdf), qdf_ref[0], preferred_element_type=F32)

    def pad_rows_bf16(x):
        return jnp.concatenate([x, jnp.zeros_like(x)], axis=0).astype(BF16)

    @pl.when(s == n_kv - 1)
    def _weights():
        lane_n = lax.broadcasted_iota(jnp.int32, (n_new, QCOLS), 1)
        key_n = lax.broadcasted_iota(jnp.int32, (n_new, QCOLS), 0)
        qpos_n = lane_n & (n_new - 1)
        z_new = jnp.dot(pad_rows_bf16(new_tokens(kbn_ref)), qdf_ref[0], preferred_element_type=F32)[:n_new]
        z_new = z_new + bnew_ref[...]
        z_new = jnp.where(key_n <= qpos_n, z_new, NEG_BIG)
        last0 = past - PAGE_SIZE
        zdf_sc[last0:past, :] = zdf_sc[last0:past, :] + blast_ref[...]
        zp = zdf_sc[...]
        mx = jnp.maximum(jnp.max(zp, axis=0, keepdims=True), jnp.max(z_new, axis=0, keepdims=True))
        e_new = jnp.exp(z_new - mx)
        ep = jnp.exp(zp - mx)
        inv = 1.0 / (jnp.sum(ep, axis=0, keepdims=True) + jnp.sum(e_new, axis=0, keepdims=True))
        pdf_sc[...] = (ep * inv).astype(BF16)
        accdf_sc[...] = lax.dot_general(pad_rows_bf16(e_new * inv), pad_rows_bf16(new_tokens(vbn_ref)),
                                        (((0,), (0,)), ((), ())), preferred_element_type=F32)
        zs_new = jnp.dot(pad_rows_bf16(new_tokens(kan_ref)), qsb_ref[0], preferred_element_type=F32)[:n_new]
        strict = key_n < qpos_n
        sp_new = jnp.where(strict, _softplus(zs_new), 0.0)
        carry = jnp.zeros((1, QCOLS), F32)
        suffix_rows = [None] * n_new
        for i in range(n_new - 1, -1, -1):
            suffix_rows[i] = carry
            carry = carry + sp_new[i:i + 1]
        suffix_new = jnp.concatenate(suffix_rows, axis=0)
        a_new = jnp.exp(jnp.where(strict, zs_new - sp_new - suffix_new, NEG_BIG))
        accsb_sc[...] = lax.dot_general(pad_rows_bf16(a_new), pad_rows_bf16(new_tokens(van_ref)),
                                        (((0,), (0,)), ((), ())), preferred_element_type=F32)
        tc = 256
        rr = lax.broadcasted_iota(jnp.int32, (tc, tc), 0)
        cc = lax.broadcasted_iota(jnp.int32, (tc, tc), 1)
        upper = jnp.where(cc > rr, 1.0, 0.0).astype(BF16)
        for ci in range(past // tc - 1, -1, -1):
            zc = zsb_sc[ci * tc:(ci + 1) * tc, :]
            sp = _softplus(zc)
            sp_hi = sp.astype(BF16)
            sp_lo = (sp - sp_hi.astype(F32)).astype(BF16)
            suffix = (jnp.dot(upper, sp_hi, preferred_element_type=F32)
                      + jnp.dot(upper, sp_lo, preferred_element_type=F32))
            psb_sc[ci * tc:(ci + 1) * tc, :] = jnp.exp(zc - sp - suffix - carry).astype(BF16)
            carry = carry + jnp.sum(sp, axis=0, keepdims=True)

    @pl.when(s >= n_kv)
    def _values():
        start = pl.multiple_of((s - n_kv) * chunk, chunk)
        accsb_sc[...] += lax.dot_general(psb_sc[pl.ds(start, chunk), :], pages_bf16(vsb),
                                         (((0,), (0,)), ((), ())), preferred_element_type=F32)
        accdf_sc[...] += lax.dot_general(pdf_sc[pl.ds(start, chunk), :], pages_bf16(vdf),
                                         (((0,), (0,)), ((), ())), preferred_element_type=F32)

    @pl.when(s == 2 * n_kv - 1)
    def _finish():
        lam = _lambda(lq1[...], lk1[...], lq2[...], lk2[...], lam_init)
        for h in range(N_HEADS):
            cols = slice(h * HEAD_DIM, (h + 1) * HEAD_DIM)
            oa_ref[:, cols] = accsb_sc[h * 2 * n_new:h * 2 * n_new + n_new, cols]
            o1 = accdf_sc[h * 2 * n_new:h * 2 * n_new + n_new, cols]
            o2 = accdf_sc[h * 2 * n_new + n_new:(h + 1) * 2 * n_new, cols]
            o = o1 - lam * o2
            ms = jnp.mean(o * o, axis=-1, keepdims=True)
            ob_ref[:, cols] = o * lax.rsqrt(ms + EPS) * g_ref[...] * (1.0 - lam_init)


def _sample_attn(page_table, qsb, qdf, ka_n, va_n, kb_n, vb_n, blast, bnew, lams, subln,
                 c_sbk, c_sbv, c_dfk, c_dfv, layer, lam_init):
    n_seq, n_pages = page_table.shape
    n_new = ka_n.shape[0] // (n_seq * N_HEADS)
    assert 2 * N_HEADS * n_new == QCOLS
    g = PAGES_PER_STEP
    n_kv = n_pages // g
    past = n_pages * PAGE_SIZE

    def cache_spec(gi, is_value):
        def imap(b, s, pt):
            step = jnp.maximum(s - n_kv, 0) if is_value else jnp.minimum(s, n_kv - 1)
            return (layer, pt[b * n_pages + step * g + gi], 0, 0)
        return pl.BlockSpec((None, None, PAGE_SIZE * N_HEADS, HEAD_DIM), imap)

    vec64 = pl.BlockSpec((1, DF_HALF), lambda b, s, pt: (0, 0))
    new_spec = pl.BlockSpec((n_new * N_HEADS, HEAD_DIM), lambda b, s, pt: (b, 0))
    out_spec = pl.BlockSpec((n_new, WIDTH), lambda b, s, pt: (b, 0))
    in_specs = [pl.BlockSpec((1, WIDTH, QCOLS), lambda b, s, pt: (b, 0, 0)),
                pl.BlockSpec((1, WIDTH, QCOLS), lambda b, s, pt: (b, 0, 0)),
                new_spec, new_spec, new_spec, new_spec,
                pl.BlockSpec((PAGE_SIZE, QCOLS), lambda b, s, pt: (0, 0)),
                pl.BlockSpec((n_new, QCOLS), lambda b, s, pt: (0, 0)),
                vec64, vec64, vec64, vec64,
                pl.BlockSpec((1, HEAD_DIM), lambda b, s, pt: (0, 0))]
    in_specs += [cache_spec(gi, False) for gi in range(g)] * 2
    in_specs += [cache_spec(gi, True) for gi in range(g)] * 2
    caches = [c_sbk] * g + [c_dfk] * g + [c_sbv] * g + [c_dfv] * g
    grid_spec = pltpu.PrefetchScalarGridSpec(
        num_scalar_prefetch=1,
        grid=(n_seq, 2 * n_kv),
        in_specs=in_specs,
        out_specs=[out_spec, out_spec],
        scratch_shapes=[pltpu.VMEM((past, QCOLS), F32), pltpu.VMEM((past, QCOLS), F32),
                        pltpu.VMEM((past, QCOLS), BF16), pltpu.VMEM((past, QCOLS), BF16),
                        pltpu.VMEM((QCOLS, WIDTH), F32), pltpu.VMEM((QCOLS, WIDTH), F32)])
    return pl.pallas_call(
        functools.partial(_sample_attn_kernel, n_pages=n_pages, n_new=n_new, lam_init=lam_init),
        grid_spec=grid_spec,
        out_shape=[jax.ShapeDtypeStruct((n_seq * n_new, WIDTH), F32)] * 2,
        compiler_params=_cparams(("parallel", "arbitrary")),
        name="sample_attn",
    )(page_table.reshape(-1), qsb, qdf, ka_n, va_n, kb_n, vb_n, blast, bnew, *lams, subln, *caches)


def _block_diag_queries(q, n_seq, n_new, halves):
    qh = q.reshape(n_seq, n_new, N_HEADS, halves, HEAD_DIM // halves)
    eye_h = jnp.eye(N_HEADS, dtype=q.dtype)
    eye_c = jnp.eye(halves, 2, dtype=q.dtype)
    m = jnp.einsum("bqhcd,hg,ce->bhcdgeq", qh, eye_h, eye_c)
    return m.reshape(n_seq, WIDTH, QCOLS)


def _merge_kernel(x_ref, oa_ref, ob_ref, ga_ref, gb_ref, wsb_ref, wdf_ref, wout_ref, g2_ref, x1_ref, hn_ref):
    ya = jnp.dot(oa_ref[...].astype(BF16), wsb_ref[...], preferred_element_type=F32)
    yb = jnp.dot(ob_ref[...].astype(BF16), wdf_ref[...], preferred_element_type=F32)
    m = jax.nn.sigmoid(ga_ref[...].astype(F32)) * ya + jax.nn.sigmoid(gb_ref[...].astype(F32)) * yb
    x1 = x_ref[...] + jnp.dot(m.astype(BF16), wout_ref[...], preferred_element_type=F32)
    x1_ref[...] = x1
    ms = jnp.mean(x1 * x1, axis=-1, keepdims=True)
    hn_ref[...] = (x1 * lax.rsqrt(ms + EPS) * g2_ref[...]).astype(BF16)


def _merge(x, oa, ob, gates, wsb, wdf, wout, norm2, tm):
    m, d = x.shape
    const = lambda shape: pl.BlockSpec(shape, lambda i: (0, 0), pipeline_mode=pl.Buffered(1))
    return pl.pallas_call(
        _merge_kernel,
        grid=(m // tm,),
        in_specs=[pl.BlockSpec((tm, d), lambda i: (i, 0)),
                  pl.BlockSpec((tm, WIDTH), lambda i: (i, 0)),
                  pl.BlockSpec((tm, WIDTH), lambda i: (i, 0)),
                  pl.BlockSpec((tm, d), lambda i: (i, 0)),
                  pl.BlockSpec((tm, d), lambda i: (i, 1)),
                  const((WIDTH, d)), const((WIDTH, d)), const((d, d)),
                  pl.BlockSpec((1, d), lambda i: (0, 0))],
        out_specs=[pl.BlockSpec((tm, d), lambda i: (i, 0)), pl.BlockSpec((tm, d), lambda i: (i, 0))],
        out_shape=[jax.ShapeDtypeStruct((m, d), F32), jax.ShapeDtypeStruct((m, d), BF16)],
        compiler_params=_cparams(("parallel",)),
        name="merge",
    )(x, oa, ob, gates, gates, wsb, wdf, wout, norm2.reshape(1, d))


def _mlp_kernel(x1_ref, hn_ref, wup_ref, wdn_ref, o_ref):
    f = pl.program_id(1)

    @pl.when(f == 0)
    def _():
        o_ref[...] = x1_ref[...]

    u = jnp.maximum(jnp.dot(hn_ref[...], wup_ref[...], preferred_element_type=F32), 0.0)
    o_ref[...] += jnp.dot((u * u).astype(BF16), wdn_ref[...], preferred_element_type=F32)


def _mlp(x1, hn, wup, wdn, tm, tf):
    m, d = x1.shape
    dff = wup.shape[1]
    return pl.pallas_call(
        _mlp_kernel,
        grid=(m // tm, dff // tf),
        in_specs=[pl.BlockSpec((tm, d), lambda i, f: (i, 0)),
                  pl.BlockSpec((tm, d), lambda i, f: (i, 0)),
                  pl.BlockSpec((d, tf), lambda i, f: (0, f)),
                  pl.BlockSpec((tf, d), lambda i, f: (f, 0))],
        out_specs=pl.BlockSpec((tm, d), lambda i, f: (i, 0)),
        out_shape=jax.ShapeDtypeStruct((m, d), F32),
        compiler_params=_cparams(("parallel", "arbitrary")),
        name="mlp",
    )(x1, hn, wup, wdn)


def _project_group(x, norm1, w_in, q_norm, k_norm, tm):
    h = _rmsnorm_bf16(x, norm1, min(tm, 512))
    (qa,) = _proj(h, w_in, COL_QA, 1, tm=tm, scale=SB_SCALE)
    ka, ka16 = _proj(h, w_in, COL_KA, 1, tm=tm, want_f32=True)
    va, va16 = _proj(h, w_in, COL_VA, 1, tm=tm, want_f32=True)
    (qb,) = _proj(h, w_in, COL_QB, 1, tm=tm, qknorm=True, gain=q_norm, scale=DF_SCALE)
    kb, kb16 = _proj(h, w_in, COL_KB, 1, tm=tm, qknorm=True, gain=k_norm, want_f32=True)
    vb, vb16 = _proj(h, w_in, COL_VB, 1, tm=tm, want_f32=True)
    (gates,) = _proj(h, w_in, COL_GA, 4, tm=tm)
    return dict(qa=qa, ka=ka, ka16=ka16, va=va, va16=va16, qb=qb, kb=kb, kb16=kb16, vb=vb, vb16=vb16, gates=gates)


def kernel(x_prompt, x_sample, cache_sb_k, cache_sb_v, cache_df_k, cache_df_v, page_table, rel_bias, norm1, w_in, q_norm, k_norm, lambda_q1, lambda_k1, lambda_q2, lambda_k2, subln, w_branch_sb, w_branch_df, w_out, norm2, w_up, w_down):
    depth = norm1.shape[0]
    batch, seq, d = x_prompt.shape
    n_seq, n_new, _ = x_sample.shape
    n_pages = page_table.shape[1]
    past = n_pages * PAGE_SIZE
    xp = x_prompt.reshape(batch * seq, d)
    xs = x_sample.reshape(n_seq * n_new, d)
    bias_p = _bias_prompt(rel_bias, ATT_T)
    bias_last, bias_new = _bias_sample(rel_bias, past, n_new)
    n_pool = cache_sb_k.shape[1]
    leaves = [[] for _ in range(8)]
    for l in range(depth):
        lam_init = 0.8 - 0.6 * math.exp(-0.3 * l)
        w_in16 = w_in[l].astype(BF16)
        wsb16 = w_branch_sb[l].astype(BF16)
        wdf16 = w_branch_df[l].astype(BF16)
        wout16 = w_out[l].astype(BF16)
        wup16 = w_up[l].astype(BF16)
        wdn16 = w_down[l].astype(BF16)
        lams = [v[l].reshape(1, DF_HALF).astype(F32) for v in (lambda_q1, lambda_k1, lambda_q2, lambda_k2)]
        sub = subln[l].reshape(1, HEAD_DIM).astype(F32)

        p = _project_group(xp, norm1[l], w_in16, q_norm[l], k_norm[l], 1024)
        oa = _sb_prompt(p["qa"], p["ka16"], p["va16"], batch, seq, ATT_T)
        ob = _df_prompt(p["qb"], p["kb16"], p["vb16"], bias_p, lams, sub, lam_init, batch, seq, ATT_T)
        x1, hn = _merge(xp, oa, ob, p["gates"], wsb16, wdf16, wout16, norm2[l], 256)
        xp = _mlp(x1, hn, wup16, wdn16, 512, 1024)
        for i, name in enumerate(("ka", "va", "kb", "vb")):
            leaves[i].append(p[name].reshape(batch, seq, N_HEADS, HEAD_DIM))

        s = _project_group(xs, norm1[l], w_in16, q_norm[l], k_norm[l], 1024)
        qsb = _block_diag_queries(s["qa"], n_seq, n_new, 1)
        qdf = _block_diag_queries(s["qb"], n_seq, n_new, 2)
        shape4 = (depth, n_pool, PAGE_SIZE * N_HEADS, HEAD_DIM)
        oa, ob = _sample_attn(page_table, qsb, qdf, s["ka"], s["va"], s["kb"], s["vb"], bias_last, bias_new,
                              lams, sub, cache_sb_k.reshape(shape4), cache_sb_v.reshape(shape4),
                              cache_df_k.reshape(shape4), cache_df_v.reshape(shape4), l, lam_init)
        x1, hn = _merge(xs, oa, ob, s["gates"], wsb16, wdf16, wout16, norm2[l], 256)
        xs = _mlp(x1, hn, wup16, wdn16, 512, 1024)
        for i, name in enumerate(("ka", "va", "kb", "vb")):
            leaves[4 + i].append(s[name].reshape(n_seq, n_new, N_HEADS, HEAD_DIM))

    return (xp.reshape(batch, seq, d), xs.reshape(n_seq, n_new, d)) + tuple(jnp.stack(v) for v in leaves)
```

```python
import functools
import math

import numpy as np
import jax
import jax.numpy as jnp
from jax import lax
from jax.experimental import pallas as pl
from jax.experimental.pallas import tpu as pltpu

F32 = jnp.float32
BF16 = jnp.bfloat16

D_MODEL = 2048
N_HEADS = 8
HEAD_DIM = 128
DF_HALF = 64
WIDTH = N_HEADS * HEAD_DIM
SB_SCALE = 1.0 / math.sqrt(HEAD_DIM)
DF_SCALE = 1.0 / math.sqrt(DF_HALF)
D_FF = 4 * D_MODEL
N_BUCKETS = 32
MAX_EXACT = N_BUCKETS // 2
MAX_DISTANCE = 128
EPS = 1e-6
PAGE_SIZE = 128
NEG_BIG = -1e30

COL_QA, COL_KA, COL_VA, COL_QB, COL_KB, COL_VB, COL_GA, COL_GB = 0, 1, 2, 3, 4, 5, 6, 8

ATT_T = 512
ATT_HEADS = 2
PAGES_PER_STEP = 8
QCOLS = 128

VMEM_LIMIT = 56 * 1024 * 1024


def _cparams(sem):
    return pltpu.CompilerParams(dimension_semantics=sem, vmem_limit_bytes=VMEM_LIMIT)


def _tokens_by_width(ref, n_tok):
    return jnp.concatenate([ref[pl.ds(h, n_tok, stride=N_HEADS), :] for h in range(N_HEADS)], axis=1)


def _softplus(z):
    return jnp.maximum(z, 0.0) + jnp.log(1.0 + jnp.exp(-jnp.abs(z)))


def _rmsnorm_kernel(x_ref, g_ref, o_ref):
    x = x_ref[...]
    ms = jnp.mean(x * x, axis=-1, keepdims=True)
    o_ref[...] = (x * lax.rsqrt(ms + EPS) * g_ref[...]).astype(o_ref.dtype)


def _rmsnorm_bf16(x, g, tm):
    m, d = x.shape
    return pl.pallas_call(
        _rmsnorm_kernel,
        grid=(m // tm,),
        in_specs=[pl.BlockSpec((tm, d), lambda i: (i, 0)),
                  pl.BlockSpec((1, d), lambda i: (0, 0))],
        out_specs=pl.BlockSpec((tm, d), lambda i: (i, 0)),
        out_shape=jax.ShapeDtypeStruct((m, d), BF16),
        compiler_params=_cparams(("parallel",)),
        name="rmsnorm_bf16",
    )(x, g.reshape(1, d))


def _proj_kernel(*refs, qknorm, scale, want_f32, want_bf16):
    h_ref, w_ref = refs[0], refs[1]
    pos = 2
    if qknorm:
        gain_ref, pmat_ref = refs[2], refs[3]
        pos = 4
    outs = refs[pos:]
    y = jnp.dot(h_ref[...], w_ref[...], preferred_element_type=F32)
    if qknorm:
        tn = y.shape[1]
        pieces = []
        for c in range(tn // 256):
            yb = y[:, c * 256:(c + 1) * 256]
            ms = jnp.dot((yb * yb).astype(BF16), pmat_ref[...], preferred_element_type=F32)
            pieces.append(yb * lax.rsqrt(ms + EPS))
        y = jnp.concatenate(pieces, axis=1) * gain_ref[...]
    k = 0
    if want_f32:
        for hd in range(N_HEADS):
            outs[k][pl.ds(hd, y.shape[0], stride=N_HEADS), :] = y[:, hd * HEAD_DIM:(hd + 1) * HEAD_DIM]
        k += 1
    if want_bf16:
        outs[k][...] = (y * scale).astype(BF16) if scale != 1.0 else y.astype(BF16)


def _proj(h, w, col0, ncol, *, tm, qknorm=False, gain=None, scale=1.0, want_f32=False, want_bf16=True):
    m, kdim = h.shape
    tn = WIDTH
    in_specs = [pl.BlockSpec((tm, kdim), lambda n, i: (i, 0)),
                pl.BlockSpec((kdim, tn), lambda n, i: (0, col0 + n))]
    args = [h, w]
    if qknorm:
        pmat = np.kron(np.eye(256 // DF_HALF), np.full((DF_HALF, DF_HALF), 1.0 / DF_HALF)).astype(np.float32)
        in_specs += [pl.BlockSpec((1, tn), lambda n, i: (0, 0)),
                     pl.BlockSpec((256, 256), lambda n, i: (0, 0))]
        args += [jnp.tile(gain.astype(F32), tn // DF_HALF).reshape(1, tn), jnp.asarray(pmat, dtype=BF16)]
    out_specs, out_shape = [], []
    if want_f32:
        assert ncol == 1
        out_specs.append(pl.BlockSpec((tm * N_HEADS, HEAD_DIM), lambda n, i: (i, 0)))
        out_shape.append(jax.ShapeDtypeStruct((m * N_HEADS, HEAD_DIM), F32))
    if want_bf16:
        out_specs.append(pl.BlockSpec((tm, tn), lambda n, i: (i, n)))
        out_shape.append(jax.ShapeDtypeStruct((m, ncol * tn), BF16))
    res = pl.pallas_call(
        functools.partial(_proj_kernel, qknorm=qknorm, scale=scale, want_f32=want_f32, want_bf16=want_bf16),
        grid=(ncol, m // tm),
        in_specs=in_specs,
        out_specs=out_specs,
        out_shape=out_shape,
        compiler_params=_cparams(("parallel", "parallel")),
        name="in_proj",
    )(*args)
    return res


def _bucket(rel):
    n = jnp.maximum(rel, 0)
    nf = jnp.maximum(n, 1).astype(F32)
    large = MAX_EXACT + (jnp.log(nf / MAX_EXACT) / math.log(MAX_DISTANCE / MAX_EXACT)
                         * (N_BUCKETS - MAX_EXACT)).astype(jnp.int32)
    large = jnp.minimum(large, N_BUCKETS - 1)
    return jnp.where(n < MAX_EXACT, n, large)


def _bias_prompt_kernel(rb_ref, o_ref, *, t):
    h = pl.program_id(0)
    keys = lax.broadcasted_iota(jnp.int32, (t, t), 0)
    qrys = lax.broadcasted_iota(jnp.int32, (t, t), 1)
    far = rb_ref[(N_BUCKETS - 1) * N_HEADS + h]
    for d in range(2):
        b = _bucket(qrys - keys + d * t)
        acc = jnp.zeros((t, t), F32)
        for k in range(N_BUCKETS - 1):
            acc = jnp.where(b == k, rb_ref[k * N_HEADS + h] - far, acc)
        if d == 0:
            acc = jnp.where(keys <= qrys, acc, NEG_BIG)
        o_ref[0, d] = acc


def _bias_prompt(rel_bias, t):
    return pl.pallas_call(
        functools.partial(_bias_prompt_kernel, t=t),
        grid=(N_HEADS,),
        in_specs=[pl.BlockSpec(memory_space=pltpu.SMEM)],
        out_specs=pl.BlockSpec((1, 2, t, t), lambda h: (h, 0, 0, 0)),
        out_shape=jax.ShapeDtypeStruct((N_HEADS, 2, t, t), F32),
        compiler_params=_cparams(("parallel",)),
        name="bias_prompt",
    )(rel_bias.reshape(-1))


def _bias_sample_kernel(rb_ref, last_ref, new_ref, *, past, n_new):
    def tile(rows, key0):
        lane = lax.broadcasted_iota(jnp.int32, (rows, QCOLS), 1)
        key = lax.broadcasted_iota(jnp.int32, (rows, QCOLS), 0) + key0
        head = lax.shift_right_logical(lane, (2 * n_new).bit_length() - 1)
        rel = past + (lane & (n_new - 1)) - key
        b = _bucket(rel)
        acc = jnp.zeros((rows, QCOLS), F32)
        for hh in range(N_HEADS):
            far = rb_ref[(N_BUCKETS - 1) * N_HEADS + hh]
            for k in range(N_BUCKETS - 1):
                acc = jnp.where((b == k) & (head == hh), rb_ref[k * N_HEADS + hh] - far, acc)
        return acc
    last_ref[...] = tile(PAGE_SIZE, past - PAGE_SIZE)
    new_ref[...] = tile(n_new, past)


def _bias_sample(rel_bias, past, n_new):
    return pl.pallas_call(
        functools.partial(_bias_sample_kernel, past=past, n_new=n_new),
        in_specs=[pl.BlockSpec(memory_space=pltpu.SMEM)],
        out_specs=[pl.BlockSpec((PAGE_SIZE, QCOLS), lambda: (0, 0)),
                   pl.BlockSpec((n_new, QCOLS), lambda: (0, 0))],
        out_shape=[jax.ShapeDtypeStruct((PAGE_SIZE, QCOLS), F32),
                   jax.ShapeDtypeStruct((n_new, QCOLS), F32)],
        name="bias_sample",
    )(rel_bias.reshape(-1))


def _lambda(lq1, lk1, lq2, lk2, lam_init):
    s1 = jnp.sum(lq1 * lk1, axis=-1, keepdims=True)
    s2 = jnp.sum(lq2 * lk2, axis=-1, keepdims=True)
    return jnp.exp(s1) - jnp.exp(s2) + lam_init


def _store_transposed(dst_ref, src_ref, t, row0=0):
    for ci in range(src_ref.shape[0] // t):
        chunk = src_ref[ci * t:(ci + 1) * t, :].astype(F32)
        dst_ref[row0:row0 + HEAD_DIM, ci * t:(ci + 1) * t] = chunk.T.astype(dst_ref.dtype)


def _head_cols(hh):
    return slice(hh * HEAD_DIM, (hh + 1) * HEAD_DIM)


def _sb_prompt_kernel(q_ref, k_ref, v_ref, upper_ref, o_ref, vt_sc, *, t):
    qi = pl.program_id(2)
    heads = range(ATT_HEADS)

    @pl.when(qi == 0)
    def _():
        for hh in heads:
            _store_transposed(vt_sc.at[hh], v_ref.at[:, _head_cols(hh)], t)

    qs = [q_ref[:, _head_cols(hh)] for hh in heads]
    keys = lax.broadcasted_iota(jnp.int32, (t, t), 0)
    qrys = lax.broadcasted_iota(jnp.int32, (t, t), 1)
    causal = keys < qrys

    def block(j, carry, masked):
        start = pl.multiple_of(j * t, t)
        zs = [lax.dot_general(k_ref[pl.ds(start, t), _head_cols(hh)], qs[hh], (((1,), (1,)), ((), ())),
                              preferred_element_type=F32) for hh in heads]
        sps = []
        for hh in heads:
            sp = _softplus(zs[hh])
            sps.append(jnp.where(causal, sp, 0.0) if masked else sp)
        sufs = [jnp.dot(upper_ref[...], sps[hh].astype(BF16), preferred_element_type=F32) for hh in heads]
        out = []
        for hh in heads:
            acc, c = carry[hh]
            log_a = zs[hh] - sps[hh] - sufs[hh] - c
            if masked:
                log_a = jnp.where(causal, log_a, NEG_BIG)
            a = jnp.exp(log_a)
            acc = acc + jnp.dot(vt_sc[hh, :, pl.ds(start, t)], a.astype(BF16), preferred_element_type=F32)
            c = c + jnp.sum(sps[hh], axis=0, keepdims=True)
            out.append((acc, c))
        return tuple(out)

    carry = tuple((jnp.zeros((HEAD_DIM, t), F32), jnp.zeros((1, t), F32)) for _ in heads)
    carry = block(qi, carry, True)
    carry = lax.fori_loop(0, qi, lambda i, cr: block(qi - 1 - i, cr, False), carry)
    for hh in heads:
        o_ref[:, _head_cols(hh)] = carry[hh][0].T.astype(o_ref.dtype)


def _sb_prompt(q, k, v, batch, seq, t):
    nq = seq // t
    w = ATT_HEADS * HEAD_DIM
    upper = jnp.asarray(np.triu(np.ones((t, t), np.float32), 1), dtype=BF16)
    return pl.pallas_call(
        functools.partial(_sb_prompt_kernel, t=t),
        grid=(batch, N_HEADS // ATT_HEADS, nq),
        in_specs=[pl.BlockSpec((t, w), lambda b, h, i: (b * nq + i, h)),
                  pl.BlockSpec((seq, w), lambda b, h, i: (b, h)),
                  pl.BlockSpec((seq, w), lambda b, h, i: (b, h)),
                  pl.BlockSpec((t, t), lambda b, h, i: (0, 0))],
        out_specs=pl.BlockSpec((t, w), lambda b, h, i: (b * nq + i, h)),
        out_shape=jax.ShapeDtypeStruct((batch * seq, WIDTH), BF16),
        scratch_shapes=[pltpu.VMEM((ATT_HEADS, HEAD_DIM, seq), BF16)],
        compiler_params=_cparams(("parallel", "parallel", "arbitrary")),
        name="sb_prompt",
    )(q, k, v, upper)


def _df_prompt_kernel(q_ref, k_ref, v_ref, bias_ref, lq1, lk1, lq2, lk2, g_ref, o_ref,
                      vt_sc, m_sc, l_sc, acc_sc, *, t, lam_init):
    qi = pl.program_id(2)
    heads = range(ATT_HEADS)

    @pl.when(qi == 0)
    def _():
        for hh in heads:
            _store_transposed(vt_sc.at[hh], v_ref.at[:, _head_cols(hh)], t)
            vt_sc[hh, HEAD_DIM:, :] = jnp.ones((HEAD_DIM, vt_sc.shape[2]), BF16)

    lane = lax.broadcasted_iota(jnp.int32, (t, HEAD_DIM), 1)
    qs = []
    for hh in heads:
        q = q_ref[:, _head_cols(hh)]
        zero = jnp.zeros_like(q)
        qs.append(jnp.concatenate([jnp.where(lane < DF_HALF, q, zero), jnp.where(lane >= DF_HALF, q, zero)], axis=0))

    def block(j, bias_d, first):
        start = pl.multiple_of(j * t, t)
        zs = [lax.dot_general(k_ref[pl.ds(start, t), _head_cols(hh)], qs[hh], (((1,), (1,)), ((), ())),
                              preferred_element_type=F32) for hh in heads]
        for hh in heads:
            z = zs[hh]
            if bias_d is not None:
                bias = bias_ref[hh, bias_d]
                z = z + jnp.concatenate([bias, bias], axis=1)
            zmax = jnp.max(z, axis=0, keepdims=True)
            m_new = zmax if first else jnp.maximum(m_sc[hh], zmax)
            p = jnp.exp(z - m_new).astype(BF16)
            pv = jnp.dot(vt_sc[hh, :, pl.ds(start, t)], p, preferred_element_type=F32)
            if first:
                l_sc[hh] = pv[HEAD_DIM:, :]
                acc_sc[hh] = pv[:HEAD_DIM, :]
            else:
                alpha = jnp.exp(m_sc[hh] - m_new)
                l_sc[hh] = alpha * l_sc[hh] + pv[HEAD_DIM:, :]
                acc_sc[hh] = alpha * acc_sc[hh] + pv[:HEAD_DIM, :]
            m_sc[hh] = m_new

    block(qi, 0, True)

    @pl.when(qi >= 1)
    def _():
        block(qi - 1, 1, False)

    def far(i, carry):
        block(qi - 2 - i, None, False)
        return carry
    lax.fori_loop(0, jnp.maximum(qi - 1, 0), far, 0)

    lam = _lambda(lq1[...], lk1[...], lq2[...], lk2[...], lam_init)
    for hh in heads:
        o = acc_sc[hh] / l_sc[hh]
        o = (o[:, :t] - lam * o[:, t:]).T
        ms = jnp.mean(o * o, axis=-1, keepdims=True)
        o_ref[:, _head_cols(hh)] = (o * lax.rsqrt(ms + EPS) * g_ref[...] * (1.0 - lam_init)).astype(o_ref.dtype)


def _df_prompt(q, k, v, bias, lams, subln, lam_init, batch, seq, t):
    nq = seq // t
    w = ATT_HEADS * HEAD_DIM
    vec64 = pl.BlockSpec((1, DF_HALF), lambda b, h, i: (0, 0))
    return pl.pallas_call(
        functools.partial(_df_prompt_kernel, t=t, lam_init=lam_init),
        grid=(batch, N_HEADS // ATT_HEADS, nq),
        in_specs=[pl.BlockSpec((t, w), lambda b, h, i: (b * nq + i, h)),
                  pl.BlockSpec((seq, w), lambda b, h, i: (b, h)),
                  pl.BlockSpec((seq, w), lambda b, h, i: (b, h)),
                  pl.BlockSpec((ATT_HEADS, 2, t, t), lambda b, h, i: (h, 0, 0, 0)),
                  vec64, vec64, vec64, vec64,
                  pl.BlockSpec((1, HEAD_DIM), lambda b, h, i: (0, 0))],
        out_specs=pl.BlockSpec((t, w), lambda b, h, i: (b * nq + i, h)),
        out_shape=jax.ShapeDtypeStruct((batch * seq, WIDTH), BF16),
        scratch_shapes=[pltpu.VMEM((ATT_HEADS, 2 * HEAD_DIM, seq), BF16),
                        pltpu.VMEM((ATT_HEADS, 1, 2 * t), F32), pltpu.VMEM((ATT_HEADS, HEAD_DIM, 2 * t), F32),
                        pltpu.VMEM((ATT_HEADS, HEAD_DIM, 2 * t), F32)],
        compiler_params=_cparams(("parallel", "parallel", "arbitrary")),
        name="df_prompt",
    )(q, k, v, bias, *lams, subln)


def _sample_attn_kernel(pt_ref, qsb_ref, qdf_ref, kan_ref, van_ref, kbn_ref, vbn_ref,
                        blast_ref, bnew_ref, lq1, lk1, lq2, lk2, g_ref, *rest,
                        n_pages, n_new, lam_init):
    g = PAGES_PER_STEP
    ksb = rest[0:g]
    kdf = rest[g:2 * g]
    vsb = rest[2 * g:3 * g]
    vdf = rest[3 * g:4 * g]
    oa_ref, ob_ref = rest[4 * g], rest[4 * g + 1]
    zsb_sc, zdf_sc, psb_sc, pdf_sc, accsb_sc, accdf_sc = rest[4 * g + 2:]
    s = pl.program_id(1)
    n_kv = n_pages // g
    past = n_pages * PAGE_SIZE
    chunk = g * PAGE_SIZE

    def pages_bf16(refs):
        return jnp.concatenate([_tokens_by_width(r, PAGE_SIZE).astype(BF16) for r in refs], axis=0)

    def new_tokens(ref):
        return _tokens_by_width(ref, n_new)

    @pl.when(s < n_kv)
    def _scores():
        start = pl.multiple_of(s * chunk, chunk)
        zsb_sc[pl.ds(start, chunk), :] = jnp.dot(pages_bf16(ksb), qsb_ref[0], preferred_element_type=F32)
        zdf_sc[pl.ds(start, chunk), :] = jnp.dot(pages_bf16(kdf), qdf_ref[0], preferred_element_type=F32)

    def pad_rows_bf16(x):
        return jnp.concatenate([x, jnp.zeros_like(x)], axis=0).astype(BF16)

    @pl.when(s == n_kv - 1)
    def _weights():
        lane_n = lax.broadcasted_iota(jnp.int32, (n_new, QCOLS), 1)
        key_n = lax.broadcasted_iota(jnp.int32, (n_new, QCOLS), 0)
        qpos_n = lane_n & (n_new - 1)
        z_new = jnp.dot(pad_rows_bf16(new_tokens(kbn_ref)), qdf_ref[0], preferred_element_type=F32)[:n_new]
        z_new = z_new + bnew_ref[...]
        z_new = jnp.where(key_n <= qpos_n, z_new, NEG_BIG)
        last0 = past - PAGE_SIZE
        zdf_sc[last0:past, :] = zdf_sc[last0:past, :] + blast_ref[...]
        zp = zdf_sc[...]
        mx = jnp.maximum(jnp.max(zp, axis=0, keepdims=True), jnp.max(z_new, axis=0, keepdims=True))
        e_new = jnp.exp(z_new - mx)
        ep = jnp.exp(zp - mx)
        inv = 1.0 / (jnp.sum(ep, axis=0, keepdims=True) + jnp.sum(e_new, axis=0, keepdims=True))
        pdf_sc[...] = (ep * inv).astype(BF16)
        accdf_sc[...] = lax.dot_general(pad_rows_bf16(e_new * inv), pad_rows_bf16(new_tokens(vbn_ref)),
                                        (((0,), (0,)), ((), ())), preferred_element_type=F32)
        zs_new = jnp.dot(pad_rows_bf16(new_tokens(kan_ref)), qsb_ref[0], preferred_element_type=F32)[:n_new]
        strict = key_n < qpos_n
        sp_new = jnp.where(strict, _softplus(zs_new), 0.0)
        carry = jnp.zeros((1, QCOLS), F32)
        suffix_rows = [None] * n_new
        for i in range(n_new - 1, -1, -1):
            suffix_rows[i] = carry
            carry = carry + sp_new[i:i + 1]
        suffix_new = jnp.concatenate(suffix_rows, axis=0)
        a_new = jnp.exp(jnp.where(strict, zs_new - sp_new - suffix_new, NEG_BIG))
        accsb_sc[...] = lax.dot_general(pad_rows_bf16(a_new), pad_rows_bf16(new_tokens(van_ref)),
                                        (((0,), (0,)), ((), ())), preferred_element_type=F32)
        tc = 256
        rr = lax.broadcasted_iota(jnp.int32, (tc, tc), 0)
        cc = lax.broadcasted_iota(jnp.int32, (tc, tc), 1)
        upper = jnp.where(cc > rr, 1.0, 0.0).astype(BF16)
        for ci in range(past // tc - 1, -1, -1):
            zc = zsb_sc[ci * tc:(ci + 1) * tc, :]
            sp = _softplus(zc)
            sp_hi = sp.astype(BF16)
            sp_lo = (sp - sp_hi.astype(F32)).astype(BF16)
            suffix = (jnp.dot(upper, sp_hi, preferred_element_type=F32)
                      + jnp.dot(upper, sp_lo, preferred_element_type=F32))
            psb_sc[ci * tc:(ci + 1) * tc, :] = jnp.exp(zc - sp - suffix - carry).astype(BF16)
            carry = carry + jnp.sum(sp, axis=0, keepdims=True)

    @pl.when(s >= n_kv)
    def _values():
        start = pl.multiple_of((s - n_kv) * chunk, chunk)
        accsb_sc[...] += lax.dot_general(psb_sc[pl.ds(start, chunk), :], pages_bf16(vsb),
                                         (((0,), (0,)), ((), ())), preferred_element_type=F32)
        accdf_sc[...] += lax.dot_general(pdf_sc[pl.ds(start, chunk), :], pages_bf16(vdf),
                                         (((0,), (0,)), ((), ())), preferred_element_type=F32)

    @pl.when(s == 2 * n_kv - 1)
    def _finish():
        lam = _lambda(lq1[...], lk1[...], lq2[...], lk2[...], lam_init)
        for h in range(N_HEADS):
            cols = slice(h * HEAD_DIM, (h + 1) * HEAD_DIM)
            oa_ref[:, cols] = accsb_sc[h * 2 * n_new:h * 2 * n_new + n_new, cols]
            o1 = accdf_sc[h * 2 * n_new:h * 2 * n_new + n_new, cols]
            o2 = accdf_sc[h * 2 * n_new + n_new:(h + 1) * 2 * n_new, cols]
            o = o1 - lam * o2
            ms = jnp.mean(o * o, axis=-1, keepdims=True)
            ob_ref[:, cols] = o * lax.rsqrt(ms + EPS) * g_ref[...] * (1.0 - lam_init)


def _sample_attn(page_table, qsb, qdf, ka_n, va_n, kb_n, vb_n, blast, bnew, lams, subln,
                 c_sbk, c_sbv, c_dfk, c_dfv, layer, lam_init):
    n_seq, n_pages = page_table.shape
    n_new = ka_n.shape[0] // (n_seq * N_HEADS)
    assert 2 * N_HEADS * n_new == QCOLS
    g = PAGES_PER_STEP
    n_kv = n_pages // g
    past = n_pages * PAGE_SIZE

    def cache_spec(gi, is_value):
        def imap(b, s, pt):
            if is_value:
                in_phase = s >= n_kv
                seq_i = jnp.where(in_phase, b, jnp.maximum(b - 1, 0))
                step = jnp.where(in_phase, s - n_kv, n_kv - 1)
            else:
                seq_i = b
                step = jnp.minimum(s, n_kv - 1)
            return (layer, pt[seq_i * n_pages + step * g + gi], 0, 0)
        return pl.BlockSpec((None, None, PAGE_SIZE * N_HEADS, HEAD_DIM), imap)

    vec64 = pl.BlockSpec((1, DF_HALF), lambda b, s, pt: (0, 0))
    new_spec = pl.BlockSpec((n_new * N_HEADS, HEAD_DIM), lambda b, s, pt: (b, 0))
    out_spec = pl.BlockSpec((n_new, WIDTH), lambda b, s, pt: (b, 0))
    in_specs = [pl.BlockSpec((1, WIDTH, QCOLS), lambda b, s, pt: (b, 0, 0)),
                pl.BlockSpec((1, WIDTH, QCOLS), lambda b, s, pt: (b, 0, 0)),
                new_spec, new_spec, new_spec, new_spec,
                pl.BlockSpec((PAGE_SIZE, QCOLS), lambda b, s, pt: (0, 0)),
                pl.BlockSpec((n_new, QCOLS), lambda b, s, pt: (0, 0)),
                vec64, vec64, vec64, vec64,
                pl.BlockSpec((1, HEAD_DIM), lambda b, s, pt: (0, 0))]
    in_specs += [cache_spec(gi, False) for gi in range(g)] * 2
    in_specs += [cache_spec(gi, True) for gi in range(g)] * 2
    caches = [c_sbk] * g + [c_dfk] * g + [c_sbv] * g + [c_dfv] * g
    grid_spec = pltpu.PrefetchScalarGridSpec(
        num_scalar_prefetch=1,
        grid=(n_seq, 2 * n_kv),
        in_specs=in_specs,
        out_specs=[out_spec, out_spec],
        scratch_shapes=[pltpu.VMEM((past, QCOLS), F32), pltpu.VMEM((past, QCOLS), F32),
                        pltpu.VMEM((past, QCOLS), BF16), pltpu.VMEM((past, QCOLS), BF16),
                        pltpu.VMEM((QCOLS, WIDTH), F32), pltpu.VMEM((QCOLS, WIDTH), F32)])
    return pl.pallas_call(
        functools.partial(_sample_attn_kernel, n_pages=n_pages, n_new=n_new, lam_init=lam_init),
        grid_spec=grid_spec,
        out_shape=[jax.ShapeDtypeStruct((n_seq * n_new, WIDTH), F32)] * 2,
        compiler_params=_cparams(("parallel", "arbitrary")),
        name="sample_attn",
    )(page_table.reshape(-1), qsb, qdf, ka_n, va_n, kb_n, vb_n, blast, bnew, *lams, subln, *caches)


def _block_diag_queries(q, n_seq, n_new, halves):
    qh = q.reshape(n_seq, n_new, N_HEADS, halves, HEAD_DIM // halves)
    eye_h = jnp.eye(N_HEADS, dtype=q.dtype)
    eye_c = jnp.eye(halves, 2, dtype=q.dtype)
    m = jnp.einsum("bqhcd,hg,ce->bhcdgeq", qh, eye_h, eye_c)
    return m.reshape(n_seq, WIDTH, QCOLS)


def _merge_kernel(x_ref, oa_ref, ob_ref, ga_ref, gb_ref, wsb_ref, wdf_ref, wout_ref, g2_ref, x1_ref, hn_ref):
    ya = jnp.dot(oa_ref[...].astype(BF16), wsb_ref[...], preferred_element_type=F32)
    yb = jnp.dot(ob_ref[...].astype(BF16), wdf_ref[...], preferred_element_type=F32)
    m = jax.nn.sigmoid(ga_ref[...].astype(F32)) * ya + jax.nn.sigmoid(gb_ref[...].astype(F32)) * yb
    x1 = x_ref[...] + jnp.dot(m.astype(BF16), wout_ref[...], preferred_element_type=F32)
    x1_ref[...] = x1
    ms = jnp.mean(x1 * x1, axis=-1, keepdims=True)
    hn_ref[...] = (x1 * lax.rsqrt(ms + EPS) * g2_ref[...]).astype(BF16)


def _merge(x, oa, ob, gates, wsb, wdf, wout, norm2, tm):
    m, d = x.shape
    const = lambda shape: pl.BlockSpec(shape, lambda i: (0, 0), pipeline_mode=pl.Buffered(1))
    return pl.pallas_call(
        _merge_kernel,
        grid=(m // tm,),
        in_specs=[pl.BlockSpec((tm, d), lambda i: (i, 0)),
                  pl.BlockSpec((tm, WIDTH), lambda i: (i, 0)),
                  pl.BlockSpec((tm, WIDTH), lambda i: (i, 0)),
                  pl.BlockSpec((tm, d), lambda i: (i, 0)),
                  pl.BlockSpec((tm, d), lambda i: (i, 1)),
                  const((WIDTH, d)), const((WIDTH, d)), const((d, d)),
                  pl.BlockSpec((1, d), lambda i: (0, 0))],
        out_specs=[pl.BlockSpec((tm, d), lambda i: (i, 0)), pl.BlockSpec((tm, d), lambda i: (i, 0))],
        out_shape=[jax.ShapeDtypeStruct((m, d), F32), jax.ShapeDtypeStruct((m, d), BF16)],
        compiler_params=_cparams(("parallel",)),
        name="merge",
    )(x, oa, ob, gates, gates, wsb, wdf, wout, norm2.reshape(1, d))


def _mlp_kernel(x1_ref, hn_ref, wup_ref, wdn_ref, o_ref):
    f = pl.program_id(1)

    @pl.when(f == 0)
    def _():
        o_ref[...] = x1_ref[...]

    u = jnp.maximum(jnp.dot(hn_ref[...], wup_ref[...], preferred_element_type=F32), 0.0)
    o_ref[...] += jnp.dot((u * u).astype(BF16), wdn_ref[...], preferred_element_type=F32)


def _mlp(x1, hn, wup, wdn, tm, tf):
    m, d = x1.shape
    dff = wup.shape[1]
    return pl.pallas_call(
        _mlp_kernel,
        grid=(m // tm, dff // tf),
        in_specs=[pl.BlockSpec((tm, d), lambda i, f: (i, 0)),
                  pl.BlockSpec((tm, d), lambda i, f: (i, 0)),
                  pl.BlockSpec((d, tf), lambda i, f: (0, f)),
                  pl.BlockSpec((tf, d), lambda i, f: (f, 0))],
        out_specs=pl.BlockSpec((tm, d), lambda i, f: (i, 0)),
        out_shape=jax.ShapeDtypeStruct((m, d), F32),
        compiler_params=_cparams(("parallel", "arbitrary")),
        name="mlp",
    )(x1, hn, wup, wdn)


def _project_group(x, norm1, w_in, q_norm, k_norm, tm):
    h = _rmsnorm_bf16(x, norm1, min(tm, 512))
    (qa,) = _proj(h, w_in, COL_QA, 1, tm=tm, scale=SB_SCALE)
    ka, ka16 = _proj(h, w_in, COL_KA, 1, tm=tm, want_f32=True)
    va, va16 = _proj(h, w_in, COL_VA, 1, tm=tm, want_f32=True)
    (qb,) = _proj(h, w_in, COL_QB, 1, tm=tm, qknorm=True, gain=q_norm, scale=DF_SCALE)
    kb, kb16 = _proj(h, w_in, COL_KB, 1, tm=tm, qknorm=True, gain=k_norm, want_f32=True)
    vb, vb16 = _proj(h, w_in, COL_VB, 1, tm=tm, want_f32=True)
    (gates,) = _proj(h, w_in, COL_GA, 4, tm=tm)
    return dict(qa=qa, ka=ka, ka16=ka16, va=va, va16=va16, qb=qb, kb=kb, kb16=kb16, vb=vb, vb16=vb16, gates=gates)


def kernel(x_prompt, x_sample, cache_sb_k, cache_sb_v, cache_df_k, cache_df_v, page_table, rel_bias, norm1, w_in, q_norm, k_norm, lambda_q1, lambda_k1, lambda_q2, lambda_k2, subln, w_branch_sb, w_branch_df, w_out, norm2, w_up, w_down):
    depth = norm1.shape[0]
    batch, seq, d = x_prompt.shape
    n_seq, n_new, _ = x_sample.shape
    n_pages = page_table.shape[1]
    past = n_pages * PAGE_SIZE
    xp = x_prompt.reshape(batch * seq, d)
    xs = x_sample.reshape(n_seq * n_new, d)
    bias_p = _bias_prompt(rel_bias, ATT_T)
    bias_last, bias_new = _bias_sample(rel_bias, past, n_new)
    n_pool = cache_sb_k.shape[1]
    leaves = [[] for _ in range(8)]
    for l in range(depth):
        lam_init = 0.8 - 0.6 * math.exp(-0.3 * l)
        w_in16 = w_in[l].astype(BF16)
        wsb16 = w_branch_sb[l].astype(BF16)
        wdf16 = w_branch_df[l].astype(BF16)
        wout16 = w_out[l].astype(BF16)
        wup16 = w_up[l].astype(BF16)
        wdn16 = w_down[l].astype(BF16)
        lams = [v[l].reshape(1, DF_HALF).astype(F32) for v in (lambda_q1, lambda_k1, lambda_q2, lambda_k2)]
        sub = subln[l].reshape(1, HEAD_DIM).astype(F32)

        p = _project_group(xp, norm1[l], w_in16, q_norm[l], k_norm[l], 1024)
        oa = _sb_prompt(p["qa"], p["ka16"], p["va16"], batch, seq, ATT_T)
        ob = _df_prompt(p["qb"], p["kb16"], p["vb16"], bias_p, lams, sub, lam_init, batch, seq, ATT_T)
        x1, hn = _merge(xp, oa, ob, p["gates"], wsb16, wdf16, wout16, norm2[l], 256)
        xp = _mlp(x1, hn, wup16, wdn16, 512, 1024)
        for i, name in enumerate(("ka", "va", "kb", "vb")):
            leaves[i].append(p[name].reshape(batch, seq, N_HEADS, HEAD_DIM))

        s = _project_group(xs, norm1[l], w_in16, q_norm[l], k_norm[l], 1024)
        qsb = _block_diag_queries(s["qa"], n_seq, n_new, 1)
        qdf = _block_diag_queries(s["qb"], n_seq, n_new, 2)
        shape4 = (depth, n_pool, PAGE_SIZE * N_HEADS, HEAD_DIM)
        oa, ob = _sample_attn(page_table, qsb, qdf, s["ka"], s["va"], s["kb"], s["vb"], bias_last, bias_new,
                              lams, sub, cache_sb_k.reshape(shape4), cache_sb_v.reshape(shape4),
                              cache_df_k.reshape(shape4), cache_df_v.reshape(shape4), l, lam_init)
        x1, hn = _merge(xs, oa, ob, s["gates"], wsb16, wdf16, wout16, norm2[l], 256)
        xs = _mlp(x1, hn, wup16, wdn16, 512, 1024)
        for i, name in enumerate(("ka", "va", "kb", "vb")):
            leaves[4 + i].append(s[name].reshape(n_seq, n_new, N_HEADS, HEAD_DIM))

    return (xp.reshape(batch, seq, d), xs.reshape(n_seq, n_new, d)) + tuple(jnp.stack(v) for v in leaves)
```

```python
import functools
import math

import numpy as np
import jax
import jax.numpy as jnp
from jax import lax
from jax.experimental import pallas as pl
from jax.experimental.pallas import tpu as pltpu

F32 = jnp.float32
BF16 = jnp.bfloat16

D_MODEL = 2048
N_HEADS = 8
HEAD_DIM = 128
DF_HALF = 64
WIDTH = N_HEADS * HEAD_DIM
SB_SCALE = 1.0 / math.sqrt(HEAD_DIM)
DF_SCALE = 1.0 / math.sqrt(DF_HALF)
D_FF = 4 * D_MODEL
N_BUCKETS = 32
MAX_EXACT = N_BUCKETS // 2
MAX_DISTANCE = 128
EPS = 1e-6
PAGE_SIZE = 128
NEG_BIG = -1e30

COL_QA, COL_KA, COL_VA, COL_QB, COL_KB, COL_VB, COL_GA, COL_GB = 0, 1, 2, 3, 4, 5, 6, 8

ATT_T = 512
ATT_HEADS = 2
PAGES_PER_STEP = 8
RING_DEPTH = 3
QCOLS = 128
QROWS = 16

VMEM_LIMIT = 56 * 1024 * 1024


def _cparams(sem):
    return pltpu.CompilerParams(dimension_semantics=sem, vmem_limit_bytes=VMEM_LIMIT)


def _tokens_by_width(ref, n_tok):
    return jnp.concatenate([ref[pl.ds(h, n_tok, stride=N_HEADS), :] for h in range(N_HEADS)], axis=1)


def _softplus(z):
    return jnp.maximum(z, 0.0) + jnp.log(1.0 + jnp.exp(-jnp.abs(z)))


def _rmsnorm_kernel(x_ref, g_ref, o_ref):
    x = x_ref[...]
    ms = jnp.mean(x * x, axis=-1, keepdims=True)
    o_ref[...] = (x * lax.rsqrt(ms + EPS) * g_ref[...]).astype(o_ref.dtype)


def _rmsnorm_bf16(x, g, tm):
    m, d = x.shape
    return pl.pallas_call(
        _rmsnorm_kernel,
        grid=(m // tm,),
        in_specs=[pl.BlockSpec((tm, d), lambda i: (i, 0)),
                  pl.BlockSpec((1, d), lambda i: (0, 0))],
        out_specs=pl.BlockSpec((tm, d), lambda i: (i, 0)),
        out_shape=jax.ShapeDtypeStruct((m, d), BF16),
        compiler_params=_cparams(("parallel",)),
        name="rmsnorm_bf16",
    )(x, g.reshape(1, d))


def _proj_kernel(*refs, qknorm, scale, want_f32, want_bf16):
    h_ref, w_ref = refs[0], refs[1]
    pos = 2
    if qknorm:
        gain_ref, pmat_ref = refs[2], refs[3]
        pos = 4
    outs = refs[pos:]
    y = jnp.dot(h_ref[...], w_ref[...], preferred_element_type=F32)
    if qknorm:
        tn = y.shape[1]
        pieces = []
        for c in range(tn // 256):
            yb = y[:, c * 256:(c + 1) * 256]
            ms = jnp.dot((yb * yb).astype(BF16), pmat_ref[...], preferred_element_type=F32)
            pieces.append(yb * lax.rsqrt(ms + EPS))
        y = jnp.concatenate(pieces, axis=1) * gain_ref[...]
    k = 0
    if want_f32:
        for hd in range(N_HEADS):
            outs[k][pl.ds(hd, y.shape[0], stride=N_HEADS), :] = y[:, hd * HEAD_DIM:(hd + 1) * HEAD_DIM]
        k += 1
    if want_bf16:
        outs[k][...] = (y * scale).astype(BF16) if scale != 1.0 else y.astype(BF16)


def _proj(h, w, col0, ncol, *, tm, qknorm=False, gain=None, scale=1.0, want_f32=False, want_bf16=True):
    m, kdim = h.shape
    tn = WIDTH
    in_specs = [pl.BlockSpec((tm, kdim), lambda n, i: (i, 0)),
                pl.BlockSpec((kdim, tn), lambda n, i: (0, col0 + n))]
    args = [h, w]
    if qknorm:
        pmat = np.kron(np.eye(256 // DF_HALF), np.full((DF_HALF, DF_HALF), 1.0 / DF_HALF)).astype(np.float32)
        in_specs += [pl.BlockSpec((1, tn), lambda n, i: (0, 0)),
                     pl.BlockSpec((256, 256), lambda n, i: (0, 0))]
        args += [jnp.tile(gain.astype(F32), tn // DF_HALF).reshape(1, tn), jnp.asarray(pmat, dtype=BF16)]
    out_specs, out_shape = [], []
    if want_f32:
        assert ncol == 1
        out_specs.append(pl.BlockSpec((tm * N_HEADS, HEAD_DIM), lambda n, i: (i, 0)))
        out_shape.append(jax.ShapeDtypeStruct((m * N_HEADS, HEAD_DIM), F32))
    if want_bf16:
        out_specs.append(pl.BlockSpec((tm, tn), lambda n, i: (i, n)))
        out_shape.append(jax.ShapeDtypeStruct((m, ncol * tn), BF16))
    res = pl.pallas_call(
        functools.partial(_proj_kernel, qknorm=qknorm, scale=scale, want_f32=want_f32, want_bf16=want_bf16),
        grid=(ncol, m // tm),
        in_specs=in_specs,
        out_specs=out_specs,
        out_shape=out_shape,
        compiler_params=_cparams(("parallel", "parallel")),
        name="in_proj",
    )(*args)
    return res


def _bucket(rel):
    n = jnp.maximum(rel, 0)
    nf = jnp.maximum(n, 1).astype(F32)
    large = MAX_EXACT + (jnp.log(nf / MAX_EXACT) / math.log(MAX_DISTANCE / MAX_EXACT)
                         * (N_BUCKETS - MAX_EXACT)).astype(jnp.int32)
    large = jnp.minimum(large, N_BUCKETS - 1)
    return jnp.where(n < MAX_EXACT, n, large)


def _bias_prompt_kernel(rb_ref, o_ref, *, t):
    h = pl.program_id(0)
    keys = lax.broadcasted_iota(jnp.int32, (t, t), 0)
    qrys = lax.broadcasted_iota(jnp.int32, (t, t), 1)
    far = rb_ref[(N_BUCKETS - 1) * N_HEADS + h]
    for d in range(2):
        b = _bucket(qrys - keys + d * t)
        acc = jnp.zeros((t, t), F32)
        for k in range(N_BUCKETS - 1):
            acc = jnp.where(b == k, rb_ref[k * N_HEADS + h] - far, acc)
        if d == 0:
            acc = jnp.where(keys <= qrys, acc, NEG_BIG)
        o_ref[0, d] = acc


def _bias_prompt(rel_bias, t):
    return pl.pallas_call(
        functools.partial(_bias_prompt_kernel, t=t),
        grid=(N_HEADS,),
        in_specs=[pl.BlockSpec(memory_space=pltpu.SMEM)],
        out_specs=pl.BlockSpec((1, 2, t, t), lambda h: (h, 0, 0, 0)),
        out_shape=jax.ShapeDtypeStruct((N_HEADS, 2, t, t), F32),
        compiler_params=_cparams(("parallel",)),
        name="bias_prompt",
    )(rel_bias.reshape(-1))


def _bias_sample_kernel(rb_ref, last_ref, new_ref, *, past, n_new):
    def tile(rows, key0):
        lane = lax.broadcasted_iota(jnp.int32, (rows, QCOLS), 1)
        key = lax.broadcasted_iota(jnp.int32, (rows, QCOLS), 0) + key0
        head = lax.shift_right_logical(lane, (2 * n_new).bit_length() - 1)
        rel = past + (lane & (n_new - 1)) - key
        b = _bucket(rel)
        acc = jnp.zeros((rows, QCOLS), F32)
        for hh in range(N_HEADS):
            far = rb_ref[(N_BUCKETS - 1) * N_HEADS + hh]
            for k in range(N_BUCKETS - 1):
                acc = jnp.where((b == k) & (head == hh), rb_ref[k * N_HEADS + hh] - far, acc)
        return acc
    last_ref[...] = tile(PAGE_SIZE, past - PAGE_SIZE)
    new_ref[...] = tile(n_new, past)


def _bias_sample(rel_bias, past, n_new):
    return pl.pallas_call(
        functools.partial(_bias_sample_kernel, past=past, n_new=n_new),
        in_specs=[pl.BlockSpec(memory_space=pltpu.SMEM)],
        out_specs=[pl.BlockSpec((PAGE_SIZE, QCOLS), lambda: (0, 0)),
                   pl.BlockSpec((n_new, QCOLS), lambda: (0, 0))],
        out_shape=[jax.ShapeDtypeStruct((PAGE_SIZE, QCOLS), F32),
                   jax.ShapeDtypeStruct((n_new, QCOLS), F32)],
        name="bias_sample",
    )(rel_bias.reshape(-1))


def _lambda(lq1, lk1, lq2, lk2, lam_init):
    s1 = jnp.sum(lq1 * lk1, axis=-1, keepdims=True)
    s2 = jnp.sum(lq2 * lk2, axis=-1, keepdims=True)
    return jnp.exp(s1) - jnp.exp(s2) + lam_init


def _store_transposed(dst_ref, src_ref, t, row0=0):
    for ci in range(src_ref.shape[0] // t):
        chunk = src_ref[ci * t:(ci + 1) * t, :].astype(F32)
        dst_ref[row0:row0 + HEAD_DIM, ci * t:(ci + 1) * t] = chunk.T.astype(dst_ref.dtype)


def _head_cols(hh):
    return slice(hh * HEAD_DIM, (hh + 1) * HEAD_DIM)


def _sb_prompt_kernel(q_ref, k_ref, v_ref, upper_ref, o_ref, vt_sc, *, t):
    qi = pl.program_id(2)
    heads = range(ATT_HEADS)

    @pl.when(qi == 0)
    def _():
        for hh in heads:
            _store_transposed(vt_sc.at[hh], v_ref.at[:, _head_cols(hh)], t)

    qs = [q_ref[:, _head_cols(hh)] for hh in heads]
    keys = lax.broadcasted_iota(jnp.int32, (t, t), 0)
    qrys = lax.broadcasted_iota(jnp.int32, (t, t), 1)
    causal = keys < qrys

    def block(j, carry, masked):
        start = pl.multiple_of(j * t, t)
        zs = [lax.dot_general(k_ref[pl.ds(start, t), _head_cols(hh)], qs[hh], (((1,), (1,)), ((), ())),
                              preferred_element_type=F32) for hh in heads]
        sps = []
        for hh in heads:
            sp = _softplus(zs[hh])
            sps.append(jnp.where(causal, sp, 0.0) if masked else sp)
        sufs = [jnp.dot(upper_ref[...], sps[hh].astype(BF16), preferred_element_type=F32) for hh in heads]
        out = []
        for hh in heads:
            acc, c = carry[hh]
            log_a = zs[hh] - sps[hh] - sufs[hh] - c
            if masked:
                log_a = jnp.where(causal, log_a, NEG_BIG)
            a = jnp.exp(log_a)
            acc = acc + jnp.dot(vt_sc[hh, :, pl.ds(start, t)], a.astype(BF16), preferred_element_type=F32)
            c = c + jnp.sum(sps[hh], axis=0, keepdims=True)
            out.append((acc, c))
        return tuple(out)

    carry = tuple((jnp.zeros((HEAD_DIM, t), F32), jnp.zeros((1, t), F32)) for _ in heads)
    carry = block(qi, carry, True)
    carry = lax.fori_loop(0, qi, lambda i, cr: block(qi - 1 - i, cr, False), carry)
    for hh in heads:
        o_ref[:, _head_cols(hh)] = carry[hh][0].T.astype(o_ref.dtype)


def _sb_prompt(q, k, v, batch, seq, t):
    nq = seq // t
    w = ATT_HEADS * HEAD_DIM
    upper = jnp.asarray(np.triu(np.ones((t, t), np.float32), 1), dtype=BF16)
    return pl.pallas_call(
        functools.partial(_sb_prompt_kernel, t=t),
        grid=(batch, N_HEADS // ATT_HEADS, nq),
        in_specs=[pl.BlockSpec((t, w), lambda b, h, i: (b * nq + i, h)),
                  pl.BlockSpec((seq, w), lambda b, h, i: (b, h)),
                  pl.BlockSpec((seq, w), lambda b, h, i: (b, h)),
                  pl.BlockSpec((t, t), lambda b, h, i: (0, 0))],
        out_specs=pl.BlockSpec((t, w), lambda b, h, i: (b * nq + i, h)),
        out_shape=jax.ShapeDtypeStruct((batch * seq, WIDTH), BF16),
        scratch_shapes=[pltpu.VMEM((ATT_HEADS, HEAD_DIM, seq), BF16)],
        compiler_params=_cparams(("parallel", "parallel", "arbitrary")),
        name="sb_prompt",
    )(q, k, v, upper)


def _df_prompt_kernel(q_ref, k_ref, v_ref, bias_ref, lq1, lk1, lq2, lk2, g_ref, o_ref,
                      vt_sc, m_sc, l_sc, acc_sc, *, t, lam_init):
    qi = pl.program_id(2)
    heads = range(ATT_HEADS)

    @pl.when(qi == 0)
    def _():
        for hh in heads:
            _store_transposed(vt_sc.at[hh], v_ref.at[:, _head_cols(hh)], t)
            vt_sc[hh, HEAD_DIM:, :] = jnp.ones((HEAD_DIM, vt_sc.shape[2]), BF16)

    lane = lax.broadcasted_iota(jnp.int32, (t, HEAD_DIM), 1)
    qs = []
    for hh in heads:
        q = q_ref[:, _head_cols(hh)]
        zero = jnp.zeros_like(q)
        qs.append(jnp.concatenate([jnp.where(lane < DF_HALF, q, zero), jnp.where(lane >= DF_HALF, q, zero)], axis=0))

    def block(j, bias_d, first):
        start = pl.multiple_of(j * t, t)
        zs = [lax.dot_general(k_ref[pl.ds(start, t), _head_cols(hh)], qs[hh], (((1,), (1,)), ((), ())),
                              preferred_element_type=F32) for hh in heads]
        for hh in heads:
            z = zs[hh]
            if bias_d is not None:
                bias = bias_ref[hh, bias_d]
                z = z + jnp.concatenate([bias, bias], axis=1)
            zmax = jnp.max(z, axis=0, keepdims=True)
            m_new = zmax if first else jnp.maximum(m_sc[hh], zmax)
            p = jnp.exp(z - m_new).astype(BF16)
            pv = jnp.dot(vt_sc[hh, :, pl.ds(start, t)], p, preferred_element_type=F32)
            if first:
                l_sc[hh] = pv[HEAD_DIM:, :]
                acc_sc[hh] = pv[:HEAD_DIM, :]
            else:
                alpha = jnp.exp(m_sc[hh] - m_new)
                l_sc[hh] = alpha * l_sc[hh] + pv[HEAD_DIM:, :]
                acc_sc[hh] = alpha * acc_sc[hh] + pv[:HEAD_DIM, :]
            m_sc[hh] = m_new

    block(qi, 0, True)

    @pl.when(qi >= 1)
    def _():
        block(qi - 1, 1, False)

    def far(i, carry):
        block(qi - 2 - i, None, False)
        return carry
    lax.fori_loop(0, jnp.maximum(qi - 1, 0), far, 0)

    lam = _lambda(lq1[...], lk1[...], lq2[...], lk2[...], lam_init)
    for hh in heads:
        o = acc_sc[hh] / l_sc[hh]
        o = (o[:, :t] - lam * o[:, t:]).T
        ms = jnp.mean(o * o, axis=-1, keepdims=True)
        o_ref[:, _head_cols(hh)] = (o * lax.rsqrt(ms + EPS) * g_ref[...] * (1.0 - lam_init)).astype(o_ref.dtype)


def _df_prompt(q, k, v, bias, lams, subln, lam_init, batch, seq, t):
    nq = seq // t
    w = ATT_HEADS * HEAD_DIM
    vec64 = pl.BlockSpec((1, DF_HALF), lambda b, h, i: (0, 0))
    return pl.pallas_call(
        functools.partial(_df_prompt_kernel, t=t, lam_init=lam_init),
        grid=(batch, N_HEADS // ATT_HEADS, nq),
        in_specs=[pl.BlockSpec((t, w), lambda b, h, i: (b * nq + i, h)),
                  pl.BlockSpec((seq, w), lambda b, h, i: (b, h)),
                  pl.BlockSpec((seq, w), lambda b, h, i: (b, h)),
                  pl.BlockSpec((ATT_HEADS, 2, t, t), lambda b, h, i: (h, 0, 0, 0)),
                  vec64, vec64, vec64, vec64,
                  pl.BlockSpec((1, HEAD_DIM), lambda b, h, i: (0, 0))],
        out_specs=pl.BlockSpec((t, w), lambda b, h, i: (b * nq + i, h)),
        out_shape=jax.ShapeDtypeStruct((batch * seq, WIDTH), BF16),
        scratch_shapes=[pltpu.VMEM((ATT_HEADS, 2 * HEAD_DIM, seq), BF16),
                        pltpu.VMEM((ATT_HEADS, 1, 2 * t), F32), pltpu.VMEM((ATT_HEADS, HEAD_DIM, 2 * t), F32),
                        pltpu.VMEM((ATT_HEADS, HEAD_DIM, 2 * t), F32)],
        compiler_params=_cparams(("parallel", "parallel", "arbitrary")),
        name="df_prompt",
    )(q, k, v, bias, *lams, subln)


def _score_matrix(q_ref, seq_in_block, n_new, halves):
    row = lax.broadcasted_iota(jnp.int32, (QROWS, QCOLS), 0)
    lane = lax.broadcasted_iota(jnp.int32, (QROWS, QCOLS), 1)
    pick = (lane & (n_new - 1)) + seq_in_block * n_new == row
    if halves == 1:
        pick = pick & ((lane & n_new) == 0)
    spread = lax.dot_general(q_ref[...], jnp.where(pick, 1.0, 0.0).astype(BF16), (((0,), (0,)), ((), ())),
                             preferred_element_type=F32)
    feat = lax.broadcasted_iota(jnp.int32, (WIDTH, QCOLS), 0)
    col = lax.broadcasted_iota(jnp.int32, (WIDTH, QCOLS), 1)
    log2 = lambda v: v.bit_length() - 1
    keep = lax.shift_right_logical(feat, log2(HEAD_DIM)) == lax.shift_right_logical(col, log2(2 * n_new))
    if halves == 2:
        keep = keep & ((lax.shift_right_logical(feat, log2(DF_HALF)) & 1) == (lax.shift_right_logical(col, log2(n_new)) & 1))
    return jnp.where(keep, spread, 0.0).astype(BF16)


def _sample_attn_kernel(pt_ref, qa_ref, qb_ref, kan_ref, van_ref, kbn_ref, vbn_ref,
                        blast_ref, bnew_ref, lq1, lk1, lq2, lk2, g_ref, *rest,
                        n_seq, n_pages, n_new, layer, lam_init):
    g = PAGES_PER_STEP
    c_sbk, c_dfk, c_sbv, c_dfv, oa_ref, ob_ref = rest[:6]
    zsb_sc, zdf_sc, psb_sc, pdf_sc, accsb_sc, accdf_sc, qsb_sc, qdf_sc, ring, sems = rest[6:]
    b = pl.program_id(0)
    s = pl.program_id(1)
    n_kv = n_pages // g
    n_steps = 2 * n_kv
    past = n_pages * PAGE_SIZE
    chunk = g * PAGE_SIZE
    group = 2 * g

    def page_copy(cache, page, block, slot):
        return pltpu.make_async_copy(cache.at[layer, page], ring.at[block], sems.at[slot])

    def start_group(seq_i, step, slot):
        @pl.when(step < n_kv)
        def _():
            for i in range(g):
                page = pt_ref[seq_i * n_pages + step * g + i]
                page_copy(c_sbk, page, slot * group + i, slot).start()
                page_copy(c_dfk, page, slot * group + g + i, slot).start()

        @pl.when(step >= n_kv)
        def _():
            for i in range(g):
                page = pt_ref[seq_i * n_pages + (step - n_kv) * g + i]
                page_copy(c_sbv, page, slot * group + i, slot).start()
                page_copy(c_dfv, page, slot * group + g + i, slot).start()

    gidx = b * n_steps + s
    slot = lax.rem(gidx, RING_DEPTH)

    @pl.when(gidx == 0)
    def _():
        for d in range(RING_DEPTH - 1):
            start_group(jnp.int32(0), jnp.int32(d), jnp.int32(d))

    ahead = s + (RING_DEPTH - 1)
    wraps = ahead >= n_steps
    seq_ahead = jnp.where(wraps, b + 1, b)

    @pl.when(seq_ahead < n_seq)
    def _():
        start_group(seq_ahead, jnp.where(wraps, ahead - n_steps, ahead), lax.rem(gidx + (RING_DEPTH - 1), RING_DEPTH))

    for i in range(group):
        page_copy(c_sbk, 0, slot * group + i, slot).wait()

    def pages_bf16(first):
        return jnp.concatenate([_tokens_by_width(ring.at[slot * group + first + i], PAGE_SIZE).astype(BF16)
                                for i in range(g)], axis=0)

    def new_tokens(ref):
        return _tokens_by_width(ref, n_new)

    @pl.when(s == 0)
    def _queries():
        seq_in_block = lax.rem(b, QROWS // n_new)
        qsb_sc[...] = _score_matrix(qa_ref, seq_in_block, n_new, 1)
        qdf_sc[...] = _score_matrix(qb_ref, seq_in_block, n_new, 2)

    @pl.when(s < n_kv)
    def _scores():
        start = pl.multiple_of(s * chunk, chunk)
        zsb_sc[pl.ds(start, chunk), :] = jnp.dot(pages_bf16(0), qsb_sc[...], preferred_element_type=F32)
        zdf_sc[pl.ds(start, chunk), :] = jnp.dot(pages_bf16(g), qdf_sc[...], preferred_element_type=F32)

    def pad_rows_bf16(x):
        return jnp.concatenate([x, jnp.zeros_like(x)], axis=0).astype(BF16)

    @pl.when(s == n_kv - 1)
    def _weights():
        lane_n = lax.broadcasted_iota(jnp.int32, (n_new, QCOLS), 1)
        key_n = lax.broadcasted_iota(jnp.int32, (n_new, QCOLS), 0)
        qpos_n = lane_n & (n_new - 1)
        z_new = jnp.dot(pad_rows_bf16(new_tokens(kbn_ref)), qdf_sc[...], preferred_element_type=F32)[:n_new]
        z_new = z_new + bnew_ref[...]
        z_new = jnp.where(key_n <= qpos_n, z_new, NEG_BIG)
        last0 = past - PAGE_SIZE
        zdf_sc[last0:past, :] = zdf_sc[last0:past, :] + blast_ref[...]
        zp = zdf_sc[...]
        mx = jnp.maximum(jnp.max(zp, axis=0, keepdims=True), jnp.max(z_new, axis=0, keepdims=True))
        e_new = jnp.exp(z_new - mx)
        ep = jnp.exp(zp - mx)
        inv = 1.0 / (jnp.sum(ep, axis=0, keepdims=True) + jnp.sum(e_new, axis=0, keepdims=True))
        pdf_sc[...] = (ep * inv).astype(BF16)
        accdf_sc[...] = lax.dot_general(pad_rows_bf16(e_new * inv), pad_rows_bf16(new_tokens(vbn_ref)),
                                        (((0,), (0,)), ((), ())), preferred_element_type=F32)
        zs_new = jnp.dot(pad_rows_bf16(new_tokens(kan_ref)), qsb_sc[...], preferred_element_type=F32)[:n_new]
        strict = key_n < qpos_n
        sp_new = jnp.where(strict, _softplus(zs_new), 0.0)
        carry = jnp.zeros((1, QCOLS), F32)
        suffix_rows = [None] * n_new
        for i in range(n_new - 1, -1, -1):
            suffix_rows[i] = carry
            carry = carry + sp_new[i:i + 1]
        suffix_new = jnp.concatenate(suffix_rows, axis=0)
        a_new = jnp.exp(jnp.where(strict, zs_new - sp_new - suffix_new, NEG_BIG))
        accsb_sc[...] = lax.dot_general(pad_rows_bf16(a_new), pad_rows_bf16(new_tokens(van_ref)),
                                        (((0,), (0,)), ((), ())), preferred_element_type=F32)
        tc = 256
        rr = lax.broadcasted_iota(jnp.int32, (tc, tc), 0)
        cc = lax.broadcasted_iota(jnp.int32, (tc, tc), 1)
        upper = jnp.where(cc > rr, 1.0, 0.0).astype(BF16)
        for ci in range(past // tc - 1, -1, -1):
            zc = zsb_sc[ci * tc:(ci + 1) * tc, :]
            sp = _softplus(zc)
            suffix = jnp.dot(upper, sp.astype(BF16), preferred_element_type=F32)
            psb_sc[ci * tc:(ci + 1) * tc, :] = jnp.exp(zc - sp - suffix - carry).astype(BF16)
            carry = carry + jnp.sum(sp, axis=0, keepdims=True)

    @pl.when(s >= n_kv)
    def _values():
        start = pl.multiple_of((s - n_kv) * chunk, chunk)
        accsb_sc[...] += lax.dot_general(psb_sc[pl.ds(start, chunk), :], pages_bf16(0),
                                         (((0,), (0,)), ((), ())), preferred_element_type=F32)
        accdf_sc[...] += lax.dot_general(pdf_sc[pl.ds(start, chunk), :], pages_bf16(g),
                                         (((0,), (0,)), ((), ())), preferred_element_type=F32)

    @pl.when(s == 2 * n_kv - 1)
    def _finish():
        lam = _lambda(lq1[...], lk1[...], lq2[...], lk2[...], lam_init)
        for h in range(N_HEADS):
            cols = slice(h * HEAD_DIM, (h + 1) * HEAD_DIM)
            oa_ref[:, cols] = accsb_sc[h * 2 * n_new:h * 2 * n_new + n_new, cols]
            o1 = accdf_sc[h * 2 * n_new:h * 2 * n_new + n_new, cols]
            o2 = accdf_sc[h * 2 * n_new + n_new:(h + 1) * 2 * n_new, cols]
            o = o1 - lam * o2
            ms = jnp.mean(o * o, axis=-1, keepdims=True)
            ob_ref[:, cols] = o * lax.rsqrt(ms + EPS) * g_ref[...] * (1.0 - lam_init)


def _sample_attn(page_table, qsb, qdf, ka_n, va_n, kb_n, vb_n, blast, bnew, lams, subln,
                 c_sbk, c_sbv, c_dfk, c_dfv, layer, lam_init):
    n_seq, n_pages = page_table.shape
    n_new = ka_n.shape[0] // (n_seq * N_HEADS)
    assert 2 * N_HEADS * n_new == QCOLS
    g = PAGES_PER_STEP
    n_kv = n_pages // g
    past = n_pages * PAGE_SIZE

    assert n_pages % g == 0 and RING_DEPTH - 1 <= 2 * n_kv
    vec64 = pl.BlockSpec((1, DF_HALF), lambda b, s, pt: (0, 0))
    new_spec = pl.BlockSpec((n_new * N_HEADS, HEAD_DIM), lambda b, s, pt: (b, 0))
    out_spec = pl.BlockSpec((n_new, WIDTH), lambda b, s, pt: (b, 0))
    q_spec = pl.BlockSpec((QROWS, WIDTH), lambda b, s, pt: (b // (QROWS // n_new), 0))
    in_specs = [q_spec, q_spec,
                new_spec, new_spec, new_spec, new_spec,
                pl.BlockSpec((PAGE_SIZE, QCOLS), lambda b, s, pt: (0, 0)),
                pl.BlockSpec((n_new, QCOLS), lambda b, s, pt: (0, 0)),
                vec64, vec64, vec64, vec64,
                pl.BlockSpec((1, HEAD_DIM), lambda b, s, pt: (0, 0))]
    in_specs += [pl.BlockSpec(memory_space=pl.ANY)] * 4
    grid_spec = pltpu.PrefetchScalarGridSpec(
        num_scalar_prefetch=1,
        grid=(n_seq, 2 * n_kv),
        in_specs=in_specs,
        out_specs=[out_spec, out_spec],
        scratch_shapes=[pltpu.VMEM((past, QCOLS), F32), pltpu.VMEM((past, QCOLS), F32),
                        pltpu.VMEM((past, QCOLS), BF16), pltpu.VMEM((past, QCOLS), BF16),
                        pltpu.VMEM((QCOLS, WIDTH), F32), pltpu.VMEM((QCOLS, WIDTH), F32),
                        pltpu.VMEM((WIDTH, QCOLS), BF16), pltpu.VMEM((WIDTH, QCOLS), BF16),
                        pltpu.VMEM((RING_DEPTH * 2 * g, PAGE_SIZE * N_HEADS, HEAD_DIM), F32),
                        pltpu.SemaphoreType.DMA((RING_DEPTH,))])
    return pl.pallas_call(
        functools.partial(_sample_attn_kernel, n_seq=n_seq, n_pages=n_pages, n_new=n_new, layer=layer,
                          lam_init=lam_init),
        grid_spec=grid_spec,
        out_shape=[jax.ShapeDtypeStruct((n_seq * n_new, WIDTH), F32)] * 2,
        compiler_params=_cparams(("arbitrary", "arbitrary")),
        name="sample_attn",
    )(page_table.reshape(-1), qsb, qdf, ka_n, va_n, kb_n, vb_n, blast, bnew, *lams, subln,
      c_sbk, c_dfk, c_sbv, c_dfv)


def _merge_kernel(x_ref, oa_ref, ob_ref, ga_ref, gb_ref, wsb_ref, wdf_ref, wout_ref, g2_ref, x1_ref, hn_ref):
    ya = jnp.dot(oa_ref[...].astype(BF16), wsb_ref[...], preferred_element_type=F32)
    yb = jnp.dot(ob_ref[...].astype(BF16), wdf_ref[...], preferred_element_type=F32)
    m = jax.nn.sigmoid(ga_ref[...].astype(F32)) * ya + jax.nn.sigmoid(gb_ref[...].astype(F32)) * yb
    x1 = x_ref[...] + jnp.dot(m.astype(BF16), wout_ref[...], preferred_element_type=F32)
    x1_ref[...] = x1
    ms = jnp.mean(x1 * x1, axis=-1, keepdims=True)
    hn_ref[...] = (x1 * lax.rsqrt(ms + EPS) * g2_ref[...]).astype(BF16)


def _merge(x, oa, ob, gates, wsb, wdf, wout, norm2, tm):
    m, d = x.shape
    const = lambda shape: pl.BlockSpec(shape, lambda i: (0, 0), pipeline_mode=pl.Buffered(1))
    return pl.pallas_call(
        _merge_kernel,
        grid=(m // tm,),
        in_specs=[pl.BlockSpec((tm, d), lambda i: (i, 0)),
                  pl.BlockSpec((tm, WIDTH), lambda i: (i, 0)),
                  pl.BlockSpec((tm, WIDTH), lambda i: (i, 0)),
                  pl.BlockSpec((tm, d), lambda i: (i, 0)),
                  pl.BlockSpec((tm, d), lambda i: (i, 1)),
                  const((WIDTH, d)), const((WIDTH, d)), const((d, d)),
                  pl.BlockSpec((1, d), lambda i: (0, 0))],
        out_specs=[pl.BlockSpec((tm, d), lambda i: (i, 0)), pl.BlockSpec((tm, d), lambda i: (i, 0))],
        out_shape=[jax.ShapeDtypeStruct((m, d), F32), jax.ShapeDtypeStruct((m, d), BF16)],
        compiler_params=_cparams(("parallel",)),
        name="merge",
    )(x, oa, ob, gates, gates, wsb, wdf, wout, norm2.reshape(1, d))


def _mlp_kernel(x1_ref, hn_ref, wup_ref, wdn_ref, o_ref):
    f = pl.program_id(1)

    @pl.when(f == 0)
    def _():
        o_ref[...] = x1_ref[...]

    u = jnp.maximum(jnp.dot(hn_ref[...], wup_ref[...], preferred_element_type=F32), 0.0)
    o_ref[...] += jnp.dot((u * u).astype(BF16), wdn_ref[...], preferred_element_type=F32)


def _mlp(x1, hn, wup, wdn, tm, tf):
    m, d = x1.shape
    dff = wup.shape[1]
    return pl.pallas_call(
        _mlp_kernel,
        grid=(m // tm, dff // tf),
        in_specs=[pl.BlockSpec((tm, d), lambda i, f: (i, 0)),
                  pl.BlockSpec((tm, d), lambda i, f: (i, 0)),
                  pl.BlockSpec((d, tf), lambda i, f: (0, f)),
                  pl.BlockSpec((tf, d), lambda i, f: (f, 0))],
        out_specs=pl.BlockSpec((tm, d), lambda i, f: (i, 0)),
        out_shape=jax.ShapeDtypeStruct((m, d), F32),
        compiler_params=_cparams(("parallel", "arbitrary")),
        name="mlp",
    )(x1, hn, wup, wdn)


def _project_group(x, norm1, w_in, q_norm, k_norm, tm):
    h = _rmsnorm_bf16(x, norm1, min(tm, 512))
    (qa,) = _proj(h, w_in, COL_QA, 1, tm=tm, scale=SB_SCALE)
    ka, ka16 = _proj(h, w_in, COL_KA, 1, tm=tm, want_f32=True)
    va, va16 = _proj(h, w_in, COL_VA, 1, tm=tm, want_f32=True)
    (qb,) = _proj(h, w_in, COL_QB, 1, tm=tm, qknorm=True, gain=q_norm, scale=DF_SCALE)
    kb, kb16 = _proj(h, w_in, COL_KB, 1, tm=tm, qknorm=True, gain=k_norm, want_f32=True)
    vb, vb16 = _proj(h, w_in, COL_VB, 1, tm=tm, want_f32=True)
    (gates,) = _proj(h, w_in, COL_GA, 4, tm=tm)
    return dict(qa=qa, ka=ka, ka16=ka16, va=va, va16=va16, qb=qb, kb=kb, kb16=kb16, vb=vb, vb16=vb16, gates=gates)


def kernel(x_prompt, x_sample, cache_sb_k, cache_sb_v, cache_df_k, cache_df_v, page_table, rel_bias, norm1, w_in, q_norm, k_norm, lambda_q1, lambda_k1, lambda_q2, lambda_k2, subln, w_branch_sb, w_branch_df, w_out, norm2, w_up, w_down):
    depth = norm1.shape[0]
    batch, seq, d = x_prompt.shape
    n_seq, n_new, _ = x_sample.shape
    n_pages = page_table.shape[1]
    past = n_pages * PAGE_SIZE
    xp = x_prompt.reshape(batch * seq, d)
    xs = x_sample.reshape(n_seq * n_new, d)
    bias_p = _bias_prompt(rel_bias, ATT_T)
    bias_last, bias_new = _bias_sample(rel_bias, past, n_new)
    n_pool = cache_sb_k.shape[1]
    leaves = [[] for _ in range(8)]
    for l in range(depth):
        lam_init = 0.8 - 0.6 * math.exp(-0.3 * l)
        w_in16 = w_in[l].astype(BF16)
        wsb16 = w_branch_sb[l].astype(BF16)
        wdf16 = w_branch_df[l].astype(BF16)
        wout16 = w_out[l].astype(BF16)
        wup16 = w_up[l].astype(BF16)
        wdn16 = w_down[l].astype(BF16)
        lams = [v[l].reshape(1, DF_HALF).astype(F32) for v in (lambda_q1, lambda_k1, lambda_q2, lambda_k2)]
        sub = subln[l].reshape(1, HEAD_DIM).astype(F32)

        p = _project_group(xp, norm1[l], w_in16, q_norm[l], k_norm[l], 1024)
        oa = _sb_prompt(p["qa"], p["ka16"], p["va16"], batch, seq, ATT_T)
        ob = _df_prompt(p["qb"], p["kb16"], p["vb16"], bias_p, lams, sub, lam_init, batch, seq, ATT_T)
        x1, hn = _merge(xp, oa, ob, p["gates"], wsb16, wdf16, wout16, norm2[l], 256)
        xp = _mlp(x1, hn, wup16, wdn16, 512, 1024)
        for i, name in enumerate(("ka", "va", "kb", "vb")):
            leaves[i].append(p[name].reshape(batch, seq, N_HEADS, HEAD_DIM))

        s = _project_group(xs, norm1[l], w_in16, q_norm[l], k_norm[l], 1024)
        shape4 = (depth, n_pool, PAGE_SIZE * N_HEADS, HEAD_DIM)
        oa, ob = _sample_attn(page_table, s["qa"], s["qb"], s["ka"], s["va"], s["kb"], s["vb"], bias_last, bias_new,
                              lams, sub, cache_sb_k.reshape(shape4), cache_sb_v.reshape(shape4),
                              cache_df_k.reshape(shape4), cache_df_v.reshape(shape4), l, lam_init)
        x1, hn = _merge(xs, oa, ob, s["gates"], wsb16, wdf16, wout16, norm2[l], 256)
        xs = _mlp(x1, hn, wup16, wdn16, 512, 1024)
        for i, name in enumerate(("ka", "va", "kb", "vb")):
            leaves[4 + i].append(s[name].reshape(n_seq, n_new, N_HEADS, HEAD_DIM))

    return (xp.reshape(batch, seq, d), xs.reshape(n_seq, n_new, d)) + tuple(jnp.stack(v) for v in leaves)
```

```python
import functools
import math

import numpy as np
import jax
import jax.numpy as jnp
from jax import lax
from jax.experimental import pallas as pl
from jax.experimental.pallas import tpu as pltpu

F32 = jnp.float32
BF16 = jnp.bfloat16

D_MODEL = 2048
N_HEADS = 8
HEAD_DIM = 128
DF_HALF = 64
WIDTH = N_HEADS * HEAD_DIM
SB_SCALE = 1.0 / math.sqrt(HEAD_DIM)
DF_SCALE = 1.0 / math.sqrt(DF_HALF)
D_FF = 4 * D_MODEL
N_BUCKETS = 32
MAX_EXACT = N_BUCKETS // 2
MAX_DISTANCE = 128
EPS = 1e-6
PAGE_SIZE = 128
NEG_BIG = -1e30

COL_QA, COL_KA, COL_VA, COL_QB, COL_KB, COL_VB, COL_GA, COL_GB = 0, 1, 2, 3, 4, 5, 6, 8

ATT_T = 512
SB_HEADS = 4
DF_HEADS = 2
ATT_UNIT = 256
ONES_ROWS = 16
LOG2E = 1.4426950408889634
PAGES_PER_STEP = 8
RING_DEPTH = 3
QCOLS = 128
QROWS = 16

VMEM_LIMIT = 56 * 1024 * 1024


def _cparams(sem):
    return pltpu.CompilerParams(dimension_semantics=sem, vmem_limit_bytes=VMEM_LIMIT)


def _tokens_by_width(ref, n_tok):
    return jnp.concatenate([ref[pl.ds(h, n_tok, stride=N_HEADS), :] for h in range(N_HEADS)], axis=1)


def _softplus2(z):
    return jnp.maximum(z, 0.0) + jnp.log2(1.0 + jnp.exp2(-jnp.abs(z)))


def _emit_skewed(n_units, stages):
    state = [None] * n_units
    for tick in range(n_units + len(stages) - 1):
        for si in range(len(stages) - 1, -1, -1):
            u = tick - si
            if 0 <= u < n_units:
                state[u] = stages[si](u, state[u])
    return state


def _rmsnorm_kernel(x_ref, g_ref, o_ref):
    x = x_ref[...]
    ms = jnp.mean(x * x, axis=-1, keepdims=True)
    o_ref[...] = (x * lax.rsqrt(ms + EPS) * g_ref[...]).astype(o_ref.dtype)


def _rmsnorm_bf16(x, g, tm):
    m, d = x.shape
    return pl.pallas_call(
        _rmsnorm_kernel,
        grid=(m // tm,),
        in_specs=[pl.BlockSpec((tm, d), lambda i: (i, 0)),
                  pl.BlockSpec((1, d), lambda i: (0, 0))],
        out_specs=pl.BlockSpec((tm, d), lambda i: (i, 0)),
        out_shape=jax.ShapeDtypeStruct((m, d), BF16),
        compiler_params=_cparams(("parallel",)),
        name="rmsnorm_bf16",
    )(x, g.reshape(1, d))


def _proj_kernel(*refs, qknorm, scale, want_f32, want_bf16):
    h_ref, w_ref = refs[0], refs[1]
    pos = 2
    if qknorm:
        gain_ref, pmat_ref = refs[2], refs[3]
        pos = 4
    outs = refs[pos:]
    y = jnp.dot(h_ref[...], w_ref[...], preferred_element_type=F32)
    if qknorm:
        tn = y.shape[1]
        pieces = []
        for c in range(tn // 256):
            yb = y[:, c * 256:(c + 1) * 256]
            ms = jnp.dot((yb * yb).astype(BF16), pmat_ref[...], preferred_element_type=F32)
            pieces.append(yb * lax.rsqrt(ms + EPS))
        y = jnp.concatenate(pieces, axis=1) * gain_ref[...]
    k = 0
    if want_f32:
        for hd in range(N_HEADS):
            outs[k][pl.ds(hd, y.shape[0], stride=N_HEADS), :] = y[:, hd * HEAD_DIM:(hd + 1) * HEAD_DIM]
        k += 1
    if want_bf16:
        outs[k][...] = (y * scale).astype(BF16) if scale != 1.0 else y.astype(BF16)


def _proj(h, w, col0, ncol, *, tm, qknorm=False, gain=None, scale=1.0, want_f32=False, want_bf16=True):
    m, kdim = h.shape
    tn = WIDTH
    in_specs = [pl.BlockSpec((tm, kdim), lambda n, i: (i, 0)),
                pl.BlockSpec((kdim, tn), lambda n, i: (0, col0 + n))]
    args = [h, w]
    if qknorm:
        pmat = np.kron(np.eye(256 // DF_HALF), np.full((DF_HALF, DF_HALF), 1.0 / DF_HALF)).astype(np.float32)
        in_specs += [pl.BlockSpec((1, tn), lambda n, i: (0, 0)),
                     pl.BlockSpec((256, 256), lambda n, i: (0, 0))]
        args += [jnp.tile(gain.astype(F32), tn // DF_HALF).reshape(1, tn), jnp.asarray(pmat, dtype=BF16)]
    out_specs, out_shape = [], []
    if want_f32:
        assert ncol == 1
        out_specs.append(pl.BlockSpec((tm * N_HEADS, HEAD_DIM), lambda n, i: (i, 0)))
        out_shape.append(jax.ShapeDtypeStruct((m * N_HEADS, HEAD_DIM), F32))
    if want_bf16:
        out_specs.append(pl.BlockSpec((tm, tn), lambda n, i: (i, n)))
        out_shape.append(jax.ShapeDtypeStruct((m, ncol * tn), BF16))
    res = pl.pallas_call(
        functools.partial(_proj_kernel, qknorm=qknorm, scale=scale, want_f32=want_f32, want_bf16=want_bf16),
        grid=(ncol, m // tm),
        in_specs=in_specs,
        out_specs=out_specs,
        out_shape=out_shape,
        compiler_params=_cparams(("parallel", "parallel")),
        name="in_proj",
    )(*args)
    return res


def _bucket(rel):
    n = jnp.maximum(rel, 0)
    nf = jnp.maximum(n, 1).astype(F32)
    large = MAX_EXACT + (jnp.log(nf / MAX_EXACT) / math.log(MAX_DISTANCE / MAX_EXACT)
                         * (N_BUCKETS - MAX_EXACT)).astype(jnp.int32)
    large = jnp.minimum(large, N_BUCKETS - 1)
    return jnp.where(n < MAX_EXACT, n, large)


def _bias_prompt_kernel(rb_ref, o_ref, *, t):
    h = pl.program_id(0)
    keys = lax.broadcasted_iota(jnp.int32, (t, t), 0)
    qrys = lax.broadcasted_iota(jnp.int32, (t, t), 1)
    far = rb_ref[(N_BUCKETS - 1) * N_HEADS + h]
    for d in range(2):
        b = _bucket(qrys - keys + d * t)
        acc = jnp.zeros((t, t), F32)
        for k in range(N_BUCKETS - 1):
            acc = jnp.where(b == k, rb_ref[k * N_HEADS + h] - far, acc)
        if d == 0:
            acc = jnp.where(keys <= qrys, acc, NEG_BIG)
        o_ref[0, d] = acc
    o_ref[0, 2] = jnp.zeros((t, t), F32)


def _bias_prompt(rel_bias, t):
    return pl.pallas_call(
        functools.partial(_bias_prompt_kernel, t=t),
        grid=(N_HEADS,),
        in_specs=[pl.BlockSpec(memory_space=pltpu.SMEM)],
        out_specs=pl.BlockSpec((1, 3, t, t), lambda h: (h, 0, 0, 0)),
        out_shape=jax.ShapeDtypeStruct((N_HEADS, 3, t, t), F32),
        compiler_params=_cparams(("parallel",)),
        name="bias_prompt",
    )(rel_bias.reshape(-1))


def _bias_sample_kernel(rb_ref, last_ref, new_ref, *, past, n_new):
    def tile(rows, key0):
        lane = lax.broadcasted_iota(jnp.int32, (rows, QCOLS), 1)
        key = lax.broadcasted_iota(jnp.int32, (rows, QCOLS), 0) + key0
        head = lax.shift_right_logical(lane, (2 * n_new).bit_length() - 1)
        rel = past + (lane & (n_new - 1)) - key
        b = _bucket(rel)
        acc = jnp.zeros((rows, QCOLS), F32)
        for hh in range(N_HEADS):
            far = rb_ref[(N_BUCKETS - 1) * N_HEADS + hh]
            for k in range(N_BUCKETS - 1):
                acc = jnp.where((b == k) & (head == hh), rb_ref[k * N_HEADS + hh] - far, acc)
        return acc
    last_ref[...] = tile(PAGE_SIZE, past - PAGE_SIZE)
    new_ref[...] = tile(n_new, past)


def _bias_sample(rel_bias, past, n_new):
    return pl.pallas_call(
        functools.partial(_bias_sample_kernel, past=past, n_new=n_new),
        in_specs=[pl.BlockSpec(memory_space=pltpu.SMEM)],
        out_specs=[pl.BlockSpec((PAGE_SIZE, QCOLS), lambda: (0, 0)),
                   pl.BlockSpec((n_new, QCOLS), lambda: (0, 0))],
        out_shape=[jax.ShapeDtypeStruct((PAGE_SIZE, QCOLS), F32),
                   jax.ShapeDtypeStruct((n_new, QCOLS), F32)],
        name="bias_sample",
    )(rel_bias.reshape(-1))


def _lambda(lq1, lk1, lq2, lk2, lam_init):
    s1 = jnp.sum(lq1 * lk1, axis=-1, keepdims=True)
    s2 = jnp.sum(lq2 * lk2, axis=-1, keepdims=True)
    return jnp.exp(s1) - jnp.exp(s2) + lam_init


def _store_transposed(dst_ref, src_ref, t, row0=0):
    for ci in range(src_ref.shape[0] // t):
        chunk = src_ref[ci * t:(ci + 1) * t, :].astype(F32)
        dst_ref[row0:row0 + HEAD_DIM, ci * t:(ci + 1) * t] = chunk.T.astype(dst_ref.dtype)


def _head_cols(hh):
    return slice(hh * HEAD_DIM, (hh + 1) * HEAD_DIM)


def _sb_prompt_kernel(q_ref, k_ref, v_ref, upper_ref, o_ref, vt_sc, *, t):
    qi = pl.program_id(2)
    heads = range(SB_HEADS)

    @pl.when(qi == 0)
    def _():
        for hh in heads:
            _store_transposed(vt_sc.at[hh], v_ref.at[:, _head_cols(hh)], t)

    nsub = t // ATT_UNIT
    units = [(hh, cb) for hh in heads for cb in range(nsub)]
    qs = [q_ref[cb * ATT_UNIT:(cb + 1) * ATT_UNIT, _head_cols(hh)] for hh, cb in units]
    keys = lax.broadcasted_iota(jnp.int32, (ATT_UNIT, ATT_UNIT), 0)
    qrys = lax.broadcasted_iota(jnp.int32, (ATT_UNIT, ATT_UNIT), 1)
    tri = keys < qrys

    def block(j, carry, masked):
        start = pl.multiple_of(j * t, t)

        def scores(u, _):
            hh, _cb = units[u]
            return lax.dot_general(k_ref[pl.ds(start, t), _head_cols(hh)], qs[u], (((1,), (1,)), ((), ())),
                                   preferred_element_type=F32)

        def softplus(u, z):
            cb = units[u][1]
            c = carry[u][1]
            subs = []
            for kb in range(nsub - 1, -1, -1):
                if masked and kb > cb:
                    continue
                zk = z[kb * ATT_UNIT:(kb + 1) * ATT_UNIT]
                sp = _softplus2(zk)
                diag = masked and kb == cb
                if diag:
                    sp = jnp.where(tri, sp, 0.0)
                subs.append((kb, zk, sp, c, diag))
                c = c + jnp.sum(sp, axis=0, keepdims=True)
            return subs, c

        def suffixes(u, state):
            subs, c = state
            return [sub + (jnp.dot(upper_ref[...], sub[2].astype(BF16), preferred_element_type=F32),)
                    for sub in subs], c

        def weights(u, state):
            subs, c = state
            hh = units[u][0]
            acc = carry[u][0]
            for kb, zk, sp, c_in, diag, suffix in subs:
                log_a = zk - sp - suffix - c_in
                if diag:
                    log_a = jnp.where(tri, log_a, NEG_BIG)
                a = jnp.exp2(log_a).astype(BF16)
                acc = acc + jnp.dot(vt_sc[hh, :, pl.ds(start + kb * ATT_UNIT, ATT_UNIT)], a,
                                    preferred_element_type=F32)
            return acc, c

        return tuple(_emit_skewed(len(units), [scores, softplus, suffixes, weights]))

    carry = tuple((jnp.zeros((HEAD_DIM, ATT_UNIT), F32), jnp.zeros((1, ATT_UNIT), F32)) for _ in units)
    carry = block(qi, carry, True)
    carry = lax.fori_loop(0, qi, lambda i, cr: block(qi - 1 - i, cr, False), carry)
    for u, (hh, cb) in enumerate(units):
        o_ref[cb * ATT_UNIT:(cb + 1) * ATT_UNIT, _head_cols(hh)] = carry[u][0].T.astype(o_ref.dtype)


def _sb_prompt(q, k, v, batch, seq, t):
    nq = seq // t
    w = SB_HEADS * HEAD_DIM
    upper =jnp.asarray(np.triu(np.ones((ATT_UNIT, ATT_UNIT), np.float32), 1), dtype=BF16)
    return pl.pallas_call(
        functools.partial(_sb_prompt_kernel, t=t),
        grid=(batch, N_HEADS // SB_HEADS, nq),
        in_specs=[pl.BlockSpec((t, w), lambda b, h, i: (b * nq + i, h)),
                  pl.BlockSpec((seq, w), lambda b, h, i: (b, h)),
                  pl.BlockSpec((seq, w), lambda b, h, i: (b, h)),
                  pl.BlockSpec((ATT_UNIT, ATT_UNIT), lambda b, h, i: (0, 0))],
        out_specs=pl.BlockSpec((t, w), lambda b, h, i: (b * nq + i, h)),
        out_shape=jax.ShapeDtypeStruct((batch * seq, WIDTH), BF16),
        scratch_shapes=[pltpu.VMEM((SB_HEADS, HEAD_DIM, seq), BF16)],
        compiler_params=_cparams(("parallel", "parallel", "arbitrary")),
        name="sb_prompt",
    )(q, k, v, upper)


def _df_prompt_kernel(q_ref, k_ref, v_ref, bias_ref, lq1, lk1, lq2, lk2, g_ref, o_ref,
                      vt_sc, m_sc, l_sc, acc_sc, z_sc, *, t, lam_init):
    qi = pl.program_id(2)
    heads = range(DF_HEADS)

    @pl.when(qi == 0)
    def _():
        for hh in heads:
            _store_transposed(vt_sc.at[hh], v_ref.at[:, _head_cols(hh)], t)
            vt_sc[hh, HEAD_DIM:, :] = jnp.ones((ONES_ROWS, vt_sc.shape[2]), BF16)

    lane = lax.broadcasted_iota(jnp.int32, (t, HEAD_DIM), 1)
    qs = []
    for hh in heads:
        q = q_ref[:, _head_cols(hh)]
        zero = jnp.zeros_like(q)
        qs.append(jnp.concatenate([jnp.where(lane < DF_HALF, q, zero), jnp.where(lane >= DF_HALF, q, zero)], axis=0))
        m_sc[hh] = jnp.full(m_sc.shape[1:], NEG_BIG, F32)
        l_sc[hh] = jnp.zeros(l_sc.shape[1:], F32)
        acc_sc[hh] = jnp.zeros(acc_sc.shape[1:], F32)

    def scores(j, bias_d, slot):
        start = pl.multiple_of(j * t, t)
        for hh in heads:
            z = lax.dot_general(k_ref[pl.ds(start, t), _head_cols(hh)], qs[hh], (((1,), (1,)), ((), ())),
                                preferred_element_type=F32)
            bias = bias_ref[hh, bias_d]
            z_sc[slot, hh] = z + jnp.concatenate([bias, bias], axis=1)

    def consume(j, slot):
        for hh in heads:
            consume_head(j, z_sc[slot, hh], hh)

    def consume_head(j, z, hh):
        start = pl.multiple_of(j * t, t)
        m_old = m_sc[hh]
        m_new = jnp.maximum(m_old, jnp.max(z, axis=0, keepdims=True))
        p = jnp.exp(z - m_new).astype(BF16)
        alpha = jnp.exp(m_old - m_new)
        pv = jnp.dot(vt_sc[hh, :, pl.ds(start, t)], p, preferred_element_type=F32)
        l_sc[hh] = alpha * l_sc[hh] + pv[HEAD_DIM:HEAD_DIM + 1, :]
        acc_sc[hh] = alpha * acc_sc[hh] + pv[:HEAD_DIM, :]
        m_sc[hh] = m_new

    scores(qi, 0, 0)

    def pair(p, carry):
        j = qi - 2 * p
        scores(j - 1, jnp.minimum(2 * p + 1, 2), 1)
        consume(j, 0)
        scores(j - 2, 2, 0)
        consume(j - 1, 1)
        return carry
    lax.fori_loop(0, qi // 2, pair, 0)

    @pl.when(qi % 2 == 1)
    def _():
        scores(0, jnp.minimum(qi, 2), 1)
        consume(1, 0)
        consume(0, 1)

    @pl.when(qi % 2 == 0)
    def _():
        consume(0, 0)

    lam = _lambda(lq1[...], lk1[...], lq2[...], lk2[...], lam_init)
    for hh in heads:
        o = acc_sc[hh] / l_sc[hh]
        o = (o[:, :t] - lam * o[:, t:]).T
        ms = jnp.mean(o * o, axis=-1, keepdims=True)
        o_ref[:, _head_cols(hh)] = (o * lax.rsqrt(ms + EPS) * g_ref[...] * (1.0 - lam_init)).astype(o_ref.dtype)


def _df_prompt(q, k, v, bias, lams, subln, lam_init, batch, seq, t):
    nq = seq // t
    w = DF_HEADS * HEAD_DIM
    vec64 = pl.BlockSpec((1, DF_HALF), lambda b, h, i: (0, 0))
    return pl.pallas_call(
        functools.partial(_df_prompt_kernel, t=t, lam_init=lam_init),
        grid=(batch, N_HEADS // DF_HEADS, nq),
        in_specs=[pl.BlockSpec((t, w), lambda b, h, i: (b * nq + i, h)),
                  pl.BlockSpec((seq, w), lambda b, h, i: (b, h)),
                  pl.BlockSpec((seq, w), lambda b, h, i: (b, h)),
                  pl.BlockSpec((DF_HEADS, 3, t, t), lambda b, h, i: (h, 0, 0, 0)),
                  vec64, vec64, vec64, vec64,
                  pl.BlockSpec((1, HEAD_DIM), lambda b, h, i: (0, 0))],
        out_specs=pl.BlockSpec((t, w), lambda b, h, i: (b * nq + i, h)),
        out_shape=jax.ShapeDtypeStruct((batch * seq, WIDTH), BF16),
        scratch_shapes=[pltpu.VMEM((DF_HEADS, HEAD_DIM + ONES_ROWS, seq), BF16),
                        pltpu.VMEM((DF_HEADS, 1, 2 * t), F32), pltpu.VMEM((DF_HEADS, 1, 2 * t), F32),
                        pltpu.VMEM((DF_HEADS, HEAD_DIM, 2 * t), F32),
                        pltpu.VMEM((2, DF_HEADS, t, 2 * t), F32)],
        compiler_params=_cparams(("parallel", "parallel", "arbitrary")),
        name="df_prompt",
    )(q, k, v, bias, *lams, subln)


def _score_matrix(q_ref, seq_in_block, n_new, halves):
    row = lax.broadcasted_iota(jnp.int32, (QROWS, QCOLS), 0)
    lane = lax.broadcasted_iota(jnp.int32, (QROWS, QCOLS), 1)
    pick = (lane & (n_new - 1)) + seq_in_block * n_new == row
    if halves == 1:
        pick = pick & ((lane & n_new) == 0)
    spread = lax.dot_general(q_ref[...], jnp.where(pick, 1.0, 0.0).astype(BF16), (((0,), (0,)), ((), ())),
                             preferred_element_type=F32)
    feat = lax.broadcasted_iota(jnp.int32, (WIDTH, QCOLS), 0)
    col = lax.broadcasted_iota(jnp.int32, (WIDTH, QCOLS), 1)
    log2 = lambda v: v.bit_length() - 1
    keep = lax.shift_right_logical(feat, log2(HEAD_DIM)) == lax.shift_right_logical(col, log2(2 * n_new))
    if halves == 2:
        keep = keep & ((lax.shift_right_logical(feat, log2(DF_HALF)) & 1) == (lax.shift_right_logical(col, log2(n_new)) & 1))
    return jnp.where(keep, spread, 0.0).astype(BF16)


def _sample_attn_kernel(pt_ref, qa_ref, qb_ref, kan_ref, van_ref, kbn_ref, vbn_ref,
                        blast_ref, bnew_ref, lq1, lk1, lq2, lk2, g_ref, *rest,
                        n_seq, n_pages, n_new, layer, lam_init):
    g = PAGES_PER_STEP
    c_sbk, c_dfk, c_sbv, c_dfv, oa_ref, ob_ref = rest[:6]
    zsb_sc, zdf_sc, psb_sc, pdf_sc, accsb_sc, accdf_sc, qsb_sc, qdf_sc, ring, sems = rest[6:]
    b = pl.program_id(0)
    s = pl.program_id(1)
    n_kv = n_pages // g
    n_steps = 2 * n_kv
    past = n_pages * PAGE_SIZE
    chunk = g * PAGE_SIZE
    group = 2 * g

    def page_copy(cache, page, block, slot):
        return pltpu.make_async_copy(cache.at[layer, page], ring.at[block], sems.at[slot])

    def start_group(seq_i, step, slot):
        @pl.when(step < n_kv)
        def _():
            for i in range(g):
                page = pt_ref[seq_i * n_pages + step * g + i]
                page_copy(c_sbk, page, slot * group + i, slot).start()
                page_copy(c_dfk, page, slot * group + g + i, slot).start()

        @pl.when(step >= n_kv)
        def _():
            for i in range(g):
                page = pt_ref[seq_i * n_pages + (step - n_kv) * g + i]
                page_copy(c_sbv, page, slot * group + i, slot).start()
                page_copy(c_dfv, page, slot * group + g + i, slot).start()

    gidx = b * n_steps + s
    slot = lax.rem(gidx, RING_DEPTH)

    @pl.when(gidx == 0)
    def _():
        for d in range(RING_DEPTH - 1):
            start_group(jnp.int32(0), jnp.int32(d), jnp.int32(d))

    ahead = s + (RING_DEPTH - 1)
    wraps = ahead >= n_steps
    seq_ahead = jnp.where(wraps, b + 1, b)

    @pl.when(seq_ahead < n_seq)
    def _():
        start_group(seq_ahead, jnp.where(wraps, ahead - n_steps, ahead), lax.rem(gidx + (RING_DEPTH - 1), RING_DEPTH))

    for i in range(group):
        page_copy(c_sbk, 0, slot * group + i, slot).wait()

    def pages_bf16(first):
        return jnp.concatenate([_tokens_by_width(ring.at[slot * group + first + i], PAGE_SIZE).astype(BF16)
                                for i in range(g)], axis=0)

    def new_tokens(ref):
        return _tokens_by_width(ref, n_new)

    @pl.when(s == 0)
    def _queries():
        seq_in_block = lax.rem(b, QROWS // n_new)
        qsb_sc[...] = _score_matrix(qa_ref, seq_in_block, n_new, 1)
        qdf_sc[...] = _score_matrix(qb_ref, seq_in_block, n_new, 2)

    @pl.when(s < n_kv)
    def _scores():
        start = pl.multiple_of(s * chunk, chunk)
        zsb_sc[pl.ds(start, chunk), :] = jnp.dot(pages_bf16(0), qsb_sc[...], preferred_element_type=F32)
        zdf_sc[pl.ds(start, chunk), :] = jnp.dot(pages_bf16(g), qdf_sc[...], preferred_element_type=F32)

    def pad_rows_bf16(x):
        return jnp.concatenate([x, jnp.zeros_like(x)], axis=0).astype(BF16)

    @pl.when(s == n_kv - 1)
    def _weights():
        lane_n = lax.broadcasted_iota(jnp.int32, (n_new, QCOLS), 1)
        key_n = lax.broadcasted_iota(jnp.int32, (n_new, QCOLS), 0)
        qpos_n = lane_n & (n_new - 1)
        z_new = jnp.dot(pad_rows_bf16(new_tokens(kbn_ref)), qdf_sc[...], preferred_element_type=F32)[:n_new]
        z_new = z_new + bnew_ref[...]
        z_new = jnp.where(key_n <= qpos_n, z_new, NEG_BIG)
        last0 = past - PAGE_SIZE
        zdf_sc[last0:past, :] = zdf_sc[last0:past, :] + blast_ref[...]
        zp = zdf_sc[...]
        mx = jnp.maximum(jnp.max(zp, axis=0, keepdims=True), jnp.max(z_new, axis=0, keepdims=True))
        e_new = jnp.exp(z_new - mx)
        ep = jnp.exp(zp - mx)
        inv = 1.0 / (jnp.sum(ep, axis=0, keepdims=True) + jnp.sum(e_new, axis=0, keepdims=True))
        pdf_sc[...] = (ep * inv).astype(BF16)
        accdf_sc[...] = lax.dot_general(pad_rows_bf16(e_new * inv), pad_rows_bf16(new_tokens(vbn_ref)),
                                        (((0,), (0,)), ((), ())), preferred_element_type=F32)
        zs_new = jnp.dot(pad_rows_bf16(new_tokens(kan_ref)), qsb_sc[...], preferred_element_type=F32)[:n_new]
        strict = key_n < qpos_n
        sp_new = jnp.where(strict, _softplus2(zs_new), 0.0)
        carry = jnp.zeros((1, QCOLS), F32)
        suffix_rows = [None] * n_new
        for i in range(n_new - 1, -1, -1):
            suffix_rows[i] = carry
            carry = carry + sp_new[i:i + 1]
        suffix_new = jnp.concatenate(suffix_rows, axis=0)
        a_new = jnp.exp2(jnp.where(strict, zs_new - sp_new - suffix_new, NEG_BIG))
        accsb_sc[...] = lax.dot_general(pad_rows_bf16(a_new), pad_rows_bf16(new_tokens(van_ref)),
                                        (((0,), (0,)), ((), ())), preferred_element_type=F32)
        tc = 256
        rr = lax.broadcasted_iota(jnp.int32, (tc, tc), 0)
        cc = lax.broadcasted_iota(jnp.int32, (tc, tc), 1)
        upper = jnp.where(cc > rr, 1.0, 0.0).astype(BF16)
        for ci in range(past // tc - 1, -1, -1):
            zc = zsb_sc[ci * tc:(ci + 1) * tc, :]
            sp = _softplus2(zc)
            suffix = jnp.dot(upper, sp.astype(BF16), preferred_element_type=F32)
            psb_sc[ci * tc:(ci + 1) * tc, :] = jnp.exp2(zc - sp - suffix - carry).astype(BF16)
            carry = carry + jnp.sum(sp, axis=0, keepdims=True)

    @pl.when(s >= n_kv)
    def _values():
        start = pl.multiple_of((s - n_kv) * chunk, chunk)
        accsb_sc[...] += lax.dot_general(psb_sc[pl.ds(start, chunk), :], pages_bf16(0),
                                         (((0,), (0,)), ((), ())), preferred_element_type=F32)
        accdf_sc[...] += lax.dot_general(pdf_sc[pl.ds(start, chunk), :], pages_bf16(g),
                                         (((0,), (0,)), ((), ())), preferred_element_type=F32)

    @pl.when(s == 2 * n_kv - 1)
    def _finish():
        lam = _lambda(lq1[...], lk1[...], lq2[...], lk2[...], lam_init)
        for h in range(N_HEADS):
            cols = slice(h * HEAD_DIM, (h + 1) * HEAD_DIM)
            oa_ref[:, cols] = accsb_sc[h * 2 * n_new:h * 2 * n_new + n_new, cols]
            o1 = accdf_sc[h * 2 * n_new:h * 2 * n_new + n_new, cols]
            o2 = accdf_sc[h * 2 * n_new + n_new:(h + 1) * 2 * n_new, cols]
            o = o1 - lam * o2
            ms = jnp.mean(o * o, axis=-1, keepdims=True)
            ob_ref[:, cols] = o * lax.rsqrt(ms + EPS) * g_ref[...] * (1.0 - lam_init)


def _sample_attn(page_table, qsb, qdf, ka_n, va_n, kb_n, vb_n, blast, bnew, lams, subln,
                 c_sbk, c_sbv, c_dfk, c_dfv, layer, lam_init):
    n_seq, n_pages = page_table.shape
    n_new = ka_n.shape[0] // (n_seq * N_HEADS)
    assert 2 * N_HEADS * n_new == QCOLS
    g = PAGES_PER_STEP
    n_kv = n_pages // g
    past = n_pages * PAGE_SIZE

    assert n_pages % g == 0 and RING_DEPTH - 1 <= 2 * n_kv
    vec64 = pl.BlockSpec((1, DF_HALF), lambda b, s, pt: (0, 0))
    new_spec = pl.BlockSpec((n_new * N_HEADS, HEAD_DIM), lambda b, s, pt: (b, 0))
    out_spec = pl.BlockSpec((n_new, WIDTH), lambda b, s, pt: (b, 0))
    q_spec = pl.BlockSpec((QROWS, WIDTH), lambda b, s, pt: (b // (QROWS // n_new), 0))
    in_specs = [q_spec, q_spec,
                new_spec, new_spec, new_spec, new_spec,
                pl.BlockSpec((PAGE_SIZE, QCOLS), lambda b, s, pt: (0, 0)),
                pl.BlockSpec((n_new, QCOLS), lambda b, s, pt: (0, 0)),
                vec64, vec64, vec64, vec64,
                pl.BlockSpec((1, HEAD_DIM), lambda b, s, pt: (0, 0))]
    in_specs += [pl.BlockSpec(memory_space=pl.ANY)] * 4
    grid_spec = pltpu.PrefetchScalarGridSpec(
        num_scalar_prefetch=1,
        grid=(n_seq, 2 * n_kv),
        in_specs=in_specs,
        out_specs=[out_spec, out_spec],
        scratch_shapes=[pltpu.VMEM((past, QCOLS), F32), pltpu.VMEM((past, QCOLS), F32),
                        pltpu.VMEM((past, QCOLS), BF16), pltpu.VMEM((past, QCOLS), BF16),
                        pltpu.VMEM((QCOLS, WIDTH), F32), pltpu.VMEM((QCOLS, WIDTH), F32),
                        pltpu.VMEM((WIDTH, QCOLS), BF16), pltpu.VMEM((WIDTH, QCOLS), BF16),
                        pltpu.VMEM((RING_DEPTH * 2 * g, PAGE_SIZE * N_HEADS, HEAD_DIM), F32),
                        pltpu.SemaphoreType.DMA((RING_DEPTH,))])
    return pl.pallas_call(
        functools.partial(_sample_attn_kernel, n_seq=n_seq, n_pages=n_pages, n_new=n_new, layer=layer,
                          lam_init=lam_init),
        grid_spec=grid_spec,
        out_shape=[jax.ShapeDtypeStruct((n_seq * n_new, WIDTH), F32)] * 2,
        compiler_params=_cparams(("arbitrary", "arbitrary")),
        name="sample_attn",
    )(page_table.reshape(-1), qsb, qdf, ka_n, va_n, kb_n, vb_n, blast, bnew, *lams, subln,
      c_sbk, c_dfk, c_sbv, c_dfv)


def _merge_kernel(x_ref, oa_ref, ob_ref, ga_ref, gb_ref, wsb_ref, wdf_ref, wout_ref, g2_ref, x1_ref, hn_ref):
    ya = jnp.dot(oa_ref[...].astype(BF16), wsb_ref[...], preferred_element_type=F32)
    yb = jnp.dot(ob_ref[...].astype(BF16), wdf_ref[...], preferred_element_type=F32)
    m = jax.nn.sigmoid(ga_ref[...].astype(F32)) * ya + jax.nn.sigmoid(gb_ref[...].astype(F32)) * yb
    x1 = x_ref[...] + jnp.dot(m.astype(BF16), wout_ref[...], preferred_element_type=F32)
    x1_ref[...] = x1
    ms = jnp.mean(x1 * x1, axis=-1, keepdims=True)
    hn_ref[...] = (x1 * lax.rsqrt(ms + EPS) * g2_ref[...]).astype(BF16)


def _merge(x, oa, ob, gates, wsb, wdf, wout, norm2, tm):
    m, d = x.shape
    const = lambda shape: pl.BlockSpec(shape, lambda i: (0, 0), pipeline_mode=pl.Buffered(1))
    return pl.pallas_call(
        _merge_kernel,
        grid=(m // tm,),
        in_specs=[pl.BlockSpec((tm, d), lambda i: (i, 0)),
                  pl.BlockSpec((tm, WIDTH), lambda i: (i, 0)),
                  pl.BlockSpec((tm, WIDTH), lambda i: (i, 0)),
                  pl.BlockSpec((tm, d), lambda i: (i, 0)),
                  pl.BlockSpec((tm, d), lambda i: (i, 1)),
                  const((WIDTH, d)), const((WIDTH, d)), const((d, d)),
                  pl.BlockSpec((1, d), lambda i: (0, 0))],
        out_specs=[pl.BlockSpec((tm, d), lambda i: (i, 0)), pl.BlockSpec((tm, d), lambda i: (i, 0))],
        out_shape=[jax.ShapeDtypeStruct((m, d), F32), jax.ShapeDtypeStruct((m, d), BF16)],
        compiler_params=_cparams(("parallel",)),
        name="merge",
    )(x, oa, ob, gates, gates, wsb, wdf, wout, norm2.reshape(1, d))


def _mlp_kernel(x1_ref, hn_ref, wup_ref, wdn_ref, o_ref):
    f = pl.program_id(1)

    @pl.when(f == 0)
    def _():
        o_ref[...] = x1_ref[...]

    u = jnp.maximum(jnp.dot(hn_ref[...], wup_ref[...], preferred_element_type=F32), 0.0)
    o_ref[...] += jnp.dot((u * u).astype(BF16), wdn_ref[...], preferred_element_type=F32)


def _mlp(x1, hn, wup, wdn, tm, tf):
    m, d = x1.shape
    dff = wup.shape[1]
    return pl.pallas_call(
        _mlp_kernel,
        grid=(m // tm, dff // tf),
        in_specs=[pl.BlockSpec((tm, d), lambda i, f: (i, 0)),
                  pl.BlockSpec((tm, d), lambda i, f: (i, 0)),
                  pl.BlockSpec((d, tf), lambda i, f: (0, f)),
                  pl.BlockSpec((tf, d), lambda i, f: (f, 0))],
        out_specs=pl.BlockSpec((tm, d), lambda i, f: (i, 0)),
        out_shape=jax.ShapeDtypeStruct((m, d), F32),
        compiler_params=_cparams(("parallel", "arbitrary")),
        name="mlp",
    )(x1, hn, wup, wdn)


def _project_group(x, norm1, w_in, q_norm, k_norm, tm):
    h = _rmsnorm_bf16(x, norm1, min(tm, 512))
    (qa,) = _proj(h, w_in, COL_QA, 1, tm=tm, scale=SB_SCALE * LOG2E)
    ka, ka16 = _proj(h, w_in, COL_KA, 1, tm=tm, want_f32=True)
    va, va16 = _proj(h, w_in, COL_VA, 1, tm=tm, want_f32=True)
    (qb,) = _proj(h, w_in, COL_QB, 1, tm=tm, qknorm=True, gain=q_norm, scale=DF_SCALE)
    kb, kb16 = _proj(h, w_in, COL_KB, 1, tm=tm, qknorm=True, gain=k_norm, want_f32=True)
    vb, vb16 = _proj(h, w_in, COL_VB, 1, tm=tm, want_f32=True)
    (gates,) = _proj(h, w_in, COL_GA, 4, tm=tm)
    return dict(qa=qa, ka=ka, ka16=ka16, va=va, va16=va16, qb=qb, kb=kb, kb16=kb16, vb=vb, vb16=vb16, gates=gates)


def kernel(x_prompt, x_sample, cache_sb_k, cache_sb_v, cache_df_k, cache_df_v, page_table, rel_bias, norm1, w_in, q_norm, k_norm, lambda_q1, lambda_k1, lambda_q2, lambda_k2, subln, w_branch_sb, w_branch_df, w_out, norm2, w_up, w_down):
    depth = norm1.shape[0]
    batch, seq, d = x_prompt.shape
    n_seq, n_new, _ = x_sample.shape
    n_pages = page_table.shape[1]
    past = n_pages * PAGE_SIZE
    xp = x_prompt.reshape(batch * seq, d)
    xs = x_sample.reshape(n_seq * n_new, d)
    bias_p = _bias_prompt(rel_bias, ATT_T)
    bias_last, bias_new = _bias_sample(rel_bias, past, n_new)
    n_pool = cache_sb_k.shape[1]
    leaves = [[] for _ in range(8)]
    for l in range(depth):
        lam_init = 0.8 - 0.6 * math.exp(-0.3 * l)
        w_in16 = w_in[l].astype(BF16)
        wsb16 = w_branch_sb[l].astype(BF16)
        wdf16 = w_branch_df[l].astype(BF16)
        wout16 = w_out[l].astype(BF16)
        wup16 = w_up[l].astype(BF16)
        wdn16 = w_down[l].astype(BF16)
        lams = [v[l].reshape(1, DF_HALF).astype(F32) for v in (lambda_q1, lambda_k1, lambda_q2, lambda_k2)]
        sub = subln[l].reshape(1, HEAD_DIM).astype(F32)

        p = _project_group(xp, norm1[l], w_in16, q_norm[l], k_norm[l], 1024)
        oa = _sb_prompt(p["qa"], p["ka16"], p["va16"], batch, seq, ATT_T)
        ob = _df_prompt(p["qb"], p["kb16"], p["vb16"], bias_p, lams, sub, lam_init, batch, seq, ATT_T)
        x1, hn = _merge(xp, oa, ob, p["gates"], wsb16, wdf16, wout16, norm2[l], 256)
        xp = _mlp(x1, hn, wup16, wdn16, 512, 1024)
        for i, name in enumerate(("ka", "va", "kb", "vb")):
            leaves[i].append(p[name].reshape(batch, seq, N_HEADS, HEAD_DIM))

        s = _project_group(xs, norm1[l], w_in16, q_norm[l], k_norm[l], 1024)
        shape4 = (depth, n_pool, PAGE_SIZE * N_HEADS, HEAD_DIM)
        oa, ob = _sample_attn(page_table, s["qa"], s["qb"], s["ka"], s["va"], s["kb"], s["vb"], bias_last, bias_new,
                              lams, sub, cache_sb_k.reshape(shape4), cache_sb_v.reshape(shape4),
                              cache_df_k.reshape(shape4), cache_df_v.reshape(shape4), l, lam_init)
        x1, hn = _merge(xs, oa, ob, s["gates"], wsb16, wdf16, wout16, norm2[l], 256)
        xs = _mlp(x1, hn, wup16, wdn16, 512, 1024)
        for i, name in enumerate(("ka", "va", "kb", "vb")):
            leaves[4 + i].append(s[name].reshape(n_seq, n_new, N_HEADS, HEAD_DIM))

    return (xp.reshape(batch, seq, d), xs.reshape(n_seq, n_new, d)) + tuple(jnp.stack(v) for v in leaves)
```

```python
import functools
import math

import numpy as np
import jax
import jax.numpy as jnp
from jax import lax
from jax.experimental import pallas as pl
from jax.experimental.pallas import tpu as pltpu

F32 = jnp.float32
BF16 = jnp.bfloat16

D_MODEL = 2048
N_HEADS = 8
HEAD_DIM = 128
DF_HALF = 64
WIDTH = N_HEADS * HEAD_DIM
SB_SCALE = 1.0 / math.sqrt(HEAD_DIM)
DF_SCALE = 1.0 / math.sqrt(DF_HALF)
D_FF = 4 * D_MODEL
N_BUCKETS = 32
MAX_EXACT = N_BUCKETS // 2
MAX_DISTANCE = 128
EPS = 1e-6
PAGE_SIZE = 128
NEG_BIG = -1e30

COL_QA, COL_KA, COL_VA, COL_QB, COL_KB, COL_VB, COL_GA, COL_GB = 0, 1, 2, 3, 4, 5, 6, 8

ATT_T = 512
SB_HEADS = 4
DF_HEADS = 2
ATT_UNIT = 256
SB_DONE_BITS = 160.0
ONES_ROWS = 16
LOG2E = 1.4426950408889634
PAGES_PER_STEP = 8
RING_DEPTH = 3
QCOLS = 128
QROWS = 16

VMEM_LIMIT = 56 * 1024 * 1024


def _cparams(sem):
    return pltpu.CompilerParams(dimension_semantics=sem, vmem_limit_bytes=VMEM_LIMIT)


def _tokens_by_width(ref, n_tok):
    return jnp.concatenate([ref[pl.ds(h, n_tok, stride=N_HEADS), :] for h in range(N_HEADS)], axis=1)


def _softplus2(z):
    return jnp.maximum(z, 0.0) + jnp.log2(1.0 + jnp.exp2(-jnp.abs(z)))


def _emit_skewed(n_units, stages):
    state = [None] * n_units
    for tick in range(n_units + len(stages) - 1):
        for si in range(len(stages) - 1, -1, -1):
            u = tick - si
            if 0 <= u < n_units:
                state[u] = stages[si](u, state[u])
    return state


def _rmsnorm_kernel(x_ref, g_ref, o_ref):
    x = x_ref[...]
    ms = jnp.mean(x * x, axis=-1, keepdims=True)
    o_ref[...] = (x * lax.rsqrt(ms + EPS) * g_ref[...]).astype(o_ref.dtype)


def _rmsnorm_bf16(x, g, tm):
    m, d = x.shape
    return pl.pallas_call(
        _rmsnorm_kernel,
        grid=(m // tm,),
        in_specs=[pl.BlockSpec((tm, d), lambda i: (i, 0)),
                  pl.BlockSpec((1, d), lambda i: (0, 0))],
        out_specs=pl.BlockSpec((tm, d), lambda i: (i, 0)),
        out_shape=jax.ShapeDtypeStruct((m, d), BF16),
        compiler_params=_cparams(("parallel",)),
        name="rmsnorm_bf16",
    )(x, g.reshape(1, d))


def _proj_kernel(*refs, qknorm, scale, want_f32, want_bf16):
    h_ref, w_ref = refs[0], refs[1]
    pos = 2
    if qknorm:
        gain_ref, pmat_ref = refs[2], refs[3]
        pos = 4
    outs = refs[pos:]
    y = jnp.dot(h_ref[...], w_ref[...], preferred_element_type=F32)
    if qknorm:
        tn = y.shape[1]
        pieces = []
        for c in range(tn // 256):
            yb = y[:, c * 256:(c + 1) * 256]
            ms = jnp.dot((yb * yb).astype(BF16), pmat_ref[...], preferred_element_type=F32)
            pieces.append(yb * lax.rsqrt(ms + EPS))
        y = jnp.concatenate(pieces, axis=1) * gain_ref[...]
    k = 0
    if want_f32:
        for hd in range(N_HEADS):
            outs[k][pl.ds(hd, y.shape[0], stride=N_HEADS), :] = y[:, hd * HEAD_DIM:(hd + 1) * HEAD_DIM]
        k += 1
    if want_bf16:
        outs[k][...] = (y * scale).astype(BF16) if scale != 1.0 else y.astype(BF16)


def _proj(h, w, col0, ncol, *, tm, qknorm=False, gain=None, scale=1.0, want_f32=False, want_bf16=True):
    m, kdim = h.shape
    tn = WIDTH
    in_specs = [pl.BlockSpec((tm, kdim), lambda n, i: (i, 0)),
                pl.BlockSpec((kdim, tn), lambda n, i: (0, col0 + n))]
    args = [h, w]
    if qknorm:
        pmat = np.kron(np.eye(256 // DF_HALF), np.full((DF_HALF, DF_HALF), 1.0 / DF_HALF)).astype(np.float32)
        in_specs += [pl.BlockSpec((1, tn), lambda n, i: (0, 0)),
                     pl.BlockSpec((256, 256), lambda n, i: (0, 0))]
        args += [jnp.tile(gain.astype(F32), tn // DF_HALF).reshape(1, tn), jnp.asarray(pmat, dtype=BF16)]
    out_specs, out_shape = [], []
    if want_f32:
        assert ncol == 1
        out_specs.append(pl.BlockSpec((tm * N_HEADS, HEAD_DIM), lambda n, i: (i, 0)))
        out_shape.append(jax.ShapeDtypeStruct((m * N_HEADS, HEAD_DIM), F32))
    if want_bf16:
        out_specs.append(pl.BlockSpec((tm, tn), lambda n, i: (i, n)))
        out_shape.append(jax.ShapeDtypeStruct((m, ncol * tn), BF16))
    res = pl.pallas_call(
        functools.partial(_proj_kernel, qknorm=qknorm, scale=scale, want_f32=want_f32, want_bf16=want_bf16),
        grid=(ncol, m // tm),
        in_specs=in_specs,
        out_specs=out_specs,
        out_shape=out_shape,
        compiler_params=_cparams(("parallel", "parallel")),
        name="in_proj",
    )(*args)
    return res


def _bucket(rel):
    n = jnp.maximum(rel, 0)
    nf = jnp.maximum(n, 1).astype(F32)
    large = MAX_EXACT + (jnp.log(nf / MAX_EXACT) / math.log(MAX_DISTANCE / MAX_EXACT)
                         * (N_BUCKETS - MAX_EXACT)).astype(jnp.int32)
    large = jnp.minimum(large, N_BUCKETS - 1)
    return jnp.where(n < MAX_EXACT, n, large)


def _bias_prompt_kernel(rb_ref, o_ref, *, t):
    h = pl.program_id(0)
    keys = lax.broadcasted_iota(jnp.int32, (t, t), 0)
    qrys = lax.broadcasted_iota(jnp.int32, (t, t), 1)
    far = rb_ref[(N_BUCKETS - 1) * N_HEADS + h]
    for d in range(2):
        b = _bucket(qrys - keys + d * t)
        acc = jnp.zeros((t, t), F32)
        for k in range(N_BUCKETS - 1):
            acc = jnp.where(b == k, rb_ref[k * N_HEADS + h] - far, acc)
        if d == 0:
            acc = jnp.where(keys <= qrys, acc, NEG_BIG)
        o_ref[0, d] = acc
    o_ref[0, 2] = jnp.zeros((t, t), F32)


def _bias_prompt(rel_bias, t):
    return pl.pallas_call(
        functools.partial(_bias_prompt_kernel, t=t),
        grid=(N_HEADS,),
        in_specs=[pl.BlockSpec(memory_space=pltpu.SMEM)],
        out_specs=pl.BlockSpec((1, 3, t, t), lambda h: (h, 0, 0, 0)),
        out_shape=jax.ShapeDtypeStruct((N_HEADS, 3, t, t), F32),
        compiler_params=_cparams(("parallel",)),
        name="bias_prompt",
    )(rel_bias.reshape(-1))


def _bias_sample_kernel(rb_ref, last_ref, new_ref, *, past, n_new):
    def tile(rows, key0):
        lane = lax.broadcasted_iota(jnp.int32, (rows, QCOLS), 1)
        key = lax.broadcasted_iota(jnp.int32, (rows, QCOLS), 0) + key0
        head = lax.shift_right_logical(lane, (2 * n_new).bit_length() - 1)
        rel = past + (lane & (n_new - 1)) - key
        b = _bucket(rel)
        acc = jnp.zeros((rows, QCOLS), F32)
        for hh in range(N_HEADS):
            far = rb_ref[(N_BUCKETS - 1) * N_HEADS + hh]
            for k in range(N_BUCKETS - 1):
                acc = jnp.where((b == k) & (head == hh), rb_ref[k * N_HEADS + hh] - far, acc)
        return acc
    last_ref[...] = tile(PAGE_SIZE, past - PAGE_SIZE)
    new_ref[...] = tile(n_new, past)


def _bias_sample(rel_bias, past, n_new):
    return pl.pallas_call(
        functools.partial(_bias_sample_kernel, past=past, n_new=n_new),
        in_specs=[pl.BlockSpec(memory_space=pltpu.SMEM)],
        out_specs=[pl.BlockSpec((PAGE_SIZE, QCOLS), lambda: (0, 0)),
                   pl.BlockSpec((n_new, QCOLS), lambda: (0, 0))],
        out_shape=[jax.ShapeDtypeStruct((PAGE_SIZE, QCOLS), F32),
                   jax.ShapeDtypeStruct((n_new, QCOLS), F32)],
        name="bias_sample",
    )(rel_bias.reshape(-1))


def _lambda(lq1, lk1, lq2, lk2, lam_init):
    s1 = jnp.sum(lq1 * lk1, axis=-1, keepdims=True)
    s2 = jnp.sum(lq2 * lk2, axis=-1, keepdims=True)
    return jnp.exp(s1) - jnp.exp(s2) + lam_init


def _store_transposed(dst_ref, src_ref, t, row0=0):
    for ci in range(src_ref.shape[0] // t):
        chunk = src_ref[ci * t:(ci + 1) * t, :].astype(F32)
        dst_ref[row0:row0 + HEAD_DIM, ci * t:(ci + 1) * t] = chunk.T.astype(dst_ref.dtype)


def _head_cols(hh):
    return slice(hh * HEAD_DIM, (hh + 1) * HEAD_DIM)


def _sb_prompt_kernel(q_ref, k_ref, v_ref, upper_ref, o_ref, vt_sc, *, t):
    qi = pl.program_id(2)
    heads = range(SB_HEADS)

    @pl.when(qi == 0)
    def _():
        for hh in heads:
            _store_transposed(vt_sc.at[hh], v_ref.at[:, _head_cols(hh)], t)

    nsub = t // ATT_UNIT
    units = [(hh, cb) for hh in heads for cb in range(nsub)]
    qs = [q_ref[cb * ATT_UNIT:(cb + 1) * ATT_UNIT, _head_cols(hh)] for hh, cb in units]
    keys = lax.broadcasted_iota(jnp.int32, (ATT_UNIT, ATT_UNIT), 0)
    qrys = lax.broadcasted_iota(jnp.int32, (ATT_UNIT, ATT_UNIT), 1)
    tri = keys < qrys

    def block(j, carry, masked):
        start = pl.multiple_of(j * t, t)

        def scores(u, _):
            hh, _cb = units[u]
            return lax.dot_general(k_ref[pl.ds(start, t), _head_cols(hh)], qs[u], (((1,), (1,)), ((), ())),
                                   preferred_element_type=F32)

        def softplus(u, z):
            cb = units[u][1]
            c = carry[u][1]
            subs = []
            for kb in range(nsub - 1, -1, -1):
                if masked and kb > cb:
                    continue
                zk = z[kb * ATT_UNIT:(kb + 1) * ATT_UNIT]
                sp = _softplus2(zk)
                diag = masked and kb == cb
                if diag:
                    sp = jnp.where(tri, sp, 0.0)
                subs.append((kb, zk, sp, c, diag))
                c = c + jnp.sum(sp, axis=0, keepdims=True)
            return subs, c

        def suffixes(u, state):
            subs, c = state
            return [sub + (jnp.dot(upper_ref[...], sub[2].astype(BF16), preferred_element_type=F32),)
                    for sub in subs], c

        def weights(u, state):
            subs, c = state
            hh = units[u][0]
            acc = carry[u][0]
            for kb, zk, sp, c_in, diag, suffix in subs:
                log_a = zk - sp - suffix - c_in
                if diag:
                    log_a = jnp.where(tri, log_a, NEG_BIG)
                a = jnp.exp2(log_a).astype(BF16)
                acc = acc + jnp.dot(vt_sc[hh, :, pl.ds(start + kb * ATT_UNIT, ATT_UNIT)], a,
                                    preferred_element_type=F32)
            return acc, c

        return tuple(_emit_skewed(len(units), [scores, softplus, suffixes, weights]))

    carry = tuple((jnp.zeros((HEAD_DIM, ATT_UNIT), F32), jnp.zeros((1, ATT_UNIT), F32)) for _ in units)
    carry = block(qi, carry, True)

    def spent(cr):
        return functools.reduce(jnp.minimum, [jnp.min(c) for _, c in cr])

    def more(state):
        i, _, lowest = state
        return jnp.logical_and(i < qi, lowest < SB_DONE_BITS)

    def step(state):
        i, cr, _ = state
        cr = block(qi - 1 - i, cr, False)
        return i + 1, cr, spent(cr)

    _, carry, _ = lax.while_loop(more, step, (jnp.int32(0), carry, spent(carry)))
    for u, (hh, cb) in enumerate(units):
        o_ref[cb * ATT_UNIT:(cb + 1) * ATT_UNIT, _head_cols(hh)] = carry[u][0].T.astype(o_ref.dtype)


def _sb_prompt(q, k, v, batch, seq, t):
    nq = seq // t
    w = SB_HEADS * HEAD_DIM
    upper =jnp.asarray(np.triu(np.ones((ATT_UNIT, ATT_UNIT), np.float32), 1), dtype=BF16)
    return pl.pallas_call(
        functools.partial(_sb_prompt_kernel, t=t),
        grid=(batch, N_HEADS // SB_HEADS, nq),
        in_specs=[pl.BlockSpec((t, w), lambda b, h, i: (b * nq + i, h)),
                  pl.BlockSpec((seq, w), lambda b, h, i: (b, h)),
                  pl.BlockSpec((seq, w), lambda b, h, i: (b, h)),
                  pl.BlockSpec((ATT_UNIT, ATT_UNIT), lambda b, h, i: (0, 0))],
        out_specs=pl.BlockSpec((t, w), lambda b, h, i: (b * nq + i, h)),
        out_shape=jax.ShapeDtypeStruct((batch * seq, WIDTH), BF16),
        scratch_shapes=[pltpu.VMEM((SB_HEADS, HEAD_DIM, seq), BF16)],
        compiler_params=_cparams(("parallel", "parallel", "arbitrary")),
        name="sb_prompt",
    )(q, k, v, upper)


def _df_prompt_kernel(q_ref, k_ref, v_ref, bias_ref, lq1, lk1, lq2, lk2, g_ref, o_ref,
                      vt_sc, m_sc, l_sc, acc_sc, z_sc, *, t, lam_init):
    qi = pl.program_id(2)
    heads = range(DF_HEADS)

    @pl.when(qi == 0)
    def _():
        for hh in heads:
            _store_transposed(vt_sc.at[hh], v_ref.at[:, _head_cols(hh)], t)
            vt_sc[hh, HEAD_DIM:, :] = jnp.ones((ONES_ROWS, vt_sc.shape[2]), BF16)

    lane = lax.broadcasted_iota(jnp.int32, (t, HEAD_DIM), 1)
    qs = []
    for hh in heads:
        q = q_ref[:, _head_cols(hh)]
        zero = jnp.zeros_like(q)
        qs.append(jnp.concatenate([jnp.where(lane < DF_HALF, q, zero), jnp.where(lane >= DF_HALF, q, zero)], axis=0))
        m_sc[hh] = jnp.full(m_sc.shape[1:], NEG_BIG, F32)
        l_sc[hh] = jnp.zeros(l_sc.shape[1:], F32)
        acc_sc[hh] = jnp.zeros(acc_sc.shape[1:], F32)

    def scores(j, bias_d, slot):
        start = pl.multiple_of(j * t, t)
        for hh in heads:
            z = lax.dot_general(k_ref[pl.ds(start, t), _head_cols(hh)], qs[hh], (((1,), (1,)), ((), ())),
                                preferred_element_type=F32)
            bias = bias_ref[hh, bias_d]
            z_sc[slot, hh] = z + jnp.concatenate([bias, bias], axis=1)

    def consume(j, slot):
        for hh in heads:
            consume_head(j, z_sc[slot, hh], hh)

    def consume_head(j, z, hh):
        start = pl.multiple_of(j * t, t)
        m_old = m_sc[hh]
        m_new = jnp.maximum(m_old, jnp.max(z, axis=0, keepdims=True))
        p = jnp.exp(z - m_new).astype(BF16)
        alpha = jnp.exp(m_old - m_new)
        pv = jnp.dot(vt_sc[hh, :, pl.ds(start, t)], p, preferred_element_type=F32)
        l_sc[hh] = alpha * l_sc[hh] + pv[HEAD_DIM:HEAD_DIM + 1, :]
        acc_sc[hh] = alpha * acc_sc[hh] + pv[:HEAD_DIM, :]
        m_sc[hh] = m_new

    scores(qi, 0, 0)

    def pair(p, carry):
        j = qi - 2 * p
        scores(j - 1, jnp.minimum(2 * p + 1, 2), 1)
        consume(j, 0)
        scores(j - 2, 2, 0)
        consume(j - 1, 1)
        return carry
    lax.fori_loop(0, qi // 2, pair, 0)

    @pl.when(qi % 2 == 1)
    def _():
        scores(0, jnp.minimum(qi, 2), 1)
        consume(1, 0)
        consume(0, 1)

    @pl.when(qi % 2 == 0)
    def _():
        consume(0, 0)

    lam = _lambda(lq1[...], lk1[...], lq2[...], lk2[...], lam_init)
    for hh in heads:
        o = acc_sc[hh] / l_sc[hh]
        o = (o[:, :t] - lam * o[:, t:]).T
        ms = jnp.mean(o * o, axis=-1, keepdims=True)
        o_ref[:, _head_cols(hh)] = (o * lax.rsqrt(ms + EPS) * g_ref[...] * (1.0 - lam_init)).astype(o_ref.dtype)


def _df_prompt(q, k, v, bias, lams, subln, lam_init, batch, seq, t):
    nq = seq // t
    w = DF_HEADS * HEAD_DIM
    vec64 = pl.BlockSpec((1, DF_HALF), lambda b, h, i: (0, 0))
    return pl.pallas_call(
        functools.partial(_df_prompt_kernel, t=t, lam_init=lam_init),
        grid=(batch, N_HEADS // DF_HEADS, nq),
        in_specs=[pl.BlockSpec((t, w), lambda b, h, i: (b * nq + i, h)),
                  pl.BlockSpec((seq, w), lambda b, h, i: (b, h)),
                  pl.BlockSpec((seq, w), lambda b, h, i: (b, h)),
                  pl.BlockSpec((DF_HEADS, 3, t, t), lambda b, h, i: (h, 0, 0, 0)),
                  vec64, vec64, vec64, vec64,
                  pl.BlockSpec((1, HEAD_DIM), lambda b, h, i: (0, 0))],
        out_specs=pl.BlockSpec((t, w), lambda b, h, i: (b * nq + i, h)),
        out_shape=jax.ShapeDtypeStruct((batch * seq, WIDTH), BF16),
        scratch_shapes=[pltpu.VMEM((DF_HEADS, HEAD_DIM + ONES_ROWS, seq), BF16),
                        pltpu.VMEM((DF_HEADS, 1, 2 * t), F32), pltpu.VMEM((DF_HEADS, 1, 2 * t), F32),
                        pltpu.VMEM((DF_HEADS, HEAD_DIM, 2 * t), F32),
                        pltpu.VMEM((2, DF_HEADS, t, 2 * t), F32)],
        compiler_params=_cparams(("parallel", "parallel", "arbitrary")),
        name="df_prompt",
    )(q, k, v, bias, *lams, subln)


def _score_matrix(q_ref, seq_in_block, n_new, halves):
    row = lax.broadcasted_iota(jnp.int32, (QROWS, QCOLS), 0)
    lane = lax.broadcasted_iota(jnp.int32, (QROWS, QCOLS), 1)
    pick = (lane & (n_new - 1)) + seq_in_block * n_new == row
    if halves == 1:
        pick = pick & ((lane & n_new) == 0)
    spread = lax.dot_general(q_ref[...], jnp.where(pick, 1.0, 0.0).astype(BF16), (((0,), (0,)), ((), ())),
                             preferred_element_type=F32)
    feat = lax.broadcasted_iota(jnp.int32, (WIDTH, QCOLS), 0)
    col = lax.broadcasted_iota(jnp.int32, (WIDTH, QCOLS), 1)
    log2 = lambda v: v.bit_length() - 1
    keep = lax.shift_right_logical(feat, log2(HEAD_DIM)) == lax.shift_right_logical(col, log2(2 * n_new))
    if halves == 2:
        keep = keep & ((lax.shift_right_logical(feat, log2(DF_HALF)) & 1) == (lax.shift_right_logical(col, log2(n_new)) & 1))
    return jnp.where(keep, spread, 0.0).astype(BF16)


def _sample_attn_kernel(pt_ref, qa_ref, qb_ref, kan_ref, van_ref, kbn_ref, vbn_ref,
                        blast_ref, bnew_ref, lq1, lk1, lq2, lk2, g_ref, *rest,
                        n_seq, n_pages, n_new, layer, lam_init):
    g = PAGES_PER_STEP
    c_sbk, c_dfk, c_sbv, c_dfv, oa_ref, ob_ref = rest[:6]
    zsb_sc, zdf_sc, psb_sc, pdf_sc, accsb_sc, accdf_sc, qsb_sc, qdf_sc, ring, sems = rest[6:]
    b = pl.program_id(0)
    s = pl.program_id(1)
    n_kv = n_pages // g
    n_steps = 2 * n_kv
    past = n_pages * PAGE_SIZE
    chunk = g * PAGE_SIZE
    group = 2 * g

    def page_copy(cache, page, block, slot):
        return pltpu.make_async_copy(cache.at[layer, page], ring.at[block], sems.at[slot])

    def start_group(seq_i, step, slot):
        @pl.when(step < n_kv)
        def _():
            for i in range(g):
                page = pt_ref[seq_i * n_pages + step * g + i]
                page_copy(c_sbk, page, slot * group + i, slot).start()
                page_copy(c_dfk, page, slot * group + g + i, slot).start()

        @pl.when(step >= n_kv)
        def _():
            for i in range(g):
                page = pt_ref[seq_i * n_pages + (step - n_kv) * g + i]
                page_copy(c_sbv, page, slot * group + i, slot).start()
                page_copy(c_dfv, page, slot * group + g + i, slot).start()

    gidx = b * n_steps + s
    slot = lax.rem(gidx, RING_DEPTH)

    @pl.when(gidx == 0)
    def _():
        for d in range(RING_DEPTH - 1):
            start_group(jnp.int32(0), jnp.int32(d), jnp.int32(d))

    ahead = s + (RING_DEPTH - 1)
    wraps = ahead >= n_steps
    seq_ahead = jnp.where(wraps, b + 1, b)

    @pl.when(seq_ahead < n_seq)
    def _():
        start_group(seq_ahead, jnp.where(wraps, ahead - n_steps, ahead), lax.rem(gidx + (RING_DEPTH - 1), RING_DEPTH))

    for i in range(group):
        page_copy(c_sbk, 0, slot * group + i, slot).wait()

    def pages_bf16(first):
        return jnp.concatenate([_tokens_by_width(ring.at[slot * group + first + i], PAGE_SIZE).astype(BF16)
                                for i in range(g)], axis=0)

    def new_tokens(ref):
        return _tokens_by_width(ref, n_new)

    @pl.when(s == 0)
    def _queries():
        seq_in_block = lax.rem(b, QROWS // n_new)
        qsb_sc[...] = _score_matrix(qa_ref, seq_in_block, n_new, 1)
        qdf_sc[...] = _score_matrix(qb_ref, seq_in_block, n_new, 2)

    @pl.when(s < n_kv)
    def _scores():
        start = pl.multiple_of(s * chunk, chunk)
        zsb_sc[pl.ds(start, chunk), :] = jnp.dot(pages_bf16(0), qsb_sc[...], preferred_element_type=F32)
        zdf_sc[pl.ds(start, chunk), :] = jnp.dot(pages_bf16(g), qdf_sc[...], preferred_element_type=F32)

    def pad_rows_bf16(x):
        return jnp.concatenate([x, jnp.zeros_like(x)], axis=0).astype(BF16)

    @pl.when(s == n_kv - 1)
    def _weights():
        lane_n = lax.broadcasted_iota(jnp.int32, (n_new, QCOLS), 1)
        key_n = lax.broadcasted_iota(jnp.int32, (n_new, QCOLS), 0)
        qpos_n = lane_n & (n_new - 1)
        z_new = jnp.dot(pad_rows_bf16(new_tokens(kbn_ref)), qdf_sc[...], preferred_element_type=F32)[:n_new]
        z_new = z_new + bnew_ref[...]
        z_new = jnp.where(key_n <= qpos_n, z_new, NEG_BIG)
        last0 = past - PAGE_SIZE
        zdf_sc[last0:past, :] = zdf_sc[last0:past, :] + blast_ref[...]
        zp = zdf_sc[...]
        mx = jnp.maximum(jnp.max(zp, axis=0, keepdims=True), jnp.max(z_new, axis=0, keepdims=True))
        e_new = jnp.exp(z_new - mx)
        ep = jnp.exp(zp - mx)
        inv = 1.0 / (jnp.sum(ep, axis=0, keepdims=True) + jnp.sum(e_new, axis=0, keepdims=True))
        pdf_sc[...] = (ep * inv).astype(BF16)
        accdf_sc[...] = lax.dot_general(pad_rows_bf16(e_new * inv), pad_rows_bf16(new_tokens(vbn_ref)),
                                        (((0,), (0,)), ((), ())), preferred_element_type=F32)
        zs_new = jnp.dot(pad_rows_bf16(new_tokens(kan_ref)), qsb_sc[...], preferred_element_type=F32)[:n_new]
        strict = key_n < qpos_n
        sp_new = jnp.where(strict, _softplus2(zs_new), 0.0)
        carry = jnp.zeros((1, QCOLS), F32)
        suffix_rows = [None] * n_new
        for i in range(n_new - 1, -1, -1):
            suffix_rows[i] = carry
            carry = carry + sp_new[i:i + 1]
        suffix_new = jnp.concatenate(suffix_rows, axis=0)
        a_new = jnp.exp2(jnp.where(strict, zs_new - sp_new - suffix_new, NEG_BIG))
        accsb_sc[...] = lax.dot_general(pad_rows_bf16(a_new), pad_rows_bf16(new_tokens(van_ref)),
                                        (((0,), (0,)), ((), ())), preferred_element_type=F32)
        tc = 256
        rr = lax.broadcasted_iota(jnp.int32, (tc, tc), 0)
        cc = lax.broadcasted_iota(jnp.int32, (tc, tc), 1)
        upper = jnp.where(cc > rr, 1.0, 0.0).astype(BF16)
        for ci in range(past // tc - 1, -1, -1):
            zc = zsb_sc[ci * tc:(ci + 1) * tc, :]
            sp = _softplus2(zc)
            suffix = jnp.dot(upper, sp.astype(BF16), preferred_element_type=F32)
            psb_sc[ci * tc:(ci + 1) * tc, :] = jnp.exp2(zc - sp - suffix - carry).astype(BF16)
            carry = carry + jnp.sum(sp, axis=0, keepdims=True)

    @pl.when(s >= n_kv)
    def _values():
        start = pl.multiple_of((s - n_kv) * chunk, chunk)
        accsb_sc[...] += lax.dot_general(psb_sc[pl.ds(start, chunk), :], pages_bf16(0),
                                         (((0,), (0,)), ((), ())), preferred_element_type=F32)
        accdf_sc[...] += lax.dot_general(pdf_sc[pl.ds(start, chunk), :], pages_bf16(g),
                                         (((0,), (0,)), ((), ())), preferred_element_type=F32)

    @pl.when(s == 2 * n_kv - 1)
    def _finish():
        lam = _lambda(lq1[...], lk1[...], lq2[...], lk2[...], lam_init)
        for h in range(N_HEADS):
            cols = slice(h * HEAD_DIM, (h + 1) * HEAD_DIM)
            oa_ref[:, cols] = accsb_sc[h * 2 * n_new:h * 2 * n_new + n_new, cols]
            o1 = accdf_sc[h * 2 * n_new:h * 2 * n_new + n_new, cols]
            o2 = accdf_sc[h * 2 * n_new + n_new:(h + 1) * 2 * n_new, cols]
            o = o1 - lam * o2
            ms = jnp.mean(o * o, axis=-1, keepdims=True)
            ob_ref[:, cols] = o * lax.rsqrt(ms + EPS) * g_ref[...] * (1.0 - lam_init)


def _sample_attn(page_table, qsb, qdf, ka_n, va_n, kb_n, vb_n, blast, bnew, lams, subln,
                 c_sbk, c_sbv, c_dfk, c_dfv, layer, lam_init):
    n_seq, n_pages = page_table.shape
    n_new = ka_n.shape[0] // (n_seq * N_HEADS)
    assert 2 * N_HEADS * n_new == QCOLS
    g = PAGES_PER_STEP
    n_kv = n_pages // g
    past = n_pages * PAGE_SIZE

    assert n_pages % g == 0 and RING_DEPTH - 1 <= 2 * n_kv
    vec64 = pl.BlockSpec((1, DF_HALF), lambda b, s, pt: (0, 0))
    new_spec = pl.BlockSpec((n_new * N_HEADS, HEAD_DIM), lambda b, s, pt: (b, 0))
    out_spec = pl.BlockSpec((n_new, WIDTH), lambda b, s, pt: (b, 0))
    q_spec = pl.BlockSpec((QROWS, WIDTH), lambda b, s, pt: (b // (QROWS // n_new), 0))
    in_specs = [q_spec, q_spec,
                new_spec, new_spec, new_spec, new_spec,
                pl.BlockSpec((PAGE_SIZE, QCOLS), lambda b, s, pt: (0, 0)),
                pl.BlockSpec((n_new, QCOLS), lambda b, s, pt: (0, 0)),
                vec64, vec64, vec64, vec64,
                pl.BlockSpec((1, HEAD_DIM), lambda b, s, pt: (0, 0))]
    in_specs += [pl.BlockSpec(memory_space=pl.ANY)] * 4
    grid_spec = pltpu.PrefetchScalarGridSpec(
        num_scalar_prefetch=1,
        grid=(n_seq, 2 * n_kv),
        in_specs=in_specs,
        out_specs=[out_spec, out_spec],
        scratch_shapes=[pltpu.VMEM((past, QCOLS), F32), pltpu.VMEM((past, QCOLS), F32),
                        pltpu.VMEM((past, QCOLS), BF16), pltpu.VMEM((past, QCOLS), BF16),
                        pltpu.VMEM((QCOLS, WIDTH), F32), pltpu.VMEM((QCOLS, WIDTH), F32),
                        pltpu.VMEM((WIDTH, QCOLS), BF16), pltpu.VMEM((WIDTH, QCOLS), BF16),
                        pltpu.VMEM((RING_DEPTH * 2 * g, PAGE_SIZE * N_HEADS, HEAD_DIM), F32),
                        pltpu.SemaphoreType.DMA((RING_DEPTH,))])
    return pl.pallas_call(
        functools.partial(_sample_attn_kernel, n_seq=n_seq, n_pages=n_pages, n_new=n_new, layer=layer,
                          lam_init=lam_init),
        grid_spec=grid_spec,
        out_shape=[jax.ShapeDtypeStruct((n_seq * n_new, WIDTH), F32)] * 2,
        compiler_params=_cparams(("arbitrary", "arbitrary")),
        name="sample_attn",
    )(page_table.reshape(-1), qsb, qdf, ka_n, va_n, kb_n, vb_n, blast, bnew, *lams, subln,
      c_sbk, c_dfk, c_sbv, c_dfv)


def _merge_kernel(x_ref, oa_ref, ob_ref, ga_ref, gb_ref, wsb_ref, wdf_ref, wout_ref, g2_ref, x1_ref, hn_ref):
    ya = jnp.dot(oa_ref[...].astype(BF16), wsb_ref[...], preferred_element_type=F32)
    yb = jnp.dot(ob_ref[...].astype(BF16), wdf_ref[...], preferred_element_type=F32)
    m = jax.nn.sigmoid(ga_ref[...].astype(F32)) * ya + jax.nn.sigmoid(gb_ref[...].astype(F32)) * yb
    x1 = x_ref[...] + jnp.dot(m.astype(BF16), wout_ref[...], preferred_element_type=F32)
    x1_ref[...] = x1
    ms = jnp.mean(x1 * x1, axis=-1, keepdims=True)
    hn_ref[...] = (x1 * lax.rsqrt(ms + EPS) * g2_ref[...]).astype(BF16)


def _merge(x, oa, ob, gates, wsb, wdf, wout, norm2, tm):
    m, d = x.shape
    const = lambda shape: pl.BlockSpec(shape, lambda i: (0, 0), pipeline_mode=pl.Buffered(1))
    return pl.pallas_call(
        _merge_kernel,
        grid=(m // tm,),
        in_specs=[pl.BlockSpec((tm, d), lambda i: (i, 0)),
                  pl.BlockSpec((tm, WIDTH), lambda i: (i, 0)),
                  pl.BlockSpec((tm, WIDTH), lambda i: (i, 0)),
                  pl.BlockSpec((tm, d), lambda i: (i, 0)),
                  pl.BlockSpec((tm, d), lambda i: (i, 1)),
                  const((WIDTH, d)), const((WIDTH, d)), const((d, d)),
                  pl.BlockSpec((1, d), lambda i: (0, 0))],
        out_specs=[pl.BlockSpec((tm, d), lambda i: (i, 0)), pl.BlockSpec((tm, d), lambda i: (i, 0))],
        out_shape=[jax.ShapeDtypeStruct((m, d), F32), jax.ShapeDtypeStruct((m, d), BF16)],
        compiler_params=_cparams(("parallel",)),
        name="merge",
    )(x, oa, ob, gates, gates, wsb, wdf, wout, norm2.reshape(1, d))


def _mlp_kernel(x1_ref, hn_ref, wup_ref, wdn_ref, o_ref):
    f = pl.program_id(1)

    @pl.when(f == 0)
    def _():
        o_ref[...] = x1_ref[...]

    u = jnp.maximum(jnp.dot(hn_ref[...], wup_ref[...], preferred_element_type=F32), 0.0)
    o_ref[...] += jnp.dot((u * u).astype(BF16), wdn_ref[...], preferred_element_type=F32)


def _mlp(x1, hn, wup, wdn, tm, tf):
    m, d = x1.shape
    dff = wup.shape[1]
    return pl.pallas_call(
        _mlp_kernel,
        grid=(m // tm, dff // tf),
        in_specs=[pl.BlockSpec((tm, d), lambda i, f: (i, 0)),
                  pl.BlockSpec((tm, d), lambda i, f: (i, 0)),
                  pl.BlockSpec((d, tf), lambda i, f: (0, f)),
                  pl.BlockSpec((tf, d), lambda i, f: (f, 0))],
        out_specs=pl.BlockSpec((tm, d), lambda i, f: (i, 0)),
        out_shape=jax.ShapeDtypeStruct((m, d), F32),
        compiler_params=_cparams(("parallel", "arbitrary")),
        name="mlp",
    )(x1, hn, wup, wdn)


def _project_group(x, norm1, w_in, q_norm, k_norm, tm):
    h = _rmsnorm_bf16(x, norm1, min(tm, 512))
    (qa,) = _proj(h, w_in, COL_QA, 1, tm=tm, scale=SB_SCALE * LOG2E)
    ka, ka16 = _proj(h, w_in, COL_KA, 1, tm=tm, want_f32=True)
    va, va16 = _proj(h, w_in, COL_VA, 1, tm=tm, want_f32=True)
    (qb,) = _proj(h, w_in, COL_QB, 1, tm=tm, qknorm=True, gain=q_norm, scale=DF_SCALE)
    kb, kb16 = _proj(h, w_in, COL_KB, 1, tm=tm, qknorm=True, gain=k_norm, want_f32=True)
    vb, vb16 = _proj(h, w_in, COL_VB, 1, tm=tm, want_f32=True)
    (gates,) = _proj(h, w_in, COL_GA, 4, tm=tm)
    return dict(qa=qa, ka=ka, ka16=ka16, va=va, va16=va16, qb=qb, kb=kb, kb16=kb16, vb=vb, vb16=vb16, gates=gates)


def kernel(x_prompt, x_sample, cache_sb_k, cache_sb_v, cache_df_k, cache_df_v, page_table, rel_bias, norm1, w_in, q_norm, k_norm, lambda_q1, lambda_k1, lambda_q2, lambda_k2, subln, w_branch_sb, w_branch_df, w_out, norm2, w_up, w_down):
    depth = norm1.shape[0]
    batch, seq, d = x_prompt.shape
    n_seq, n_new, _ = x_sample.shape
    n_pages = page_table.shape[1]
    past = n_pages * PAGE_SIZE
    xp = x_prompt.reshape(batch * seq, d)
    xs = x_sample.reshape(n_seq * n_new, d)
    bias_p = _bias_prompt(rel_bias, ATT_T)
    bias_last, bias_new = _bias_sample(rel_bias, past, n_new)
    n_pool = cache_sb_k.shape[1]
    leaves = [[] for _ in range(8)]
    for l in range(depth):
        lam_init = 0.8 - 0.6 * math.exp(-0.3 * l)
        w_in16 = w_in[l].astype(BF16)
        wsb16 = w_branch_sb[l].astype(BF16)
        wdf16 = w_branch_df[l].astype(BF16)
        wout16 = w_out[l].astype(BF16)
        wup16 = w_up[l].astype(BF16)
        wdn16 = w_down[l].astype(BF16)
        lams = [v[l].reshape(1, DF_HALF).astype(F32) for v in (lambda_q1, lambda_k1, lambda_q2, lambda_k2)]
        sub = subln[l].reshape(1, HEAD_DIM).astype(F32)

        p = _project_group(xp, norm1[l], w_in16, q_norm[l], k_norm[l], 1024)
        oa = _sb_prompt(p["qa"], p["ka16"], p["va16"], batch, seq, ATT_T)
        ob = _df_prompt(p["qb"], p["kb16"], p["vb16"], bias_p, lams, sub, lam_init, batch, seq, ATT_T)
        x1, hn = _merge(xp, oa, ob, p["gates"], wsb16, wdf16, wout16, norm2[l], 256)
        xp = _mlp(x1, hn, wup16, wdn16, 512, 1024)
        for i, name in enumerate(("ka", "va", "kb", "vb")):
            leaves[i].append(p[name].reshape(batch, seq, N_HEADS, HEAD_DIM))

        s = _project_group(xs, norm1[l], w_in16, q_norm[l], k_norm[l], 1024)
        shape4 = (depth, n_pool, PAGE_SIZE * N_HEADS, HEAD_DIM)
        oa, ob = _sample_attn(page_table, s["qa"], s["qb"], s["ka"], s["va"], s["kb"], s["vb"], bias_last, bias_new,
                              lams, sub, cache_sb_k.reshape(shape4), cache_sb_v.reshape(shape4),
                              cache_df_k.reshape(shape4), cache_df_v.reshape(shape4), l, lam_init)
        x1, hn = _merge(xs, oa, ob, s["gates"], wsb16, wdf16, wout16, norm2[l], 256)
        xs = _mlp(x1, hn, wup16, wdn16, 512, 1024)
        for i, name in enumerate(("ka", "va", "kb", "vb")):
            leaves[4 + i].append(s[name].reshape(n_seq, n_new, N_HEADS, HEAD_DIM))

    return (xp.reshape(batch, seq, d), xs.reshape(n_seq, n_new, d)) + tuple(jnp.stack(v) for v in leaves)
```

```python
import functools
import math

import numpy as np
import jax
import jax.numpy as jnp
from jax import lax
from jax.experimental import pallas as pl
from jax.experimental.pallas import tpu as pltpu

F32 = jnp.float32
BF16 = jnp.bfloat16

D_MODEL = 2048
N_HEADS = 8
HEAD_DIM = 128
DF_HALF = 64
WIDTH = N_HEADS * HEAD_DIM
SB_SCALE = 1.0 / math.sqrt(HEAD_DIM)
DF_SCALE = 1.0 / math.sqrt(DF_HALF)
D_FF = 4 * D_MODEL
N_BUCKETS = 32
MAX_EXACT = N_BUCKETS // 2
MAX_DISTANCE = 128
EPS = 1e-6
PAGE_SIZE = 128
NEG_BIG = -1e30

COL_QA, COL_KA, COL_VA, COL_QB, COL_KB, COL_VB, COL_GA, COL_GB = 0, 1, 2, 3, 4, 5, 6, 8

ATT_T = 512
SB_HEADS = 4
DF_HEADS = 2
ATT_UNIT = 256
SB_DONE_BITS = 160.0
ONES_ROWS = 16
LOG2E = 1.4426950408889634
PAGES_PER_STEP = 8
SB_HOT_PAGES = 2
RING_DEPTH = 3
QCOLS = 128
QROWS = 16

VMEM_LIMIT = 56 * 1024 * 1024


def _cparams(sem):
    return pltpu.CompilerParams(dimension_semantics=sem, vmem_limit_bytes=VMEM_LIMIT)


def _tokens_by_width(ref, n_tok):
    return jnp.concatenate([ref[pl.ds(h, n_tok, stride=N_HEADS), :] for h in range(N_HEADS)], axis=1)


def _softplus2(z):
    return jnp.maximum(z, 0.0) + jnp.log2(1.0 + jnp.exp2(-jnp.abs(z)))


def _emit_skewed(n_units, stages):
    state = [None] * n_units
    for tick in range(n_units + len(stages) - 1):
        for si in range(len(stages) - 1, -1, -1):
            u = tick - si
            if 0 <= u < n_units:
                state[u] = stages[si](u, state[u])
    return state


def _rmsnorm_kernel(x_ref, g_ref, o_ref):
    x = x_ref[...]
    ms = jnp.mean(x * x, axis=-1, keepdims=True)
    o_ref[...] = (x * lax.rsqrt(ms + EPS) * g_ref[...]).astype(o_ref.dtype)


def _rmsnorm_bf16(x, g, tm):
    m, d = x.shape
    return pl.pallas_call(
        _rmsnorm_kernel,
        grid=(m // tm,),
        in_specs=[pl.BlockSpec((tm, d), lambda i: (i, 0)),
                  pl.BlockSpec((1, d), lambda i: (0, 0))],
        out_specs=pl.BlockSpec((tm, d), lambda i: (i, 0)),
        out_shape=jax.ShapeDtypeStruct((m, d), BF16),
        compiler_params=_cparams(("parallel",)),
        name="rmsnorm_bf16",
    )(x, g.reshape(1, d))


def _proj_kernel(*refs, qknorm, scale, want_f32, want_bf16):
    h_ref, w_ref = refs[0], refs[1]
    pos = 2
    if qknorm:
        gain_ref, pmat_ref = refs[2], refs[3]
        pos = 4
    outs = refs[pos:]
    y = jnp.dot(h_ref[...], w_ref[...], preferred_element_type=F32)
    if qknorm:
        tn = y.shape[1]
        pieces = []
        for c in range(tn // 256):
            yb = y[:, c * 256:(c + 1) * 256]
            ms = jnp.dot((yb * yb).astype(BF16), pmat_ref[...], preferred_element_type=F32)
            pieces.append(yb * lax.rsqrt(ms + EPS))
        y = jnp.concatenate(pieces, axis=1) * gain_ref[...]
    k = 0
    if want_f32:
        for hd in range(N_HEADS):
            outs[k][pl.ds(hd, y.shape[0], stride=N_HEADS), :] = y[:, hd * HEAD_DIM:(hd + 1) * HEAD_DIM]
        k += 1
    if want_bf16:
        outs[k][...] = (y * scale).astype(BF16) if scale != 1.0 else y.astype(BF16)


def _proj(h, w, col0, ncol, *, tm, qknorm=False, gain=None, scale=1.0, want_f32=False, want_bf16=True):
    m, kdim = h.shape
    tn = WIDTH
    in_specs = [pl.BlockSpec((tm, kdim), lambda n, i: (i, 0)),
                pl.BlockSpec((kdim, tn), lambda n, i: (0, col0 + n))]
    args = [h, w]
    if qknorm:
        pmat = np.kron(np.eye(256 // DF_HALF), np.full((DF_HALF, DF_HALF), 1.0 / DF_HALF)).astype(np.float32)
        in_specs += [pl.BlockSpec((1, tn), lambda n, i: (0, 0)),
                     pl.BlockSpec((256, 256), lambda n, i: (0, 0))]
        args += [jnp.tile(gain.astype(F32), tn // DF_HALF).reshape(1, tn), jnp.asarray(pmat, dtype=BF16)]
    out_specs, out_shape = [], []
    if want_f32:
        assert ncol == 1
        out_specs.append(pl.BlockSpec((tm * N_HEADS, HEAD_DIM), lambda n, i: (i, 0)))
        out_shape.append(jax.ShapeDtypeStruct((m * N_HEADS, HEAD_DIM), F32))
    if want_bf16:
        out_specs.append(pl.BlockSpec((tm, tn), lambda n, i: (i, n)))
        out_shape.append(jax.ShapeDtypeStruct((m, ncol * tn), BF16))
    res = pl.pallas_call(
        functools.partial(_proj_kernel, qknorm=qknorm, scale=scale, want_f32=want_f32, want_bf16=want_bf16),
        grid=(ncol, m // tm),
        in_specs=in_specs,
        out_specs=out_specs,
        out_shape=out_shape,
        compiler_params=_cparams(("parallel", "parallel")),
        name="in_proj",
    )(*args)
    return res


def _bucket(rel):
    n = jnp.maximum(rel, 0)
    nf = jnp.maximum(n, 1).astype(F32)
    large = MAX_EXACT + (jnp.log(nf / MAX_EXACT) / math.log(MAX_DISTANCE / MAX_EXACT)
                         * (N_BUCKETS - MAX_EXACT)).astype(jnp.int32)
    large = jnp.minimum(large, N_BUCKETS - 1)
    return jnp.where(n < MAX_EXACT, n, large)


def _bias_prompt_kernel(rb_ref, o_ref, *, t):
    h = pl.program_id(0)
    keys = lax.broadcasted_iota(jnp.int32, (t, t), 0)
    qrys = lax.broadcasted_iota(jnp.int32, (t, t), 1)
    far = rb_ref[(N_BUCKETS - 1) * N_HEADS + h]
    for d in range(2):
        b = _bucket(qrys - keys + d * t)
        acc = jnp.zeros((t, t), F32)
        for k in range(N_BUCKETS - 1):
            acc = jnp.where(b == k, rb_ref[k * N_HEADS + h] - far, acc)
        if d == 0:
            acc = jnp.where(keys <= qrys, acc, NEG_BIG)
        o_ref[0, d] = acc
    o_ref[0, 2] = jnp.zeros((t, t), F32)


def _bias_prompt(rel_bias, t):
    return pl.pallas_call(
        functools.partial(_bias_prompt_kernel, t=t),
        grid=(N_HEADS,),
        in_specs=[pl.BlockSpec(memory_space=pltpu.SMEM)],
        out_specs=pl.BlockSpec((1, 3, t, t), lambda h: (h, 0, 0, 0)),
        out_shape=jax.ShapeDtypeStruct((N_HEADS, 3, t, t), F32),
        compiler_params=_cparams(("parallel",)),
        name="bias_prompt",
    )(rel_bias.reshape(-1))


def _bias_sample_kernel(rb_ref, last_ref, new_ref, *, past, n_new):
    def tile(rows, key0):
        lane = lax.broadcasted_iota(jnp.int32, (rows, QCOLS), 1)
        key = lax.broadcasted_iota(jnp.int32, (rows, QCOLS), 0) + key0
        head = lax.shift_right_logical(lane, (2 * n_new).bit_length() - 1)
        rel = past + (lane & (n_new - 1)) - key
        b = _bucket(rel)
        acc = jnp.zeros((rows, QCOLS), F32)
        for hh in range(N_HEADS):
            far = rb_ref[(N_BUCKETS - 1) * N_HEADS + hh]
            for k in range(N_BUCKETS - 1):
                acc = jnp.where((b == k) & (head == hh), rb_ref[k * N_HEADS + hh] - far, acc)
        return acc
    last_ref[...] = tile(PAGE_SIZE, past - PAGE_SIZE)
    new_ref[...] = tile(n_new, past)


def _bias_sample(rel_bias, past, n_new):
    return pl.pallas_call(
        functools.partial(_bias_sample_kernel, past=past, n_new=n_new),
        in_specs=[pl.BlockSpec(memory_space=pltpu.SMEM)],
        out_specs=[pl.BlockSpec((PAGE_SIZE, QCOLS), lambda: (0, 0)),
                   pl.BlockSpec((n_new, QCOLS), lambda: (0, 0))],
        out_shape=[jax.ShapeDtypeStruct((PAGE_SIZE, QCOLS), F32),
                   jax.ShapeDtypeStruct((n_new, QCOLS), F32)],
        name="bias_sample",
    )(rel_bias.reshape(-1))


def _lambda(lq1, lk1, lq2, lk2, lam_init):
    s1 = jnp.sum(lq1 * lk1, axis=-1, keepdims=True)
    s2 = jnp.sum(lq2 * lk2, axis=-1, keepdims=True)
    return jnp.exp(s1) - jnp.exp(s2) + lam_init


def _store_transposed(dst_ref, src_ref, t, row0=0):
    for ci in range(src_ref.shape[0] // t):
        chunk = src_ref[ci * t:(ci + 1) * t, :].astype(F32)
        dst_ref[row0:row0 + HEAD_DIM, ci * t:(ci + 1) * t] = chunk.T.astype(dst_ref.dtype)


def _head_cols(hh):
    return slice(hh * HEAD_DIM, (hh + 1) * HEAD_DIM)


def _sb_prompt_kernel(q_ref, k_ref, v_ref, upper_ref, o_ref, vt_sc, *, t):
    qi = pl.program_id(2)
    heads = range(SB_HEADS)

    @pl.when(qi == 0)
    def _():
        for hh in heads:
            _store_transposed(vt_sc.at[hh], v_ref.at[:, _head_cols(hh)], t)

    nsub = t // ATT_UNIT
    units = [(hh, cb) for hh in heads for cb in range(nsub)]
    qs = [q_ref[cb * ATT_UNIT:(cb + 1) * ATT_UNIT, _head_cols(hh)] for hh, cb in units]
    keys = lax.broadcasted_iota(jnp.int32, (ATT_UNIT, ATT_UNIT), 0)
    qrys = lax.broadcasted_iota(jnp.int32, (ATT_UNIT, ATT_UNIT), 1)
    tri = keys < qrys

    def block(j, carry, masked):
        start = pl.multiple_of(j * t, t)

        def scores(u, _):
            hh, _cb = units[u]
            return lax.dot_general(k_ref[pl.ds(start, t), _head_cols(hh)], qs[u], (((1,), (1,)), ((), ())),
                                   preferred_element_type=F32)

        def softplus(u, z):
            cb = units[u][1]
            c = carry[u][1]
            subs = []
            for kb in range(nsub - 1, -1, -1):
                if masked and kb > cb:
                    continue
                zk = z[kb * ATT_UNIT:(kb + 1) * ATT_UNIT]
                sp = _softplus2(zk)
                diag = masked and kb == cb
                if diag:
                    sp = jnp.where(tri, sp, 0.0)
                subs.append((kb, zk, sp, c, diag))
                c = c + jnp.sum(sp, axis=0, keepdims=True)
            return subs, c

        def suffixes(u, state):
            subs, c = state
            return [sub + (jnp.dot(upper_ref[...], sub[2].astype(BF16), preferred_element_type=F32),)
                    for sub in subs], c

        def weights(u, state):
            subs, c = state
            hh = units[u][0]
            acc = carry[u][0]
            for kb, zk, sp, c_in, diag, suffix in subs:
                log_a = zk - sp - suffix - c_in
                if diag:
                    log_a = jnp.where(tri, log_a, NEG_BIG)
                a = jnp.exp2(log_a).astype(BF16)
                acc = acc + jnp.dot(vt_sc[hh, :, pl.ds(start + kb * ATT_UNIT, ATT_UNIT)], a,
                                    preferred_element_type=F32)
            return acc, c

        return tuple(_emit_skewed(len(units), [scores, softplus, suffixes, weights]))

    carry = tuple((jnp.zeros((HEAD_DIM, ATT_UNIT), F32), jnp.zeros((1, ATT_UNIT), F32)) for _ in units)
    carry = block(qi, carry, True)

    def spent(cr):
        return functools.reduce(jnp.minimum, [jnp.min(c) for _, c in cr])

    def more(state):
        i, _, lowest = state
        return jnp.logical_and(i < qi, lowest < SB_DONE_BITS)

    def step(state):
        i, cr, _ = state
        cr = block(qi - 1 - i, cr, False)
        return i + 1, cr, spent(cr)

    _, carry, _ = lax.while_loop(more, step, (jnp.int32(0), carry, spent(carry)))
    for u, (hh, cb) in enumerate(units):
        o_ref[cb * ATT_UNIT:(cb + 1) * ATT_UNIT, _head_cols(hh)] = carry[u][0].T.astype(o_ref.dtype)


def _sb_prompt(q, k, v, batch, seq, t):
    nq = seq // t
    w = SB_HEADS * HEAD_DIM
    upper =jnp.asarray(np.triu(np.ones((ATT_UNIT, ATT_UNIT), np.float32), 1), dtype=BF16)
    return pl.pallas_call(
        functools.partial(_sb_prompt_kernel, t=t),
        grid=(batch, N_HEADS // SB_HEADS, nq),
        in_specs=[pl.BlockSpec((t, w), lambda b, h, i: (b * nq + i, h)),
                  pl.BlockSpec((seq, w), lambda b, h, i: (b, h)),
                  pl.BlockSpec((seq, w), lambda b, h, i: (b, h)),
                  pl.BlockSpec((ATT_UNIT, ATT_UNIT), lambda b, h, i: (0, 0))],
        out_specs=pl.BlockSpec((t, w), lambda b, h, i: (b * nq + i, h)),
        out_shape=jax.ShapeDtypeStruct((batch * seq, WIDTH), BF16),
        scratch_shapes=[pltpu.VMEM((SB_HEADS, HEAD_DIM, seq), BF16)],
        compiler_params=_cparams(("parallel", "parallel", "arbitrary")),
        name="sb_prompt",
    )(q, k, v, upper)


def _df_prompt_kernel(q_ref, k_ref, v_ref, bias_ref, lq1, lk1, lq2, lk2, g_ref, o_ref,
                      vt_sc, m_sc, l_sc, acc_sc, z_sc, *, t, lam_init):
    qi = pl.program_id(2)
    heads = range(DF_HEADS)

    @pl.when(qi == 0)
    def _():
        for hh in heads:
            _store_transposed(vt_sc.at[hh], v_ref.at[:, _head_cols(hh)], t)
            vt_sc[hh, HEAD_DIM:, :] = jnp.ones((ONES_ROWS, vt_sc.shape[2]), BF16)

    lane = lax.broadcasted_iota(jnp.int32, (t, HEAD_DIM), 1)
    qs = []
    for hh in heads:
        q = q_ref[:, _head_cols(hh)]
        zero = jnp.zeros_like(q)
        qs.append(jnp.concatenate([jnp.where(lane < DF_HALF, q, zero), jnp.where(lane >= DF_HALF, q, zero)], axis=0))
        m_sc[hh] = jnp.full(m_sc.shape[1:], NEG_BIG, F32)
        l_sc[hh] = jnp.zeros(l_sc.shape[1:], F32)
        acc_sc[hh] = jnp.zeros(acc_sc.shape[1:], F32)

    def scores(j, bias_d, slot):
        start = pl.multiple_of(j * t, t)
        for hh in heads:
            z = lax.dot_general(k_ref[pl.ds(start, t), _head_cols(hh)], qs[hh], (((1,), (1,)), ((), ())),
                                preferred_element_type=F32)
            bias = bias_ref[hh, bias_d]
            z_sc[slot, hh] = z + jnp.concatenate([bias, bias], axis=1)

    def consume(j, slot):
        for hh in heads:
            consume_head(j, z_sc[slot, hh], hh)

    def consume_head(j, z, hh):
        start = pl.multiple_of(j * t, t)
        m_old = m_sc[hh]
        m_new = jnp.maximum(m_old, jnp.max(z, axis=0, keepdims=True))
        p = jnp.exp(z - m_new).astype(BF16)
        alpha = jnp.exp(m_old - m_new)
        pv = jnp.dot(vt_sc[hh, :, pl.ds(start, t)], p, preferred_element_type=F32)
        l_sc[hh] = alpha * l_sc[hh] + pv[HEAD_DIM:HEAD_DIM + 1, :]
        acc_sc[hh] = alpha * acc_sc[hh] + pv[:HEAD_DIM, :]
        m_sc[hh] = m_new

    scores(qi, 0, 0)

    def pair(p, carry):
        j = qi - 2 * p
        scores(j - 1, jnp.minimum(2 * p + 1, 2), 1)
        consume(j, 0)
        scores(j - 2, 2, 0)
        consume(j - 1, 1)
        return carry
    lax.fori_loop(0, qi // 2, pair, 0)

    @pl.when(qi % 2 == 1)
    def _():
        scores(0, jnp.minimum(qi, 2), 1)
        consume(1, 0)
        consume(0, 1)

    @pl.when(qi % 2 == 0)
    def _():
        consume(0, 0)

    lam = _lambda(lq1[...], lk1[...], lq2[...], lk2[...], lam_init)
    for hh in heads:
        o = acc_sc[hh] / l_sc[hh]
        o = (o[:, :t] - lam * o[:, t:]).T
        ms = jnp.mean(o * o, axis=-1, keepdims=True)
        o_ref[:, _head_cols(hh)] = (o * lax.rsqrt(ms + EPS) * g_ref[...] * (1.0 - lam_init)).astype(o_ref.dtype)


def _df_prompt(q, k, v, bias, lams, subln, lam_init, batch, seq, t):
    nq = seq // t
    w = DF_HEADS * HEAD_DIM
    vec64 = pl.BlockSpec((1, DF_HALF), lambda b, h, i: (0, 0))
    return pl.pallas_call(
        functools.partial(_df_prompt_kernel, t=t, lam_init=lam_init),
        grid=(batch, N_HEADS // DF_HEADS, nq),
        in_specs=[pl.BlockSpec((t, w), lambda b, h, i: (b * nq + i, h)),
                  pl.BlockSpec((seq, w), lambda b, h, i: (b, h)),
                  pl.BlockSpec((seq, w), lambda b, h, i: (b, h)),
                  pl.BlockSpec((DF_HEADS, 3, t, t), lambda b, h, i: (h, 0, 0, 0)),
                  vec64, vec64, vec64, vec64,
                  pl.BlockSpec((1, HEAD_DIM), lambda b, h, i: (0, 0))],
        out_specs=pl.BlockSpec((t, w), lambda b, h, i: (b * nq + i, h)),
        out_shape=jax.ShapeDtypeStruct((batch * seq, WIDTH), BF16),
        scratch_shapes=[pltpu.VMEM((DF_HEADS, HEAD_DIM + ONES_ROWS, seq), BF16),
                        pltpu.VMEM((DF_HEADS, 1, 2 * t), F32), pltpu.VMEM((DF_HEADS, 1, 2 * t), F32),
                        pltpu.VMEM((DF_HEADS, HEAD_DIM, 2 * t), F32),
                        pltpu.VMEM((2, DF_HEADS, t, 2 * t), F32)],
        compiler_params=_cparams(("parallel", "parallel", "arbitrary")),
        name="df_prompt",
    )(q, k, v, bias, *lams, subln)


def _score_matrix(q_ref, seq_in_block, n_new, halves):
    row = lax.broadcasted_iota(jnp.int32, (QROWS, QCOLS), 0)
    lane = lax.broadcasted_iota(jnp.int32, (QROWS, QCOLS), 1)
    pick = (lane & (n_new - 1)) + seq_in_block * n_new == row
    if halves == 1:
        pick = pick & ((lane & n_new) == 0)
    spread = lax.dot_general(q_ref[...], jnp.where(pick, 1.0, 0.0).astype(BF16), (((0,), (0,)), ((), ())),
                             preferred_element_type=F32)
    feat = lax.broadcasted_iota(jnp.int32, (WIDTH, QCOLS), 0)
    col = lax.broadcasted_iota(jnp.int32, (WIDTH, QCOLS), 1)
    log2 = lambda v: v.bit_length() - 1
    keep = lax.shift_right_logical(feat, log2(HEAD_DIM)) == lax.shift_right_logical(col, log2(2 * n_new))
    if halves == 2:
        keep = keep & ((lax.shift_right_logical(feat, log2(DF_HALF)) & 1) == (lax.shift_right_logical(col, log2(n_new)) & 1))
    return jnp.where(keep, spread, 0.0).astype(BF16)


def _sample_attn_kernel(pt_ref, qa_ref, qb_ref, kan_ref, van_ref, kbn_ref, vbn_ref,
                        blast_ref, bnew_ref, lq1, lk1, lq2, lk2, g_ref, *rest,
                        n_seq, n_pages, n_new, layer, lam_init):
    g = PAGES_PER_STEP
    c_sbk, c_dfk, c_sbv, c_dfv, oa_ref, ob_ref = rest[:6]
    zdf_sc, psb_sc, pdf_sc, accsb_sc, accdf_sc, qsb_sc, qdf_sc, ring, extra_buf, sems, extra_sem, extra_n = rest[6:]
    b = pl.program_id(0)
    s = pl.program_id(1)
    n_kv = n_pages // g
    n_steps = 2 * n_kv
    past = n_pages * PAGE_SIZE
    chunk = g * PAGE_SIZE
    hot = SB_HOT_PAGES * PAGE_SIZE
    group = g + SB_HOT_PAGES

    def page_copy(cache, page, block, slot):
        return pltpu.make_async_copy(cache.at[layer, page], ring.at[block], sems.at[slot])

    def group_copies(seq_i, step, slot):
        k = jnp.where(step < n_kv, step, step - n_kv)
        first = (n_kv - 1 - k) * g
        df = [(seq_i * n_pages + first + i, slot * group + i) for i in range(g)]
        sb = [(seq_i * n_pages + n_pages - SB_HOT_PAGES + i, slot * group + g + i) for i in range(SB_HOT_PAGES)]
        return df, sb, k == 0

    def start_group(seq_i, step, slot):
        df, sb, rides = group_copies(seq_i, step, slot)
        for in_phase, c_df, c_sb in ((step < n_kv, c_dfk, c_sbk), (step >= n_kv, c_dfv, c_sbv)):
            @pl.when(in_phase)
            def _():
                for idx, block in df:
                    page_copy(c_df, pt_ref[idx], block, slot).start()

            @pl.when(jnp.logical_and(in_phase, rides))
            def _():
                for idx, block in sb:
                    page_copy(c_sb, pt_ref[idx], block, slot).start()

    gidx = b * n_steps + s
    slot = lax.rem(gidx, RING_DEPTH)
    k_step = jnp.where(s < n_kv, s, s - n_kv)
    first_page = (n_kv - 1 - k_step) * g

    @pl.when(gidx == 0)
    def _():
        for d in range(RING_DEPTH - 1):
            start_group(jnp.int32(0), jnp.int32(d), jnp.int32(d))

    ahead = s + (RING_DEPTH - 1)
    wraps = ahead >= n_steps
    seq_ahead = jnp.where(wraps, b + 1, b)

    @pl.when(seq_ahead < n_seq)
    def _():
        start_group(seq_ahead, jnp.where(wraps, ahead - n_steps, ahead), lax.rem(gidx + (RING_DEPTH - 1), RING_DEPTH))

    for i in range(g):
        page_copy(c_dfk, 0, slot * group + i, slot).wait()

    @pl.when(k_step == 0)
    def _():
        for i in range(SB_HOT_PAGES):
            page_copy(c_sbk, 0, slot * group + g + i, slot).wait()

    def page_rows_bf16(ref):
        return _tokens_by_width(ref, PAGE_SIZE).astype(BF16)

    def pages_bf16(first, count):
        return jnp.concatenate([page_rows_bf16(ring.at[slot * group + first + i]) for i in range(count)], axis=0)

    def extra_page(cache, page):
        cp = pltpu.make_async_copy(cache.at[layer, pt_ref[b * n_pages + page]], extra_buf, extra_sem.at[0])
        cp.start()
        cp.wait()
        return page_rows_bf16(extra_buf)

    def new_tokens(ref):
        return _tokens_by_width(ref, n_new)

    @pl.when(s == 0)
    def _queries():
        seq_in_block = lax.rem(b, QROWS // n_new)
        qsb_sc[...] = _score_matrix(qa_ref, seq_in_block, n_new, 1)
        qdf_sc[...] = _score_matrix(qb_ref, seq_in_block, n_new, 2)

    @pl.when(s < n_kv)
    def _scores():
        start = pl.multiple_of(first_page * PAGE_SIZE, chunk)
        zdf_sc[pl.ds(start, chunk), :] = jnp.dot(pages_bf16(0, g), qdf_sc[...], preferred_element_type=F32)

    def pad_rows_bf16(x):
        return jnp.concatenate([x, jnp.zeros_like(x)], axis=0).astype(BF16)

    lane_n = lax.broadcasted_iota(jnp.int32, (n_new, QCOLS), 1)
    key_n = lax.broadcasted_iota(jnp.int32, (n_new, QCOLS), 0)
    qpos_n = lane_n & (n_new - 1)

    def sb_weights(z, carry, upper):
        sp = _softplus2(z)
        suffix = jnp.dot(upper, sp.astype(BF16), preferred_element_type=F32)
        return jnp.exp2(z - sp - suffix - carry).astype(BF16), carry + jnp.sum(sp, axis=0, keepdims=True)

    def unspent(carry):
        real = (lax.broadcasted_iota(jnp.int32, (1, QCOLS), 1) & n_new) == 0
        return jnp.min(jnp.where(real, carry, SB_DONE_BITS))

    @pl.when(s == 0)
    def _sb_weights():
        zs_new = jnp.dot(pad_rows_bf16(new_tokens(kan_ref)), qsb_sc[...], preferred_element_type=F32)[:n_new]
        strict = key_n < qpos_n
        sp_new = jnp.where(strict, _softplus2(zs_new), 0.0)
        carry = jnp.zeros((1, QCOLS), F32)
        suffix_rows = [None] * n_new
        for i in range(n_new - 1, -1, -1):
            suffix_rows[i] = carry
            carry = carry + sp_new[i:i + 1]
        suffix_new = jnp.concatenate(suffix_rows, axis=0)
        a_new = jnp.exp2(jnp.where(strict, zs_new - sp_new - suffix_new, NEG_BIG))
        accsb_sc[...] = lax.dot_general(pad_rows_bf16(a_new), pad_rows_bf16(new_tokens(van_ref)),
                                        (((0,), (0,)), ((), ())), preferred_element_type=F32)
        rr = lax.broadcasted_iota(jnp.int32, (hot, hot), 0)
        cc = lax.broadcasted_iota(jnp.int32, (hot, hot), 1)
        upper = jnp.where(cc > rr, 1.0, 0.0).astype(BF16)
        z_hot = jnp.dot(pages_bf16(g, SB_HOT_PAGES), qsb_sc[...], preferred_element_type=F32)
        a_hot, carry = sb_weights(z_hot, carry, upper)
        psb_sc[past - hot:past, :] = a_hot

        def more(state):
            page, _, lowest = state
            return jnp.logical_and(page >= 0, lowest < SB_DONE_BITS)

        def older(state):
            page, carry, _ = state
            z = jnp.dot(extra_page(c_sbk, page), qsb_sc[...], preferred_element_type=F32)
            a, carry = sb_weights(z, carry, upper[:PAGE_SIZE, :PAGE_SIZE])
            psb_sc[pl.ds(pl.multiple_of(page * PAGE_SIZE, PAGE_SIZE), PAGE_SIZE), :] = a
            return page - 1, carry, unspent(carry)

        oldest_hot = n_pages - SB_HOT_PAGES
        page, _, _ = lax.while_loop(more, older, (jnp.int32(oldest_hot - 1), carry, unspent(carry)))
        extra_n[0] = oldest_hot - 1 - page

    @pl.when(s == n_kv - 1)
    def _weights():
        z_new = jnp.dot(pad_rows_bf16(new_tokens(kbn_ref)), qdf_sc[...], preferred_element_type=F32)[:n_new]
        z_new = z_new + bnew_ref[...]
        z_new = jnp.where(key_n <= qpos_n, z_new, NEG_BIG)
        last0 = past - PAGE_SIZE
        zdf_sc[last0:past, :] = zdf_sc[last0:past, :] + blast_ref[...]
        zp = zdf_sc[...]
        mx = jnp.maximum(jnp.max(zp, axis=0, keepdims=True), jnp.max(z_new, axis=0, keepdims=True))
        e_new = jnp.exp(z_new - mx)
        ep = jnp.exp(zp - mx)
        inv = 1.0 / (jnp.sum(ep, axis=0, keepdims=True) + jnp.sum(e_new, axis=0, keepdims=True))
        pdf_sc[...] = (ep * inv).astype(BF16)
        accdf_sc[...] = lax.dot_general(pad_rows_bf16(e_new * inv), pad_rows_bf16(new_tokens(vbn_ref)),
                                        (((0,), (0,)), ((), ())), preferred_element_type=F32)

    @pl.when(s >= n_kv)
    def _values():
        start = pl.multiple_of(first_page * PAGE_SIZE, chunk)
        accdf_sc[...] += lax.dot_general(pdf_sc[pl.ds(start, chunk), :], pages_bf16(0, g),
                                         (((0,), (0,)), ((), ())), preferred_element_type=F32)

    @pl.when(s == n_kv)
    def _sb_values():
        accsb_sc[...] += lax.dot_general(psb_sc[past - hot:past, :], pages_bf16(g, SB_HOT_PAGES),
                                         (((0,), (0,)), ((), ())), preferred_element_type=F32)

        def older(i, carry):
            page = n_pages - SB_HOT_PAGES - 1 - i
            rows = pl.ds(pl.multiple_of(page * PAGE_SIZE, PAGE_SIZE), PAGE_SIZE)
            accsb_sc[...] += lax.dot_general(psb_sc[rows, :], extra_page(c_sbv, page),
                                             (((0,), (0,)), ((), ())), preferred_element_type=F32)
            return carry
        lax.fori_loop(0, extra_n[0], older, 0)

    @pl.when(s == 2 * n_kv - 1)
    def _finish():
        lam = _lambda(lq1[...], lk1[...], lq2[...], lk2[...], lam_init)
        for h in range(N_HEADS):
            cols = slice(h * HEAD_DIM, (h + 1) * HEAD_DIM)
            oa_ref[:, cols] = accsb_sc[h * 2 * n_new:h * 2 * n_new + n_new, cols]
            o1 = accdf_sc[h * 2 * n_new:h * 2 * n_new + n_new, cols]
            o2 = accdf_sc[h * 2 * n_new + n_new:(h + 1) * 2 * n_new, cols]
            o = o1 - lam * o2
            ms = jnp.mean(o * o, axis=-1, keepdims=True)
            ob_ref[:, cols] = o * lax.rsqrt(ms + EPS) * g_ref[...] * (1.0 - lam_init)


def _sample_attn(page_table, qsb, qdf, ka_n, va_n, kb_n, vb_n, blast, bnew, lams, subln,
                 c_sbk, c_sbv, c_dfk, c_dfv, layer, lam_init):
    n_seq, n_pages = page_table.shape
    n_new = ka_n.shape[0] // (n_seq * N_HEADS)
    assert 2 * N_HEADS * n_new == QCOLS
    g = PAGES_PER_STEP
    n_kv = n_pages // g
    past = n_pages * PAGE_SIZE

    assert n_pages % g == 0 and RING_DEPTH - 1 <= 2 * n_kv and SB_HOT_PAGES <= n_pages
    vec64 = pl.BlockSpec((1, DF_HALF), lambda b, s, pt: (0, 0))
    new_spec = pl.BlockSpec((n_new * N_HEADS, HEAD_DIM), lambda b, s, pt: (b, 0))
    out_spec = pl.BlockSpec((n_new, WIDTH), lambda b, s, pt: (b, 0))
    q_spec = pl.BlockSpec((QROWS, WIDTH), lambda b, s, pt: (b // (QROWS // n_new), 0))
    in_specs = [q_spec, q_spec,
                new_spec, new_spec, new_spec, new_spec,
                pl.BlockSpec((PAGE_SIZE, QCOLS), lambda b, s, pt: (0, 0)),
                pl.BlockSpec((n_new, QCOLS), lambda b, s, pt: (0, 0)),
                vec64, vec64, vec64, vec64,
                pl.BlockSpec((1, HEAD_DIM), lambda b, s, pt: (0, 0))]
    in_specs += [pl.BlockSpec(memory_space=pl.ANY)] * 4
    grid_spec = pltpu.PrefetchScalarGridSpec(
        num_scalar_prefetch=1,
        grid=(n_seq, 2 * n_kv),
        in_specs=in_specs,
        out_specs=[out_spec, out_spec],
        scratch_shapes=[pltpu.VMEM((past, QCOLS), F32),
                        pltpu.VMEM((past, QCOLS), BF16), pltpu.VMEM((past, QCOLS), BF16),
                        pltpu.VMEM((QCOLS, WIDTH), F32), pltpu.VMEM((QCOLS, WIDTH), F32),
                        pltpu.VMEM((WIDTH, QCOLS), BF16), pltpu.VMEM((WIDTH, QCOLS), BF16),
                        pltpu.VMEM((RING_DEPTH * (g + SB_HOT_PAGES), PAGE_SIZE * N_HEADS, HEAD_DIM), F32),
                        pltpu.VMEM((PAGE_SIZE * N_HEADS, HEAD_DIM), F32),
                        pltpu.SemaphoreType.DMA((RING_DEPTH,)), pltpu.SemaphoreType.DMA((1,)),
                        pltpu.SMEM((1,), jnp.int32)])
    return pl.pallas_call(
        functools.partial(_sample_attn_kernel, n_seq=n_seq, n_pages=n_pages, n_new=n_new, layer=layer,
                          lam_init=lam_init),
        grid_spec=grid_spec,
        out_shape=[jax.ShapeDtypeStruct((n_seq * n_new, WIDTH), F32)] * 2,
        compiler_params=_cparams(("arbitrary", "arbitrary")),
        name="sample_attn",
    )(page_table.reshape(-1), qsb, qdf, ka_n, va_n, kb_n, vb_n, blast, bnew, *lams, subln,
      c_sbk, c_dfk, c_sbv, c_dfv)


def _merge_kernel(x_ref, oa_ref, ob_ref, ga_ref, gb_ref, wsb_ref, wdf_ref, wout_ref, g2_ref, x1_ref, hn_ref):
    ya = jnp.dot(oa_ref[...].astype(BF16), wsb_ref[...], preferred_element_type=F32)
    yb = jnp.dot(ob_ref[...].astype(BF16), wdf_ref[...], preferred_element_type=F32)
    m = jax.nn.sigmoid(ga_ref[...].astype(F32)) * ya + jax.nn.sigmoid(gb_ref[...].astype(F32)) * yb
    x1 = x_ref[...] + jnp.dot(m.astype(BF16), wout_ref[...], preferred_element_type=F32)
    x1_ref[...] = x1
    ms = jnp.mean(x1 * x1, axis=-1, keepdims=True)
    hn_ref[...] = (x1 * lax.rsqrt(ms + EPS) * g2_ref[...]).astype(BF16)


def _merge(x, oa, ob, gates, wsb, wdf, wout, norm2, tm):
    m, d = x.shape
    const = lambda shape: pl.BlockSpec(shape, lambda i: (0, 0), pipeline_mode=pl.Buffered(1))
    return pl.pallas_call(
        _merge_kernel,
        grid=(m // tm,),
        in_specs=[pl.BlockSpec((tm, d), lambda i: (i, 0)),
                  pl.BlockSpec((tm, WIDTH), lambda i: (i, 0)),
                  pl.BlockSpec((tm, WIDTH), lambda i: (i, 0)),
                  pl.BlockSpec((tm, d), lambda i: (i, 0)),
                  pl.BlockSpec((tm, d), lambda i: (i, 1)),
                  const((WIDTH, d)), const((WIDTH, d)), const((d, d)),
                  pl.BlockSpec((1, d), lambda i: (0, 0))],
        out_specs=[pl.BlockSpec((tm, d), lambda i: (i, 0)), pl.BlockSpec((tm, d), lambda i: (i, 0))],
        out_shape=[jax.ShapeDtypeStruct((m, d), F32), jax.ShapeDtypeStruct((m, d), BF16)],
        compiler_params=_cparams(("parallel",)),
        name="merge",
    )(x, oa, ob, gates, gates, wsb, wdf, wout, norm2.reshape(1, d))


def _mlp_kernel(x1_ref, hn_ref, wup_ref, wdn_ref, o_ref):
    f = pl.program_id(1)

    @pl.when(f == 0)
    def _():
        o_ref[...] = x1_ref[...]

    u = jnp.maximum(jnp.dot(hn_ref[...], wup_ref[...], preferred_element_type=F32), 0.0)
    o_ref[...] += jnp.dot((u * u).astype(BF16), wdn_ref[...], preferred_element_type=F32)


def _mlp(x1, hn, wup, wdn, tm, tf):
    m, d = x1.shape
    dff = wup.shape[1]
    return pl.pallas_call(
        _mlp_kernel,
        grid=(m // tm, dff // tf),
        in_specs=[pl.BlockSpec((tm, d), lambda i, f: (i, 0)),
                  pl.BlockSpec((tm, d), lambda i, f: (i, 0)),
                  pl.BlockSpec((d, tf), lambda i, f: (0, f)),
                  pl.BlockSpec((tf, d), lambda i, f: (f, 0))],
        out_specs=pl.BlockSpec((tm, d), lambda i, f: (i, 0)),
        out_shape=jax.ShapeDtypeStruct((m, d), F32),
        compiler_params=_cparams(("parallel", "arbitrary")),
        name="mlp",
    )(x1, hn, wup, wdn)


def _project_group(x, norm1, w_in, q_norm, k_norm, tm):
    h = _rmsnorm_bf16(x, norm1, min(tm, 512))
    (qa,) = _proj(h, w_in, COL_QA, 1, tm=tm, scale=SB_SCALE * LOG2E)
    ka, ka16 = _proj(h, w_in, COL_KA, 1, tm=tm, want_f32=True)
    va, va16 = _proj(h, w_in, COL_VA, 1, tm=tm, want_f32=True)
    (qb,) = _proj(h, w_in, COL_QB, 1, tm=tm, qknorm=True, gain=q_norm, scale=DF_SCALE)
    kb, kb16 = _proj(h, w_in, COL_KB, 1, tm=tm, qknorm=True, gain=k_norm, want_f32=True)
    vb, vb16 = _proj(h, w_in, COL_VB, 1, tm=tm, want_f32=True)
    (gates,) = _proj(h, w_in, COL_GA, 4, tm=tm)
    return dict(qa=qa, ka=ka, ka16=ka16, va=va, va16=va16, qb=qb, kb=kb, kb16=kb16, vb=vb, vb16=vb16, gates=gates)


def kernel(x_prompt, x_sample, cache_sb_k, cache_sb_v, cache_df_k, cache_df_v, page_table, rel_bias, norm1, w_in, q_norm, k_norm, lambda_q1, lambda_k1, lambda_q2, lambda_k2, subln, w_branch_sb, w_branch_df, w_out, norm2, w_up, w_down):
    depth = norm1.shape[0]
    batch, seq, d = x_prompt.shape
    n_seq, n_new, _ = x_sample.shape
    n_pages = page_table.shape[1]
    past = n_pages * PAGE_SIZE
    xp = x_prompt.reshape(batch * seq, d)
    xs = x_sample.reshape(n_seq * n_new, d)
    bias_p = _bias_prompt(rel_bias, ATT_T)
    bias_last, bias_new = _bias_sample(rel_bias, past, n_new)
    n_pool = cache_sb_k.shape[1]
    leaves = [[] for _ in range(8)]
    for l in range(depth):
        lam_init = 0.8 - 0.6 * math.exp(-0.3 * l)
        w_in16 = w_in[l].astype(BF16)
        wsb16 = w_branch_sb[l].astype(BF16)
        wdf16 = w_branch_df[l].astype(BF16)
        wout16 = w_out[l].astype(BF16)
        wup16 = w_up[l].astype(BF16)
        wdn16 = w_down[l].astype(BF16)
        lams = [v[l].reshape(1, DF_HALF).astype(F32) for v in (lambda_q1, lambda_k1, lambda_q2, lambda_k2)]
        sub = subln[l].reshape(1, HEAD_DIM).astype(F32)

        p = _project_group(xp, norm1[l], w_in16, q_norm[l], k_norm[l], 1024)
        oa = _sb_prompt(p["qa"], p["ka16"], p["va16"], batch, seq, ATT_T)
        ob = _df_prompt(p["qb"], p["kb16"], p["vb16"], bias_p, lams, sub, lam_init, batch, seq, ATT_T)
        x1, hn = _merge(xp, oa, ob, p["gates"], wsb16, wdf16, wout16, norm2[l], 256)
        xp = _mlp(x1, hn, wup16, wdn16, 512, 1024)
        for i, name in enumerate(("ka", "va", "kb", "vb")):
            leaves[i].append(p[name].reshape(batch, seq, N_HEADS, HEAD_DIM))

        s = _project_group(xs, norm1[l], w_in16, q_norm[l], k_norm[l], 1024)
        shape4 = (depth, n_pool, PAGE_SIZE * N_HEADS, HEAD_DIM)
        oa, ob = _sample_attn(page_table, s["qa"], s["qb"], s["ka"], s["va"], s["kb"], s["vb"], bias_last, bias_new,
                              lams, sub, cache_sb_k.reshape(shape4), cache_sb_v.reshape(shape4),
                              cache_df_k.reshape(shape4), cache_df_v.reshape(shape4), l, lam_init)
        x1, hn = _merge(xs, oa, ob, s["gates"], wsb16, wdf16, wout16, norm2[l], 256)
        xs = _mlp(x1, hn, wup16, wdn16, 512, 1024)
        for i, name in enumerate(("ka", "va", "kb", "vb")):
            leaves[4 + i].append(s[name].reshape(n_seq, n_new, N_HEADS, HEAD_DIM))

    return (xp.reshape(batch, seq, d), xs.reshape(n_seq, n_new, d)) + tuple(jnp.stack(v) for v in leaves)
```

```python
import functools
import math

import numpy as np
import jax
import jax.numpy as jnp
from jax import lax
from jax.experimental import pallas as pl
from jax.experimental.pallas import tpu as pltpu

F32 = jnp.float32
BF16 = jnp.bfloat16

D_MODEL = 2048
N_HEADS = 8
HEAD_DIM = 128
DF_HALF = 64
WIDTH = N_HEADS * HEAD_DIM
SB_SCALE = 1.0 / math.sqrt(HEAD_DIM)
DF_SCALE = 1.0 / math.sqrt(DF_HALF)
D_FF = 4 * D_MODEL
N_BUCKETS = 32
MAX_EXACT = N_BUCKETS // 2
MAX_DISTANCE = 128
EPS = 1e-6
PAGE_SIZE = 128
NEG_BIG = -1e30

COL_QA, COL_KA, COL_VA, COL_QB, COL_KB, COL_VB, COL_GA, COL_GB = 0, 1, 2, 3, 4, 5, 6, 8

ATT_T = 512
SB_HEADS = 4
DF_HEADS = 2
ATT_UNIT = 256
SB_DONE_BITS = 160.0
ONES_ROWS = 16
LOG2E = 1.4426950408889634
PAGES_PER_STEP = 16
SB_HOT_PAGES = 2
RING_DEPTH = 3
QCOLS = 128
QROWS = 16

VMEM_LIMIT = 56 * 1024 * 1024


def _cparams(sem):
    return pltpu.CompilerParams(dimension_semantics=sem, vmem_limit_bytes=VMEM_LIMIT)


def _tokens_by_width(ref, n_tok):
    return jnp.concatenate([ref[pl.ds(h, n_tok, stride=N_HEADS), :] for h in range(N_HEADS)], axis=1)


def _softplus2(z):
    return jnp.maximum(z, 0.0) + jnp.log2(1.0 + jnp.exp2(-jnp.abs(z)))


def _emit_skewed(n_units, stages):
    state = [None] * n_units
    for tick in range(n_units + len(stages) - 1):
        for si in range(len(stages) - 1, -1, -1):
            u = tick - si
            if 0 <= u < n_units:
                state[u] = stages[si](u, state[u])
    return state


def _rmsnorm_kernel(x_ref, g_ref, o_ref):
    x = x_ref[...]
    ms = jnp.mean(x * x, axis=-1, keepdims=True)
    o_ref[...] = (x * lax.rsqrt(ms + EPS) * g_ref[...]).astype(o_ref.dtype)


def _rmsnorm_bf16(x, g, tm):
    m, d = x.shape
    return pl.pallas_call(
        _rmsnorm_kernel,
        grid=(m // tm,),
        in_specs=[pl.BlockSpec((tm, d), lambda i: (i, 0)),
                  pl.BlockSpec((1, d), lambda i: (0, 0))],
        out_specs=pl.BlockSpec((tm, d), lambda i: (i, 0)),
        out_shape=jax.ShapeDtypeStruct((m, d), BF16),
        compiler_params=_cparams(("parallel",)),
        name="rmsnorm_bf16",
    )(x, g.reshape(1, d))


def _proj_kernel(*refs, qknorm, scale, want_f32, want_bf16):
    h_ref, w_ref = refs[0], refs[1]
    pos = 2
    if qknorm:
        gain_ref, pmat_ref = refs[2], refs[3]
        pos = 4
    outs = refs[pos:]
    y = jnp.dot(h_ref[...], w_ref[...], preferred_element_type=F32)
    if qknorm:
        tn = y.shape[1]
        pieces = []
        for c in range(tn // 256):
            yb = y[:, c * 256:(c + 1) * 256]
            ms = jnp.dot((yb * yb).astype(BF16), pmat_ref[...], preferred_element_type=F32)
            pieces.append(yb * lax.rsqrt(ms + EPS))
        y = jnp.concatenate(pieces, axis=1) * gain_ref[...]
    k = 0
    if want_f32:
        for hd in range(N_HEADS):
            outs[k][pl.ds(hd, y.shape[0], stride=N_HEADS), :] = y[:, hd * HEAD_DIM:(hd + 1) * HEAD_DIM]
        k += 1
    if want_bf16:
        outs[k][...] = (y * scale).astype(BF16) if scale != 1.0 else y.astype(BF16)


def _proj(h, w, col0, ncol, *, tm, qknorm=False, gain=None, scale=1.0, want_f32=False, want_bf16=True):
    m, kdim = h.shape
    tn = WIDTH
    in_specs = [pl.BlockSpec((tm, kdim), lambda n, i: (i, 0)),
                pl.BlockSpec((kdim, tn), lambda n, i: (0, col0 + n))]
    args = [h, w]
    if qknorm:
        pmat = np.kron(np.eye(256 // DF_HALF), np.full((DF_HALF, DF_HALF), 1.0 / DF_HALF)).astype(np.float32)
        in_specs += [pl.BlockSpec((1, tn), lambda n, i: (0, 0)),
                     pl.BlockSpec((256, 256), lambda n, i: (0, 0))]
        args += [jnp.tile(gain.astype(F32), tn // DF_HALF).reshape(1, tn), jnp.asarray(pmat, dtype=BF16)]
    out_specs, out_shape = [], []
    if want_f32:
        assert ncol == 1
        out_specs.append(pl.BlockSpec((tm * N_HEADS, HEAD_DIM), lambda n, i: (i, 0)))
        out_shape.append(jax.ShapeDtypeStruct((m * N_HEADS, HEAD_DIM), F32))
    if want_bf16:
        out_specs.append(pl.BlockSpec((tm, tn), lambda n, i: (i, n)))
        out_shape.append(jax.ShapeDtypeStruct((m, ncol * tn), BF16))
    res = pl.pallas_call(
        functools.partial(_proj_kernel, qknorm=qknorm, scale=scale, want_f32=want_f32, want_bf16=want_bf16),
        grid=(ncol, m // tm),
        in_specs=in_specs,
        out_specs=out_specs,
        out_shape=out_shape,
        compiler_params=_cparams(("parallel", "parallel")),
        name="in_proj",
    )(*args)
    return res


def _bucket(rel):
    n = jnp.maximum(rel, 0)
    nf = jnp.maximum(n, 1).astype(F32)
    large = MAX_EXACT + (jnp.log(nf / MAX_EXACT) / math.log(MAX_DISTANCE / MAX_EXACT)
                         * (N_BUCKETS - MAX_EXACT)).astype(jnp.int32)
    large = jnp.minimum(large, N_BUCKETS - 1)
    return jnp.where(n < MAX_EXACT, n, large)


def _bias_prompt_kernel(rb_ref, o_ref, *, t):
    h = pl.program_id(0)
    keys = lax.broadcasted_iota(jnp.int32, (t, t), 0)
    qrys = lax.broadcasted_iota(jnp.int32, (t, t), 1)
    far = rb_ref[(N_BUCKETS - 1) * N_HEADS + h]
    for d in range(2):
        b = _bucket(qrys - keys + d * t)
        acc = jnp.zeros((t, t), F32)
        for k in range(N_BUCKETS - 1):
            acc = jnp.where(b == k, rb_ref[k * N_HEADS + h] - far, acc)
        if d == 0:
            acc = jnp.where(keys <= qrys, acc, NEG_BIG)
        o_ref[0, d] = acc
    o_ref[0, 2] = jnp.zeros((t, t), F32)


def _bias_prompt(rel_bias, t):
    return pl.pallas_call(
        functools.partial(_bias_prompt_kernel, t=t),
        grid=(N_HEADS,),
        in_specs=[pl.BlockSpec(memory_space=pltpu.SMEM)],
        out_specs=pl.BlockSpec((1, 3, t, t), lambda h: (h, 0, 0, 0)),
        out_shape=jax.ShapeDtypeStruct((N_HEADS, 3, t, t), F32),
        compiler_params=_cparams(("parallel",)),
        name="bias_prompt",
    )(rel_bias.reshape(-1))


def _bias_sample_kernel(rb_ref, last_ref, new_ref, *, past, n_new):
    def tile(rows, key0):
        lane = lax.broadcasted_iota(jnp.int32, (rows, QCOLS), 1)
        key = lax.broadcasted_iota(jnp.int32, (rows, QCOLS), 0) + key0
        head = lax.shift_right_logical(lane, (2 * n_new).bit_length() - 1)
        rel = past + (lane & (n_new - 1)) - key
        b = _bucket(rel)
        acc = jnp.zeros((rows, QCOLS), F32)
        for hh in range(N_HEADS):
            far = rb_ref[(N_BUCKETS - 1) * N_HEADS + hh]
            for k in range(N_BUCKETS - 1):
                acc = jnp.where((b == k) & (head == hh), rb_ref[k * N_HEADS + hh] - far, acc)
        return acc
    last_ref[...] = tile(PAGE_SIZE, past - PAGE_SIZE)
    new_ref[...] = tile(n_new, past)


def _bias_sample(rel_bias, past, n_new):
    return pl.pallas_call(
        functools.partial(_bias_sample_kernel, past=past, n_new=n_new),
        in_specs=[pl.BlockSpec(memory_space=pltpu.SMEM)],
        out_specs=[pl.BlockSpec((PAGE_SIZE, QCOLS), lambda: (0, 0)),
                   pl.BlockSpec((n_new, QCOLS), lambda: (0, 0))],
        out_shape=[jax.ShapeDtypeStruct((PAGE_SIZE, QCOLS), F32),
                   jax.ShapeDtypeStruct((n_new, QCOLS), F32)],
        name="bias_sample",
    )(rel_bias.reshape(-1))


def _lambda(lq1, lk1, lq2, lk2, lam_init):
    s1 = jnp.sum(lq1 * lk1, axis=-1, keepdims=True)
    s2 = jnp.sum(lq2 * lk2, axis=-1, keepdims=True)
    return jnp.exp(s1) - jnp.exp(s2) + lam_init


def _store_transposed(dst_ref, src_ref, t, row0=0):
    for ci in range(src_ref.shape[0] // t):
        chunk = src_ref[ci * t:(ci + 1) * t, :].astype(F32)
        dst_ref[row0:row0 + HEAD_DIM, ci * t:(ci + 1) * t] = chunk.T.astype(dst_ref.dtype)


def _head_cols(hh):
    return slice(hh * HEAD_DIM, (hh + 1) * HEAD_DIM)


def _sb_prompt_kernel(q_ref, k_ref, v_ref, upper_ref, o_ref, vt_sc, *, t):
    qi = pl.program_id(2)
    heads = range(SB_HEADS)

    @pl.when(qi == 0)
    def _():
        for hh in heads:
            _store_transposed(vt_sc.at[hh], v_ref.at[:, _head_cols(hh)], t)

    nsub = t // ATT_UNIT
    units = [(hh, cb) for hh in heads for cb in range(nsub)]
    qs = [q_ref[cb * ATT_UNIT:(cb + 1) * ATT_UNIT, _head_cols(hh)] for hh, cb in units]
    keys = lax.broadcasted_iota(jnp.int32, (ATT_UNIT, ATT_UNIT), 0)
    qrys = lax.broadcasted_iota(jnp.int32, (ATT_UNIT, ATT_UNIT), 1)
    tri = keys < qrys

    def block(j, carry, masked):
        start = pl.multiple_of(j * t, t)

        def scores(u, _):
            hh, _cb = units[u]
            return lax.dot_general(k_ref[pl.ds(start, t), _head_cols(hh)], qs[u], (((1,), (1,)), ((), ())),
                                   preferred_element_type=F32)

        def softplus(u, z):
            cb = units[u][1]
            c = carry[u][1]
            subs = []
            for kb in range(nsub - 1, -1, -1):
                if masked and kb > cb:
                    continue
                zk = z[kb * ATT_UNIT:(kb + 1) * ATT_UNIT]
                sp = _softplus2(zk)
                diag = masked and kb == cb
                if diag:
                    sp = jnp.where(tri, sp, 0.0)
                subs.append((kb, zk, sp, c, diag))
                c = c + jnp.sum(sp, axis=0, keepdims=True)
            return subs, c

        def suffixes(u, state):
            subs, c = state
            return [sub + (jnp.dot(upper_ref[...], sub[2].astype(BF16), preferred_element_type=F32),)
                    for sub in subs], c

        def weights(u, state):
            subs, c = state
            hh = units[u][0]
            acc = carry[u][0]
            for kb, zk, sp, c_in, diag, suffix in subs:
                log_a = zk - sp - suffix - c_in
                if diag:
                    log_a = jnp.where(tri, log_a, NEG_BIG)
                a = jnp.exp2(log_a).astype(BF16)
                acc = acc + jnp.dot(vt_sc[hh, :, pl.ds(start + kb * ATT_UNIT, ATT_UNIT)], a,
                                    preferred_element_type=F32)
            return acc, c

        return tuple(_emit_skewed(len(units), [scores, softplus, suffixes, weights]))

    carry = tuple((jnp.zeros((HEAD_DIM, ATT_UNIT), F32), jnp.zeros((1, ATT_UNIT), F32)) for _ in units)
    carry = block(qi, carry, True)

    def spent(cr):
        return functools.reduce(jnp.minimum, [jnp.min(c) for _, c in cr])

    def more(state):
        i, _, lowest = state
        return jnp.logical_and(i < qi, lowest < SB_DONE_BITS)

    def step(state):
        i, cr, _ = state
        cr = block(qi - 1 - i, cr, False)
        return i + 1, cr, spent(cr)

    _, carry, _ = lax.while_loop(more, step, (jnp.int32(0), carry, spent(carry)))
    for u, (hh, cb) in enumerate(units):
        o_ref[cb * ATT_UNIT:(cb + 1) * ATT_UNIT, _head_cols(hh)] = carry[u][0].T.astype(o_ref.dtype)


def _sb_prompt(q, k, v, batch, seq, t):
    nq = seq // t
    w = SB_HEADS * HEAD_DIM
    upper =jnp.asarray(np.triu(np.ones((ATT_UNIT, ATT_UNIT), np.float32), 1), dtype=BF16)
    return pl.pallas_call(
        functools.partial(_sb_prompt_kernel, t=t),
        grid=(batch, N_HEADS // SB_HEADS, nq),
        in_specs=[pl.BlockSpec((t, w), lambda b, h, i: (b * nq + i, h)),
                  pl.BlockSpec((seq, w), lambda b, h, i: (b, h)),
                  pl.BlockSpec((seq, w), lambda b, h, i: (b, h)),
                  pl.BlockSpec((ATT_UNIT, ATT_UNIT), lambda b, h, i: (0, 0))],
        out_specs=pl.BlockSpec((t, w), lambda b, h, i: (b * nq + i, h)),
        out_shape=jax.ShapeDtypeStruct((batch * seq, WIDTH), BF16),
        scratch_shapes=[pltpu.VMEM((SB_HEADS, HEAD_DIM, seq), BF16)],
        compiler_params=_cparams(("parallel", "parallel", "arbitrary")),
        name="sb_prompt",
    )(q, k, v, upper)


def _df_prompt_kernel(q_ref, k_ref, v_ref, bias_ref, lq1, lk1, lq2, lk2, g_ref, o_ref,
                      vt_sc, m_sc, l_sc, acc_sc, z_sc, *, t, lam_init):
    qi = pl.program_id(2)
    heads = range(DF_HEADS)

    @pl.when(qi == 0)
    def _():
        for hh in heads:
            _store_transposed(vt_sc.at[hh], v_ref.at[:, _head_cols(hh)], t)
            vt_sc[hh, HEAD_DIM:, :] = jnp.ones((ONES_ROWS, vt_sc.shape[2]), BF16)

    lane = lax.broadcasted_iota(jnp.int32, (t, HEAD_DIM), 1)
    qs = []
    for hh in heads:
        q = q_ref[:, _head_cols(hh)]
        zero = jnp.zeros_like(q)
        qs.append(jnp.concatenate([jnp.where(lane < DF_HALF, q, zero), jnp.where(lane >= DF_HALF, q, zero)], axis=0))
        m_sc[hh] = jnp.full(m_sc.shape[1:], NEG_BIG, F32)
        l_sc[hh] = jnp.zeros(l_sc.shape[1:], F32)
        acc_sc[hh] = jnp.zeros(acc_sc.shape[1:], F32)

    def scores(j, bias_d, slot):
        start = pl.multiple_of(j * t, t)
        for hh in heads:
            z = lax.dot_general(k_ref[pl.ds(start, t), _head_cols(hh)], qs[hh], (((1,), (1,)), ((), ())),
                                preferred_element_type=F32)
            bias = bias_ref[hh, bias_d]
            z_sc[slot, hh] = z + jnp.concatenate([bias, bias], axis=1)

    def consume(j, slot):
        for hh in heads:
            consume_head(j, z_sc[slot, hh], hh)

    def consume_head(j, z, hh):
        start = pl.multiple_of(j * t, t)
        m_old = m_sc[hh]
        m_new = jnp.maximum(m_old, jnp.max(z, axis=0, keepdims=True))
        p = jnp.exp(z - m_new).astype(BF16)
        alpha = jnp.exp(m_old - m_new)
        pv = jnp.dot(vt_sc[hh, :, pl.ds(start, t)], p, preferred_element_type=F32)
        l_sc[hh] = alpha * l_sc[hh] + pv[HEAD_DIM:HEAD_DIM + 1, :]
        acc_sc[hh] = alpha * acc_sc[hh] + pv[:HEAD_DIM, :]
        m_sc[hh] = m_new

    scores(qi, 0, 0)

    def pair(p, carry):
        j = qi - 2 * p
        scores(j - 1, jnp.minimum(2 * p + 1, 2), 1)
        consume(j, 0)
        scores(j - 2, 2, 0)
        consume(j - 1, 1)
        return carry
    lax.fori_loop(0, qi // 2, pair, 0)

    @pl.when(qi % 2 == 1)
    def _():
        scores(0, jnp.minimum(qi, 2), 1)
        consume(1, 0)
        consume(0, 1)

    @pl.when(qi % 2 == 0)
    def _():
        consume(0, 0)

    lam = _lambda(lq1[...], lk1[...], lq2[...], lk2[...], lam_init)
    for hh in heads:
        o = acc_sc[hh] / l_sc[hh]
        o = (o[:, :t] - lam * o[:, t:]).T
        ms = jnp.mean(o * o, axis=-1, keepdims=True)
        o_ref[:, _head_cols(hh)] = (o * lax.rsqrt(ms + EPS) * g_ref[...] * (1.0 - lam_init)).astype(o_ref.dtype)


def _df_prompt(q, k, v, bias, lams, subln, lam_init, batch, seq, t):
    nq = seq // t
    w = DF_HEADS * HEAD_DIM
    vec64 = pl.BlockSpec((1, DF_HALF), lambda b, h, i: (0, 0))
    return pl.pallas_call(
        functools.partial(_df_prompt_kernel, t=t, lam_init=lam_init),
        grid=(batch, N_HEADS // DF_HEADS, nq),
        in_specs=[pl.BlockSpec((t, w), lambda b, h, i: (b * nq + i, h)),
                  pl.BlockSpec((seq, w), lambda b, h, i: (b, h)),
                  pl.BlockSpec((seq, w), lambda b, h, i: (b, h)),
                  pl.BlockSpec((DF_HEADS, 3, t, t), lambda b, h, i: (h, 0, 0, 0)),
                  vec64, vec64, vec64, vec64,
                  pl.BlockSpec((1, HEAD_DIM), lambda b, h, i: (0, 0))],
        out_specs=pl.BlockSpec((t, w), lambda b, h, i: (b * nq + i, h)),
        out_shape=jax.ShapeDtypeStruct((batch * seq, WIDTH), BF16),
        scratch_shapes=[pltpu.VMEM((DF_HEADS, HEAD_DIM + ONES_ROWS, seq), BF16),
                        pltpu.VMEM((DF_HEADS, 1, 2 * t), F32), pltpu.VMEM((DF_HEADS, 1, 2 * t), F32),
                        pltpu.VMEM((DF_HEADS, HEAD_DIM, 2 * t), F32),
                        pltpu.VMEM((2, DF_HEADS, t, 2 * t), F32)],
        compiler_params=_cparams(("parallel", "parallel", "arbitrary")),
        name="df_prompt",
    )(q, k, v, bias, *lams, subln)


def _score_matrix(q_ref, seq_in_block, n_new, halves):
    row = lax.broadcasted_iota(jnp.int32, (QROWS, QCOLS), 0)
    lane = lax.broadcasted_iota(jnp.int32, (QROWS, QCOLS), 1)
    pick = (lane & (n_new - 1)) + seq_in_block * n_new == row
    if halves == 1:
        pick = pick & ((lane & n_new) == 0)
    spread = lax.dot_general(q_ref[...], jnp.where(pick, 1.0, 0.0).astype(BF16), (((0,), (0,)), ((), ())),
                             preferred_element_type=F32)
    feat = lax.broadcasted_iota(jnp.int32, (WIDTH, QCOLS), 0)
    col = lax.broadcasted_iota(jnp.int32, (WIDTH, QCOLS), 1)
    log2 = lambda v: v.bit_length() - 1
    keep = lax.shift_right_logical(feat, log2(HEAD_DIM)) == lax.shift_right_logical(col, log2(2 * n_new))
    if halves == 2:
        keep = keep & ((lax.shift_right_logical(feat, log2(DF_HALF)) & 1) == (lax.shift_right_logical(col, log2(n_new)) & 1))
    return jnp.where(keep, spread, 0.0).astype(BF16)


def _sample_attn_kernel(pt_ref, qa_ref, qb_ref, kan_ref, van_ref, kbn_ref, vbn_ref,
                        blast_ref, bnew_ref, lq1, lk1, lq2, lk2, g_ref, *rest,
                        n_seq, n_pages, n_new, layer, lam_init):
    g = PAGES_PER_STEP
    c_sbk, c_dfk, c_sbv, c_dfv, oa_ref, ob_ref = rest[:6]
    zdf_sc, psb_sc, pdf_sc, accsb_sc, accdf_sc, qsb_sc, qdf_sc, ring, extra_buf, sems, extra_sem, extra_n = rest[6:]
    b = pl.program_id(0)
    s = pl.program_id(1)
    n_kv = n_pages // g
    n_steps = 2 * n_kv
    past = n_pages * PAGE_SIZE
    chunk = g * PAGE_SIZE
    hot = SB_HOT_PAGES * PAGE_SIZE
    group = g + SB_HOT_PAGES

    def page_copy(cache, page, block, slot):
        return pltpu.make_async_copy(cache.at[layer, page], ring.at[block], sems.at[slot])

    def group_copies(seq_i, step, slot):
        k = jnp.where(step < n_kv, step, step - n_kv)
        first = (n_kv - 1 - k) * g
        df = [(seq_i * n_pages + first + i, slot * group + i) for i in range(g)]
        sb = [(seq_i * n_pages + n_pages - SB_HOT_PAGES + i, slot * group + g + i) for i in range(SB_HOT_PAGES)]
        return df, sb, k == 0

    def start_group(seq_i, step, slot):
        df, sb, rides = group_copies(seq_i, step, slot)
        for in_phase, c_df, c_sb in ((step < n_kv, c_dfk, c_sbk), (step >= n_kv, c_dfv, c_sbv)):
            @pl.when(in_phase)
            def _():
                for idx, block in df:
                    page_copy(c_df, pt_ref[idx], block, slot).start()

            @pl.when(jnp.logical_and(in_phase, rides))
            def _():
                for idx, block in sb:
                    page_copy(c_sb, pt_ref[idx], block, slot).start()

    gidx = b * n_steps + s
    slot = lax.rem(gidx, RING_DEPTH)
    k_step = jnp.where(s < n_kv, s, s - n_kv)
    first_page = (n_kv - 1 - k_step) * g

    @pl.when(gidx == 0)
    def _():
        for d in range(RING_DEPTH - 1):
            start_group(jnp.int32(0), jnp.int32(d), jnp.int32(d))

    ahead = s + (RING_DEPTH - 1)
    wraps = ahead >= n_steps
    seq_ahead = jnp.where(wraps, b + 1, b)

    @pl.when(seq_ahead < n_seq)
    def _():
        start_group(seq_ahead, jnp.where(wraps, ahead - n_steps, ahead), lax.rem(gidx + (RING_DEPTH - 1), RING_DEPTH))

    for i in range(g):
        page_copy(c_dfk, 0, slot * group + i, slot).wait()

    @pl.when(k_step == 0)
    def _():
        for i in range(SB_HOT_PAGES):
            page_copy(c_sbk, 0, slot * group + g + i, slot).wait()

    def page_rows_bf16(ref):
        return _tokens_by_width(ref, PAGE_SIZE).astype(BF16)

    def pages_bf16(first, count):
        return jnp.concatenate([page_rows_bf16(ring.at[slot * group + first + i]) for i in range(count)], axis=0)

    def extra_page(cache, page):
        cp = pltpu.make_async_copy(cache.at[layer, pt_ref[b * n_pages + page]], extra_buf, extra_sem.at[0])
        cp.start()
        cp.wait()
        return page_rows_bf16(extra_buf)

    def new_tokens(ref):
        return _tokens_by_width(ref, n_new)

    def _queries():
        seq_in_block = lax.rem(b, QROWS // n_new)
        qsb_sc[...] = _score_matrix(qa_ref, seq_in_block, n_new, 1)
        qdf_sc[...] = _score_matrix(qb_ref, seq_in_block, n_new, 2)

    def _scores():
        start = pl.multiple_of(first_page * PAGE_SIZE, chunk)
        zdf_sc[pl.ds(start, chunk), :] = jnp.dot(pages_bf16(0, g), qdf_sc[...], preferred_element_type=F32)

    def pad_rows_bf16(x):
        return jnp.concatenate([x, jnp.zeros_like(x)], axis=0).astype(BF16)

    lane_n = lax.broadcasted_iota(jnp.int32, (n_new, QCOLS), 1)
    key_n = lax.broadcasted_iota(jnp.int32, (n_new, QCOLS), 0)
    qpos_n = lane_n & (n_new - 1)

    def sb_weights(z, carry, upper):
        sp = _softplus2(z)
        suffix = jnp.dot(upper, sp.astype(BF16), preferred_element_type=F32)
        return jnp.exp2(z - sp - suffix - carry).astype(BF16), carry + jnp.sum(sp, axis=0, keepdims=True)

    def unspent(carry):
        real = (lax.broadcasted_iota(jnp.int32, (1, QCOLS), 1) & n_new) == 0
        return jnp.min(jnp.where(real, carry, SB_DONE_BITS))

    def _sb_weights():
        zs_new = jnp.dot(pad_rows_bf16(new_tokens(kan_ref)), qsb_sc[...], preferred_element_type=F32)[:n_new]
        strict = key_n < qpos_n
        sp_new = jnp.where(strict, _softplus2(zs_new), 0.0)
        carry = jnp.zeros((1, QCOLS), F32)
        suffix_rows = [None] * n_new
        for i in range(n_new - 1, -1, -1):
            suffix_rows[i] = carry
            carry = carry + sp_new[i:i + 1]
        suffix_new = jnp.concatenate(suffix_rows, axis=0)
        a_new = jnp.exp2(jnp.where(strict, zs_new - sp_new - suffix_new, NEG_BIG))
        accsb_sc[...] = lax.dot_general(pad_rows_bf16(a_new), pad_rows_bf16(new_tokens(van_ref)),
                                        (((0,), (0,)), ((), ())), preferred_element_type=F32)
        rr = lax.broadcasted_iota(jnp.int32, (hot, hot), 0)
        cc = lax.broadcasted_iota(jnp.int32, (hot, hot), 1)
        upper = jnp.where(cc > rr, 1.0, 0.0).astype(BF16)
        z_hot = jnp.dot(pages_bf16(g, SB_HOT_PAGES), qsb_sc[...], preferred_element_type=F32)
        a_hot, carry = sb_weights(z_hot, carry, upper)
        psb_sc[past - hot:past, :] = a_hot

        def more(state):
            page, _, lowest = state
            return jnp.logical_and(page >= 0, lowest < SB_DONE_BITS)

        def older(state):
            page, carry, _ = state
            z = jnp.dot(extra_page(c_sbk, page), qsb_sc[...], preferred_element_type=F32)
            a, carry = sb_weights(z, carry, upper[:PAGE_SIZE, :PAGE_SIZE])
            psb_sc[pl.ds(pl.multiple_of(page * PAGE_SIZE, PAGE_SIZE), PAGE_SIZE), :] = a
            return page - 1, carry, unspent(carry)

        oldest_hot = n_pages - SB_HOT_PAGES
        page, _, _ = lax.while_loop(more, older, (jnp.int32(oldest_hot - 1), carry, unspent(carry)))
        extra_n[0] = oldest_hot - 1 - page

    def _df_weights():
        z_new = jnp.dot(pad_rows_bf16(new_tokens(kbn_ref)), qdf_sc[...], preferred_element_type=F32)[:n_new]
        z_new = z_new + bnew_ref[...]
        z_new = jnp.where(key_n <= qpos_n, z_new, NEG_BIG)
        last0 = past - PAGE_SIZE
        zdf_sc[last0:past, :] = zdf_sc[last0:past, :] + blast_ref[...]
        zp = zdf_sc[...]
        mx = jnp.maximum(jnp.max(zp, axis=0, keepdims=True), jnp.max(z_new, axis=0, keepdims=True))
        e_new = jnp.exp(z_new - mx)
        ep = jnp.exp(zp - mx)
        inv = 1.0 / (jnp.sum(ep, axis=0, keepdims=True) + jnp.sum(e_new, axis=0, keepdims=True))
        pdf_sc[...] = (ep * inv).astype(BF16)
        accdf_sc[...] = lax.dot_general(pad_rows_bf16(e_new * inv), pad_rows_bf16(new_tokens(vbn_ref)),
                                        (((0,), (0,)), ((), ())), preferred_element_type=F32)

    def _values():
        start = pl.multiple_of(first_page * PAGE_SIZE, chunk)
        accdf_sc[...] += lax.dot_general(pdf_sc[pl.ds(start, chunk), :], pages_bf16(0, g),
                                         (((0,), (0,)), ((), ())), preferred_element_type=F32)

    def _sb_values():
        accsb_sc[...] += lax.dot_general(psb_sc[past - hot:past, :], pages_bf16(g, SB_HOT_PAGES),
                                         (((0,), (0,)), ((), ())), preferred_element_type=F32)

        def older(i, carry):
            page = n_pages - SB_HOT_PAGES - 1 - i
            rows = pl.ds(pl.multiple_of(page * PAGE_SIZE, PAGE_SIZE), PAGE_SIZE)
            accsb_sc[...] += lax.dot_general(psb_sc[rows, :], extra_page(c_sbv, page),
                                             (((0,), (0,)), ((), ())), preferred_element_type=F32)
            return carry
        lax.fori_loop(0, extra_n[0], older, 0)

    def _finish():
        lam = _lambda(lq1[...], lk1[...], lq2[...], lk2[...], lam_init)
        for h in range(N_HEADS):
            cols = slice(h * HEAD_DIM, (h + 1) * HEAD_DIM)
            oa_ref[:, cols] = accsb_sc[h * 2 * n_new:h * 2 * n_new + n_new, cols]
            o1 = accdf_sc[h * 2 * n_new:h * 2 * n_new + n_new, cols]
            o2 = accdf_sc[h * 2 * n_new + n_new:(h + 1) * 2 * n_new, cols]
            o = o1 - lam * o2
            ms = jnp.mean(o * o, axis=-1, keepdims=True)
            ob_ref[:, cols] = o * lax.rsqrt(ms + EPS) * g_ref[...] * (1.0 - lam_init)

    for k in range(n_kv):
        @pl.when(s == k)
        def _():
            if k == 0:
                _queries()
            _scores()
            if k == 0:
                _sb_weights()
            if k == n_kv - 1:
                _df_weights()

        @pl.when(s == n_kv + k)
        def _():
            _values()
            if k == 0:
                _sb_values()
            if k == n_kv - 1:
                _finish()


def _sample_attn(page_table, qsb, qdf, ka_n, va_n, kb_n, vb_n, blast, bnew, lams, subln,
                 c_sbk, c_sbv, c_dfk, c_dfv, layer, lam_init):
    n_seq, n_pages = page_table.shape
    n_new = ka_n.shape[0] // (n_seq * N_HEADS)
    assert 2 * N_HEADS * n_new == QCOLS
    g = PAGES_PER_STEP
    n_kv = n_pages // g
    past = n_pages * PAGE_SIZE

    assert n_pages % g == 0 and RING_DEPTH - 1 <= 2 * n_kv and SB_HOT_PAGES <= n_pages
    vec64 = pl.BlockSpec((1, DF_HALF), lambda b, s, pt: (0, 0))
    new_spec = pl.BlockSpec((n_new * N_HEADS, HEAD_DIM), lambda b, s, pt: (b, 0))
    out_spec = pl.BlockSpec((n_new, WIDTH), lambda b, s, pt: (b, 0))
    q_spec = pl.BlockSpec((QROWS, WIDTH), lambda b, s, pt: (b // (QROWS // n_new), 0))
    in_specs = [q_spec, q_spec,
                new_spec, new_spec, new_spec, new_spec,
                pl.BlockSpec((PAGE_SIZE, QCOLS), lambda b, s, pt: (0, 0)),
                pl.BlockSpec((n_new, QCOLS), lambda b, s, pt: (0, 0)),
                vec64, vec64, vec64, vec64,
                pl.BlockSpec((1, HEAD_DIM), lambda b, s, pt: (0, 0))]
    in_specs += [pl.BlockSpec(memory_space=pl.ANY)] * 4
    grid_spec = pltpu.PrefetchScalarGridSpec(
        num_scalar_prefetch=1,
        grid=(n_seq, 2 * n_kv),
        in_specs=in_specs,
        out_specs=[out_spec, out_spec],
        scratch_shapes=[pltpu.VMEM((past, QCOLS), F32),
                        pltpu.VMEM((past, QCOLS), BF16), pltpu.VMEM((past, QCOLS), BF16),
                        pltpu.VMEM((QCOLS, WIDTH), F32), pltpu.VMEM((QCOLS, WIDTH), F32),
                        pltpu.VMEM((WIDTH, QCOLS), BF16), pltpu.VMEM((WIDTH, QCOLS), BF16),
                        pltpu.VMEM((RING_DEPTH * (g + SB_HOT_PAGES), PAGE_SIZE * N_HEADS, HEAD_DIM), F32),
                        pltpu.VMEM((PAGE_SIZE * N_HEADS, HEAD_DIM), F32),
                        pltpu.SemaphoreType.DMA((RING_DEPTH,)), pltpu.SemaphoreType.DMA((1,)),
                        pltpu.SMEM((1,), jnp.int32)])
    return pl.pallas_call(
        functools.partial(_sample_attn_kernel, n_seq=n_seq, n_pages=n_pages, n_new=n_new, layer=layer,
                          lam_init=lam_init),
        grid_spec=grid_spec,
        out_shape=[jax.ShapeDtypeStruct((n_seq * n_new, WIDTH), F32)] * 2,
        compiler_params=_cparams(("arbitrary", "arbitrary")),
        name="sample_attn",
    )(page_table.reshape(-1), qsb, qdf, ka_n, va_n, kb_n, vb_n, blast, bnew, *lams, subln,
      c_sbk, c_dfk, c_sbv, c_dfv)


def _merge_kernel(x_ref, oa_ref, ob_ref, ga_ref, gb_ref, wsb_ref, wdf_ref, wout_ref, g2_ref, x1_ref, hn_ref):
    ya = jnp.dot(oa_ref[...].astype(BF16), wsb_ref[...], preferred_element_type=F32)
    yb = jnp.dot(ob_ref[...].astype(BF16), wdf_ref[...], preferred_element_type=F32)
    m = jax.nn.sigmoid(ga_ref[...].astype(F32)) * ya + jax.nn.sigmoid(gb_ref[...].astype(F32)) * yb
    x1 = x_ref[...] + jnp.dot(m.astype(BF16), wout_ref[...], preferred_element_type=F32)
    x1_ref[...] = x1
    ms = jnp.mean(x1 * x1, axis=-1, keepdims=True)
    hn_ref[...] = (x1 * lax.rsqrt(ms + EPS) * g2_ref[...]).astype(BF16)


def _merge(x, oa, ob, gates, wsb, wdf, wout, norm2, tm):
    m, d = x.shape
    const = lambda shape: pl.BlockSpec(shape, lambda i: (0, 0), pipeline_mode=pl.Buffered(1))
    return pl.pallas_call(
        _merge_kernel,
        grid=(m // tm,),
        in_specs=[pl.BlockSpec((tm, d), lambda i: (i, 0)),
                  pl.BlockSpec((tm, WIDTH), lambda i: (i, 0)),
                  pl.BlockSpec((tm, WIDTH), lambda i: (i, 0)),
                  pl.BlockSpec((tm, d), lambda i: (i, 0)),
                  pl.BlockSpec((tm, d), lambda i: (i, 1)),
                  const((WIDTH, d)), const((WIDTH, d)), const((d, d)),
                  pl.BlockSpec((1, d), lambda i: (0, 0))],
        out_specs=[pl.BlockSpec((tm, d), lambda i: (i, 0)), pl.BlockSpec((tm, d), lambda i: (i, 0))],
        out_shape=[jax.ShapeDtypeStruct((m, d), F32), jax.ShapeDtypeStruct((m, d), BF16)],
        compiler_params=_cparams(("parallel",)),
        name="merge",
    )(x, oa, ob, gates, gates, wsb, wdf, wout, norm2.reshape(1, d))


def _mlp_kernel(x1_ref, hn_ref, wup_ref, wdn_ref, o_ref):
    f = pl.program_id(1)

    @pl.when(f == 0)
    def _():
        o_ref[...] = x1_ref[...]

    u = jnp.maximum(jnp.dot(hn_ref[...], wup_ref[...], preferred_element_type=F32), 0.0)
    o_ref[...] += jnp.dot((u * u).astype(BF16), wdn_ref[...], preferred_element_type=F32)


def _mlp(x1, hn, wup, wdn, tm, tf):
    m, d = x1.shape
    dff = wup.shape[1]
    return pl.pallas_call(
        _mlp_kernel,
        grid=(m // tm, dff // tf),
        in_specs=[pl.BlockSpec((tm, d), lambda i, f: (i, 0)),
                  pl.BlockSpec((tm, d), lambda i, f: (i, 0)),
                  pl.BlockSpec((d, tf), lambda i, f: (0, f)),
                  pl.BlockSpec((tf, d), lambda i, f: (f, 0))],
        out_specs=pl.BlockSpec((tm, d), lambda i, f: (i, 0)),
        out_shape=jax.ShapeDtypeStruct((m, d), F32),
        compiler_params=_cparams(("parallel", "arbitrary")),
        name="mlp",
    )(x1, hn, wup, wdn)


def _project_group(x, norm1, w_in, q_norm, k_norm, tm):
    h = _rmsnorm_bf16(x, norm1, min(tm, 512))
    (qa,) = _proj(h, w_in, COL_QA, 1, tm=tm, scale=SB_SCALE * LOG2E)
    ka, ka16 = _proj(h, w_in, COL_KA, 1, tm=tm, want_f32=True)
    va, va16 = _proj(h, w_in, COL_VA, 1, tm=tm, want_f32=True)
    (qb,) = _proj(h, w_in, COL_QB, 1, tm=tm, qknorm=True, gain=q_norm, scale=DF_SCALE)
    kb, kb16 = _proj(h, w_in, COL_KB, 1, tm=tm, qknorm=True, gain=k_norm, want_f32=True)
    vb, vb16 = _proj(h, w_in, COL_VB, 1, tm=tm, want_f32=True)
    (gates,) = _proj(h, w_in, COL_GA, 4, tm=tm)
    return dict(qa=qa, ka=ka, ka16=ka16, va=va, va16=va16, qb=qb, kb=kb, kb16=kb16, vb=vb, vb16=vb16, gates=gates)


def kernel(x_prompt, x_sample, cache_sb_k, cache_sb_v, cache_df_k, cache_df_v, page_table, rel_bias, norm1, w_in, q_norm, k_norm, lambda_q1, lambda_k1, lambda_q2, lambda_k2, subln, w_branch_sb, w_branch_df, w_out, norm2, w_up, w_down):
    depth = norm1.shape[0]
    batch, seq, d = x_prompt.shape
    n_seq, n_new, _ = x_sample.shape
    n_pages = page_table.shape[1]
    past = n_pages * PAGE_SIZE
    xp = x_prompt.reshape(batch * seq, d)
    xs = x_sample.reshape(n_seq * n_new, d)
    bias_p = _bias_prompt(rel_bias, ATT_T)
    bias_last, bias_new = _bias_sample(rel_bias, past, n_new)
    n_pool = cache_sb_k.shape[1]
    leaves = [[] for _ in range(8)]
    for l in range(depth):
        lam_init = 0.8 - 0.6 * math.exp(-0.3 * l)
        w_in16 = w_in[l].astype(BF16)
        wsb16 = w_branch_sb[l].astype(BF16)
        wdf16 = w_branch_df[l].astype(BF16)
        wout16 = w_out[l].astype(BF16)
        wup16 = w_up[l].astype(BF16)
        wdn16 = w_down[l].astype(BF16)
        lams = [v[l].reshape(1, DF_HALF).astype(F32) for v in (lambda_q1, lambda_k1, lambda_q2, lambda_k2)]
        sub = subln[l].reshape(1, HEAD_DIM).astype(F32)

        p = _project_group(xp, norm1[l], w_in16, q_norm[l], k_norm[l], 1024)
        oa = _sb_prompt(p["qa"], p["ka16"], p["va16"], batch, seq, ATT_T)
        ob = _df_prompt(p["qb"], p["kb16"], p["vb16"], bias_p, lams, sub, lam_init, batch, seq, ATT_T)
        x1, hn = _merge(xp, oa, ob, p["gates"], wsb16, wdf16, wout16, norm2[l], 256)
        xp = _mlp(x1, hn, wup16, wdn16, 512, 1024)
        for i, name in enumerate(("ka", "va", "kb", "vb")):
            leaves[i].append(p[name].reshape(batch, seq, N_HEADS, HEAD_DIM))

        s = _project_group(xs, norm1[l], w_in16, q_norm[l], k_norm[l], 1024)
        shape4 = (depth, n_pool, PAGE_SIZE * N_HEADS, HEAD_DIM)
        oa, ob = _sample_attn(page_table, s["qa"], s["qb"], s["ka"], s["va"], s["kb"], s["vb"], bias_last, bias_new,
                              lams, sub, cache_sb_k.reshape(shape4), cache_sb_v.reshape(shape4),
                              cache_df_k.reshape(shape4), cache_df_v.reshape(shape4), l, lam_init)
        x1, hn = _merge(xs, oa, ob, s["gates"], wsb16, wdf16, wout16, norm2[l], 256)
        xs = _mlp(x1, hn, wup16, wdn16, 512, 1024)
        for i, name in enumerate(("ka", "va", "kb", "vb")):
            leaves[4 + i].append(s[name].reshape(n_seq, n_new, N_HEADS, HEAD_DIM))

    return (xp.reshape(batch, seq, d), xs.reshape(n_seq, n_new, d)) + tuple(jnp.stack(v) for v in leaves)
```

```python
import functools
import math

import numpy as np
import jax
import jax.numpy as jnp
from jax import lax
from jax.experimental import pallas as pl
from jax.experimental.pallas import tpu as pltpu

F32 = jnp.float32
BF16 = jnp.bfloat16

D_MODEL = 2048
N_HEADS = 8
HEAD_DIM = 128
DF_HALF = 64
WIDTH = N_HEADS * HEAD_DIM
SB_SCALE = 1.0 / math.sqrt(HEAD_DIM)
DF_SCALE = 1.0 / math.sqrt(DF_HALF)
D_FF = 4 * D_MODEL
N_BUCKETS = 32
MAX_EXACT = N_BUCKETS // 2
MAX_DISTANCE = 128
EPS = 1e-6
PAGE_SIZE = 128
NEG_BIG = -1e30

COL_QA, COL_KA, COL_VA, COL_QB, COL_KB, COL_VB, COL_GA, COL_GB = 0, 1, 2, 3, 4, 5, 6, 8

ATT_T = 512
SB_HEADS = 4
DF_HEADS = 2
ATT_UNIT = 256
SB_DONE_BITS = 160.0
ONES_ROWS = 16
LOG2E = 1.4426950408889634
PAGES_PER_STEP = 16
SB_HOT_PAGES = 2
RING_DEPTH = 3
QCOLS = 128
QROWS = 16

VMEM_LIMIT = 56 * 1024 * 1024


def _cparams(sem):
    return pltpu.CompilerParams(dimension_semantics=sem, vmem_limit_bytes=VMEM_LIMIT)


def _tokens_by_width(ref, n_tok):
    return jnp.concatenate([ref[pl.ds(h, n_tok, stride=N_HEADS), :] for h in range(N_HEADS)], axis=1)


def _softplus2(z):
    return jnp.maximum(z, 0.0) + jnp.log2(1.0 + jnp.exp2(-jnp.abs(z)))


def _emit_skewed(n_units, stages):
    state = [None] * n_units
    for tick in range(n_units + len(stages) - 1):
        for si in range(len(stages) - 1, -1, -1):
            u = tick - si
            if 0 <= u < n_units:
                state[u] = stages[si](u, state[u])
    return state


def _rmsnorm_kernel(x_ref, g_ref, o_ref):
    x = x_ref[...]
    ms = jnp.mean(x * x, axis=-1, keepdims=True)
    o_ref[...] = (x * lax.rsqrt(ms + EPS) * g_ref[...]).astype(o_ref.dtype)


def _rmsnorm_bf16(x, g, tm):
    m, d = x.shape
    return pl.pallas_call(
        _rmsnorm_kernel,
        grid=(m // tm,),
        in_specs=[pl.BlockSpec((tm, d), lambda i: (i, 0)),
                  pl.BlockSpec((1, d), lambda i: (0, 0))],
        out_specs=pl.BlockSpec((tm, d), lambda i: (i, 0)),
        out_shape=jax.ShapeDtypeStruct((m, d), BF16),
        compiler_params=_cparams(("parallel",)),
        name="rmsnorm_bf16",
    )(x, g.reshape(1, d))


def _proj_kernel(*refs, qknorm, scale, want_f32, want_bf16):
    h_ref, w_ref = refs[0], refs[1]
    pos = 2
    if qknorm:
        gain_ref, pmat_ref = refs[2], refs[3]
        pos = 4
    outs = refs[pos:]
    y = jnp.dot(h_ref[...], w_ref[...], preferred_element_type=F32)
    if qknorm:
        tn = y.shape[1]
        pieces = []
        for c in range(tn // 256):
            yb = y[:, c * 256:(c + 1) * 256]
            ms = jnp.dot((yb * yb).astype(BF16), pmat_ref[...], preferred_element_type=F32)
            pieces.append(yb * lax.rsqrt(ms + EPS))
        y = jnp.concatenate(pieces, axis=1) * gain_ref[...]
    k = 0
    if want_f32:
        for hd in range(N_HEADS):
            outs[k][pl.ds(hd, y.shape[0], stride=N_HEADS), :] = y[:, hd * HEAD_DIM:(hd + 1) * HEAD_DIM]
        k += 1
    if want_bf16:
        outs[k][...] = (y * scale).astype(BF16) if scale != 1.0 else y.astype(BF16)


def _proj(h, w, col0, ncol, *, tm, qknorm=False, gain=None, scale=1.0, want_f32=False, want_bf16=True):
    m, kdim = h.shape
    tn = WIDTH
    in_specs = [pl.BlockSpec((tm, kdim), lambda n, i: (i, 0)),
                pl.BlockSpec((kdim, tn), lambda n, i: (0, col0 + n))]
    args = [h, w]
    if qknorm:
        pmat = np.kron(np.eye(256 // DF_HALF), np.full((DF_HALF, DF_HALF), 1.0 / DF_HALF)).astype(np.float32)
        in_specs += [pl.BlockSpec((1, tn), lambda n, i: (0, 0)),
                     pl.BlockSpec((256, 256), lambda n, i: (0, 0))]
        args += [jnp.tile(gain.astype(F32), tn // DF_HALF).reshape(1, tn), jnp.asarray(pmat, dtype=BF16)]
    out_specs, out_shape = [], []
    if want_f32:
        assert ncol == 1
        out_specs.append(pl.BlockSpec((tm * N_HEADS, HEAD_DIM), lambda n, i: (i, 0)))
        out_shape.append(jax.ShapeDtypeStruct((m * N_HEADS, HEAD_DIM), F32))
    if want_bf16:
        out_specs.append(pl.BlockSpec((tm, tn), lambda n, i: (i, n)))
        out_shape.append(jax.ShapeDtypeStruct((m, ncol * tn), BF16))
    res = pl.pallas_call(
        functools.partial(_proj_kernel, qknorm=qknorm, scale=scale, want_f32=want_f32, want_bf16=want_bf16),
        grid=(ncol, m // tm),
        in_specs=in_specs,
        out_specs=out_specs,
        out_shape=out_shape,
        compiler_params=_cparams(("parallel", "parallel")),
        name="in_proj",
    )(*args)
    return res


def _bucket(rel):
    n = jnp.maximum(rel, 0)
    nf = jnp.maximum(n, 1).astype(F32)
    large = MAX_EXACT + (jnp.log(nf / MAX_EXACT) / math.log(MAX_DISTANCE / MAX_EXACT)
                         * (N_BUCKETS - MAX_EXACT)).astype(jnp.int32)
    large = jnp.minimum(large, N_BUCKETS - 1)
    return jnp.where(n < MAX_EXACT, n, large)


def _bias_prompt_kernel(rb_ref, o_ref, *, t):
    h = pl.program_id(0)
    far = rb_ref[(N_BUCKETS - 1) * N_HEADS + h]

    def shifted_bias(rel):
        b = _bucket(rel)
        acc = jnp.zeros(rel.shape, F32)
        for k in range(N_BUCKETS - 1):
            acc = jnp.where(b == k, rb_ref[k * N_HEADS + h] - far, acc)
        return acc

    keys = lax.broadcasted_iota(jnp.int32, (t, t), 0)
    qrys = lax.broadcasted_iota(jnp.int32, (t, t), 1)
    o_ref[0, 0] = jnp.where(keys <= qrys, shifted_bias(qrys - keys), NEG_BIG)
    c = MAX_DISTANCE
    ki = lax.broadcasted_iota(jnp.int32, (c, c), 0)
    qj = lax.broadcasted_iota(jnp.int32, (c, c), 1)
    corner = jnp.concatenate([shifted_bias(qj - ki + c), jnp.zeros((c, t - c), F32)], axis=1)
    o_ref[0, 1] = jnp.concatenate([jnp.zeros((t - c, t), F32), corner], axis=0)


def _bias_prompt(rel_bias, t):
    return pl.pallas_call(
        functools.partial(_bias_prompt_kernel, t=t),
        grid=(N_HEADS,),
        in_specs=[pl.BlockSpec(memory_space=pltpu.SMEM)],
        out_specs=pl.BlockSpec((1, 2, t, t), lambda h: (h, 0, 0, 0)),
        out_shape=jax.ShapeDtypeStruct((N_HEADS, 2, t, t), F32),
        compiler_params=_cparams(("parallel",)),
        name="bias_prompt",
    )(rel_bias.reshape(-1))


def _bias_sample_kernel(rb_ref, last_ref, new_ref, *, past, n_new):
    def tile(rows, key0):
        lane = lax.broadcasted_iota(jnp.int32, (rows, QCOLS), 1)
        key = lax.broadcasted_iota(jnp.int32, (rows, QCOLS), 0) + key0
        head = lax.shift_right_logical(lane, (2 * n_new).bit_length() - 1)
        rel = past + (lane & (n_new - 1)) - key
        b = _bucket(rel)
        acc = jnp.zeros((rows, QCOLS), F32)
        for hh in range(N_HEADS):
            far = rb_ref[(N_BUCKETS - 1) * N_HEADS + hh]
            for k in range(N_BUCKETS - 1):
                acc = jnp.where((b == k) & (head == hh), rb_ref[k * N_HEADS + hh] - far, acc)
        return acc
    last_ref[...] = tile(PAGE_SIZE, past - PAGE_SIZE)
    new_ref[...] = tile(n_new, past)


def _bias_sample(rel_bias, past, n_new):
    return pl.pallas_call(
        functools.partial(_bias_sample_kernel, past=past, n_new=n_new),
        in_specs=[pl.BlockSpec(memory_space=pltpu.SMEM)],
        out_specs=[pl.BlockSpec((PAGE_SIZE, QCOLS), lambda: (0, 0)),
                   pl.BlockSpec((n_new, QCOLS), lambda: (0, 0))],
        out_shape=[jax.ShapeDtypeStruct((PAGE_SIZE, QCOLS), F32),
                   jax.ShapeDtypeStruct((n_new, QCOLS), F32)],
        name="bias_sample",
    )(rel_bias.reshape(-1))


def _lambda(lq1, lk1, lq2, lk2, lam_init):
    s1 = jnp.sum(lq1 * lk1, axis=-1, keepdims=True)
    s2 = jnp.sum(lq2 * lk2, axis=-1, keepdims=True)
    return jnp.exp(s1) - jnp.exp(s2) + lam_init


def _store_transposed(dst_ref, src_ref, t, row0=0):
    for ci in range(src_ref.shape[0] // t):
        chunk = src_ref[ci * t:(ci + 1) * t, :].astype(F32)
        dst_ref[row0:row0 + HEAD_DIM, ci * t:(ci + 1) * t] = chunk.T.astype(dst_ref.dtype)


def _head_cols(hh):
    return slice(hh * HEAD_DIM, (hh + 1) * HEAD_DIM)


def _sb_prompt_kernel(q_ref, k_ref, v_ref, upper_ref, o_ref, vt_sc, *, t):
    qi = pl.program_id(2)
    heads = range(SB_HEADS)

    @pl.when(qi == 0)
    def _():
        for hh in heads:
            _store_transposed(vt_sc.at[hh], v_ref.at[:, _head_cols(hh)], t)

    nsub = t // ATT_UNIT
    units = [(hh, cb) for hh in heads for cb in range(nsub)]
    qs = [q_ref[cb * ATT_UNIT:(cb + 1) * ATT_UNIT, _head_cols(hh)] for hh, cb in units]
    keys = lax.broadcasted_iota(jnp.int32, (ATT_UNIT, ATT_UNIT), 0)
    qrys = lax.broadcasted_iota(jnp.int32, (ATT_UNIT, ATT_UNIT), 1)
    tri = keys < qrys

    def block(j, carry, masked):
        start = pl.multiple_of(j * t, t)

        def scores(u, _):
            hh, _cb = units[u]
            return lax.dot_general(k_ref[pl.ds(start, t), _head_cols(hh)], qs[u], (((1,), (1,)), ((), ())),
                                   preferred_element_type=F32)

        def softplus(u, z):
            cb = units[u][1]
            c = carry[u][1]
            subs = []
            for kb in range(nsub - 1, -1, -1):
                if masked and kb > cb:
                    continue
                zk = z[kb * ATT_UNIT:(kb + 1) * ATT_UNIT]
                sp = _softplus2(zk)
                diag = masked and kb == cb
                if diag:
                    sp = jnp.where(tri, sp, 0.0)
                subs.append((kb, zk, sp, c, diag))
                c = c + jnp.sum(sp, axis=0, keepdims=True)
            return subs, c

        def suffixes(u, state):
            subs, c = state
            return [sub + (jnp.dot(upper_ref[...], sub[2].astype(BF16), preferred_element_type=F32),)
                    for sub in subs], c

        def weights(u, state):
            subs, c = state
            hh = units[u][0]
            acc = carry[u][0]
            for kb, zk, sp, c_in, diag, suffix in subs:
                log_a = zk - sp - suffix - c_in
                if diag:
                    log_a = jnp.where(tri, log_a, NEG_BIG)
                a = jnp.exp2(log_a).astype(BF16)
                acc = acc + jnp.dot(vt_sc[hh, :, pl.ds(start + kb * ATT_UNIT, ATT_UNIT)], a,
                                    preferred_element_type=F32)
            return acc, c

        return tuple(_emit_skewed(len(units), [scores, softplus, suffixes, weights]))

    carry = tuple((jnp.zeros((HEAD_DIM, ATT_UNIT), F32), jnp.zeros((1, ATT_UNIT), F32)) for _ in units)
    carry = block(qi, carry, True)

    def spent(cr):
        return functools.reduce(jnp.minimum, [jnp.min(c) for _, c in cr])

    def more(state):
        i, _, lowest = state
        return jnp.logical_and(i < qi, lowest < SB_DONE_BITS)

    def step(state):
        i, cr, _ = state
        cr = block(qi - 1 - i, cr, False)
        return i + 1, cr, spent(cr)

    _, carry, _ = lax.while_loop(more, step, (jnp.int32(0), carry, spent(carry)))
    for u, (hh, cb) in enumerate(units):
        o_ref[cb * ATT_UNIT:(cb + 1) * ATT_UNIT, _head_cols(hh)] = carry[u][0].T.astype(o_ref.dtype)


def _sb_prompt(q, k, v, batch, seq, t):
    nq = seq // t
    w = SB_HEADS * HEAD_DIM
    upper =jnp.asarray(np.triu(np.ones((ATT_UNIT, ATT_UNIT), np.float32), 1), dtype=BF16)
    return pl.pallas_call(
        functools.partial(_sb_prompt_kernel, t=t),
        grid=(batch, N_HEADS // SB_HEADS, nq),
        in_specs=[pl.BlockSpec((t, w), lambda b, h, i: (b * nq + i, h)),
                  pl.BlockSpec((seq, w), lambda b, h, i: (b, h)),
                  pl.BlockSpec((seq, w), lambda b, h, i: (b, h)),
                  pl.BlockSpec((ATT_UNIT, ATT_UNIT), lambda b, h, i: (0, 0))],
        out_specs=pl.BlockSpec((t, w), lambda b, h, i: (b * nq + i, h)),
        out_shape=jax.ShapeDtypeStruct((batch * seq, WIDTH), BF16),
        scratch_shapes=[pltpu.VMEM((SB_HEADS, HEAD_DIM, seq), BF16)],
        compiler_params=_cparams(("parallel", "parallel", "arbitrary")),
        name="sb_prompt",
    )(q, k, v, upper)


def _df_prompt_kernel(q_ref, k_ref, v_ref, bias_ref, lq1, lk1, lq2, lk2, g_ref, o_ref,
                      vt_sc, m_sc, l_sc, acc_sc, z_sc, *, t, lam_init):
    qi = pl.program_id(2)
    heads = range(DF_HEADS)

    @pl.when(qi == 0)
    def _():
        for hh in heads:
            _store_transposed(vt_sc.at[hh], v_ref.at[:, _head_cols(hh)], t)
            vt_sc[hh, HEAD_DIM:, :] = jnp.ones((ONES_ROWS, vt_sc.shape[2]), BF16)

    lane = lax.broadcasted_iota(jnp.int32, (t, HEAD_DIM), 1)
    qs = []
    for hh in heads:
        q = q_ref[:, _head_cols(hh)]
        zero = jnp.zeros_like(q)
        qs.append(jnp.concatenate([jnp.where(lane < DF_HALF, q, zero), jnp.where(lane >= DF_HALF, q, zero)], axis=0))
        m_sc[hh] = jnp.full(m_sc.shape[1:], NEG_BIG, F32)
        l_sc[hh] = jnp.zeros(l_sc.shape[1:], F32)
        acc_sc[hh] = jnp.zeros(acc_sc.shape[1:], F32)

    def scores(j, bias_d, slot):
        start = pl.multiple_of(j * t, t)
        for hh in heads:
            z = lax.dot_general(k_ref[pl.ds(start, t), _head_cols(hh)], qs[hh], (((1,), (1,)), ((), ())),
                                preferred_element_type=F32)
            if bias_d is not None:
                bias = bias_ref[hh, bias_d]
                z = z + jnp.concatenate([bias, bias], axis=1)
            z_sc[slot, hh] = z

    def consume(j, slot):
        for hh in heads:
            consume_head(j, z_sc[slot, hh], hh)

    def consume_head(j, z, hh):
        start = pl.multiple_of(j * t, t)
        m_old = m_sc[hh]
        m_new = jnp.maximum(m_old, jnp.max(z, axis=0, keepdims=True))
        p = jnp.exp(z - m_new).astype(BF16)
        alpha = jnp.exp(m_old - m_new)
        pv = jnp.dot(vt_sc[hh, :, pl.ds(start, t)], p, preferred_element_type=F32)
        l_sc[hh] = alpha * l_sc[hh] + pv[HEAD_DIM:HEAD_DIM + 1, :]
        acc_sc[hh] = alpha * acc_sc[hh] + pv[:HEAD_DIM, :]
        m_sc[hh] = m_new

    scores(qi, 0, 0)

    def pair(p, near_bias):
        j = qi - 2 * p
        scores(j - 1, near_bias, 1)
        consume(j, 0)
        scores(j - 2, None, 0)
        consume(j - 1, 1)

    @pl.when(qi >= 2)
    def _():
        pair(0, 1)

    def far_pair(p, carry):
        pair(p, None)
        return carry
    lax.fori_loop(1, qi // 2, far_pair, 0)

    def odd_tail(near_bias):
        scores(0, near_bias, 1)
        consume(1, 0)
        consume(0, 1)

    @pl.when(qi == 1)
    def _():
        odd_tail(1)

    @pl.when(jnp.logical_and(qi % 2 == 1, qi > 1))
    def _():
        odd_tail(None)

    @pl.when(qi % 2 == 0)
    def _():
        consume(0, 0)

    lam = _lambda(lq1[...], lk1[...], lq2[...], lk2[...], lam_init)
    for hh in heads:
        o = acc_sc[hh] / l_sc[hh]
        o = (o[:, :t] - lam * o[:, t:]).T
        ms = jnp.mean(o * o, axis=-1, keepdims=True)
        o_ref[:, _head_cols(hh)] = (o * lax.rsqrt(ms + EPS) * g_ref[...] * (1.0 - lam_init)).astype(o_ref.dtype)


def _df_prompt(q, k, v, bias, lams, subln, lam_init, batch, seq, t):
    nq = seq // t
    w = DF_HEADS * HEAD_DIM
    vec64 = pl.BlockSpec((1, DF_HALF), lambda b, h, i: (0, 0))
    return pl.pallas_call(
        functools.partial(_df_prompt_kernel, t=t, lam_init=lam_init),
        grid=(batch, N_HEADS // DF_HEADS, nq),
        in_specs=[pl.BlockSpec((t, w), lambda b, h, i: (b * nq + i, h)),
                  pl.BlockSpec((seq, w), lambda b, h, i: (b, h)),
                  pl.BlockSpec((seq, w), lambda b, h, i: (b, h)),
                  pl.BlockSpec((DF_HEADS, 2, t, t), lambda b, h, i: (h, 0, 0, 0)),
                  vec64, vec64, vec64, vec64,
                  pl.BlockSpec((1, HEAD_DIM), lambda b, h, i: (0, 0))],
        out_specs=pl.BlockSpec((t, w), lambda b, h, i: (b * nq + i, h)),
        out_shape=jax.ShapeDtypeStruct((batch * seq, WIDTH), BF16),
        scratch_shapes=[pltpu.VMEM((DF_HEADS, HEAD_DIM + ONES_ROWS, seq), BF16),
                        pltpu.VMEM((DF_HEADS, 1, 2 * t), F32), pltpu.VMEM((DF_HEADS, 1, 2 * t), F32),
                        pltpu.VMEM((DF_HEADS, HEAD_DIM, 2 * t), F32),
                        pltpu.VMEM((2, DF_HEADS, t, 2 * t), F32)],
        compiler_params=_cparams(("parallel", "parallel", "arbitrary")),
        name="df_prompt",
    )(q, k, v, bias, *lams, subln)


def _score_matrix(q_ref, keep_ref, seq_in_block, n_new, halves):
    row = lax.broadcasted_iota(jnp.int32, (QROWS, QCOLS), 0)
    lane = lax.broadcasted_iota(jnp.int32, (QROWS, QCOLS), 1)
    pick = (lane & (n_new - 1)) + seq_in_block * n_new == row
    if halves == 1:
        pick = pick & ((lane & n_new) == 0)
    spread = lax.dot_general(q_ref[...], jnp.where(pick, 1.0, 0.0).astype(BF16), (((0,), (0,)), ((), ())),
                             preferred_element_type=F32)
    return (spread * keep_ref[...]).astype(BF16)


def _score_pattern(n_new, halves):
    feat = np.arange(WIDTH)[:, None]
    col = np.arange(QCOLS)[None, :]
    keep = feat // HEAD_DIM == col // (2 * n_new)
    if halves == 2:
        keep = keep & ((feat // DF_HALF) % 2 == (col // n_new) % 2)
    return jnp.asarray(keep.astype(np.float32))


def _sample_attn_kernel(pt_ref, qa_ref, qb_ref, keep_sb_ref, keep_df_ref, kan_ref, van_ref, kbn_ref, vbn_ref,
                        blast_ref, bnew_ref, lq1, lk1, lq2, lk2, g_ref, *rest,
                        n_seq, n_pages, n_new, layer, lam_init):
    g = PAGES_PER_STEP
    c_sbk, c_dfk, c_sbv, c_dfv, oa_ref, ob_ref = rest[:6]
    zdf_sc, psb_sc, pdf_sc, accsb_sc, accdf_sc, qsb_sc, qdf_sc, ring, extra_buf, sems, extra_sem, extra_n = rest[6:]
    b = pl.program_id(0)
    s = pl.program_id(1)
    n_kv = n_pages // g
    n_steps = 2 * n_kv
    past = n_pages * PAGE_SIZE
    chunk = g * PAGE_SIZE
    hot = SB_HOT_PAGES * PAGE_SIZE
    group = g + SB_HOT_PAGES

    def page_copy(cache, page, block, slot):
        return pltpu.make_async_copy(cache.at[layer, page], ring.at[block], sems.at[slot])

    def group_copies(seq_i, step, slot):
        k = jnp.where(step < n_kv, step, step - n_kv)
        first = (n_kv - 1 - k) * g
        df = [(seq_i * n_pages + first + i, slot * group + i) for i in range(g)]
        sb = [(seq_i * n_pages + n_pages - SB_HOT_PAGES + i, slot * group + g + i) for i in range(SB_HOT_PAGES)]
        return df, sb, k == 0

    def start_group(seq_i, step, slot):
        df, sb, rides = group_copies(seq_i, step, slot)
        for in_phase, c_df, c_sb in ((step < n_kv, c_dfk, c_sbk), (step >= n_kv, c_dfv, c_sbv)):
            @pl.when(in_phase)
            def _():
                for idx, block in df:
                    page_copy(c_df, pt_ref[idx], block, slot).start()

            @pl.when(jnp.logical_and(in_phase, rides))
            def _():
                for idx, block in sb:
                    page_copy(c_sb, pt_ref[idx], block, slot).start()

    gidx = b * n_steps + s
    slot = lax.rem(gidx, RING_DEPTH)
    k_step = jnp.where(s < n_kv, s, s - n_kv)
    first_page = (n_kv - 1 - k_step) * g

    @pl.when(gidx == 0)
    def _():
        for d in range(RING_DEPTH - 1):
            start_group(jnp.int32(0), jnp.int32(d), jnp.int32(d))

    ahead = s + (RING_DEPTH - 1)
    wraps = ahead >= n_steps
    seq_ahead = jnp.where(wraps, b + 1, b)

    @pl.when(seq_ahead < n_seq)
    def _():
        start_group(seq_ahead, jnp.where(wraps, ahead - n_steps, ahead), lax.rem(gidx + (RING_DEPTH - 1), RING_DEPTH))

    for i in range(g):
        page_copy(c_dfk, 0, slot * group + i, slot).wait()

    @pl.when(k_step == 0)
    def _():
        for i in range(SB_HOT_PAGES):
            page_copy(c_sbk, 0, slot * group + g + i, slot).wait()

    def page_rows_bf16(ref):
        return _tokens_by_width(ref, PAGE_SIZE).astype(BF16)

    def pages_bf16(first, count):
        return jnp.concatenate([page_rows_bf16(ring.at[slot * group + first + i]) for i in range(count)], axis=0)

    def extra_page(cache, page):
        cp = pltpu.make_async_copy(cache.at[layer, pt_ref[b * n_pages + page]], extra_buf, extra_sem.at[0])
        cp.start()
        cp.wait()
        return page_rows_bf16(extra_buf)

    def new_tokens(ref):
        return _tokens_by_width(ref, n_new)

    def _queries():
        seq_in_block = lax.rem(b, QROWS // n_new)
        qsb_sc[...] = _score_matrix(qa_ref, keep_sb_ref, seq_in_block, n_new, 1)
        qdf_sc[...] = _score_matrix(qb_ref, keep_df_ref, seq_in_block, n_new, 2)

    def _scores():
        start = pl.multiple_of(first_page * PAGE_SIZE, chunk)
        zdf_sc[pl.ds(start, chunk), :] = jnp.dot(pages_bf16(0, g), qdf_sc[...], preferred_element_type=F32)

    def pad_rows_bf16(x):
        return jnp.concatenate([x, jnp.zeros_like(x)], axis=0).astype(BF16)

    lane_n = lax.broadcasted_iota(jnp.int32, (n_new, QCOLS), 1)
    key_n = lax.broadcasted_iota(jnp.int32, (n_new, QCOLS), 0)
    qpos_n = lane_n & (n_new - 1)

    def sb_weights(z, carry, upper):
        sp = _softplus2(z)
        suffix = jnp.dot(upper, sp.astype(BF16), preferred_element_type=F32)
        return jnp.exp2(z - sp - suffix - carry).astype(BF16), carry + jnp.sum(sp, axis=0, keepdims=True)

    def unspent(carry):
        real = (lax.broadcasted_iota(jnp.int32, (1, QCOLS), 1) & n_new) == 0
        return jnp.min(jnp.where(real, carry, SB_DONE_BITS))

    def _sb_weights():
        zs_new = jnp.dot(pad_rows_bf16(new_tokens(kan_ref)), qsb_sc[...], preferred_element_type=F32)[:n_new]
        strict = key_n < qpos_n
        sp_new = jnp.where(strict, _softplus2(zs_new), 0.0)
        carry = jnp.zeros((1, QCOLS), F32)
        suffix_rows = [None] * n_new
        for i in range(n_new - 1, -1, -1):
            suffix_rows[i] = carry
            carry = carry + sp_new[i:i + 1]
        suffix_new = jnp.concatenate(suffix_rows, axis=0)
        a_new = jnp.exp2(jnp.where(strict, zs_new - sp_new - suffix_new, NEG_BIG))
        accsb_sc[...] = lax.dot_general(pad_rows_bf16(a_new), pad_rows_bf16(new_tokens(van_ref)),
                                        (((0,), (0,)), ((), ())), preferred_element_type=F32)
        rr = lax.broadcasted_iota(jnp.int32, (hot, hot), 0)
        cc = lax.broadcasted_iota(jnp.int32, (hot, hot), 1)
        upper = jnp.where(cc > rr, 1.0, 0.0).astype(BF16)
        z_hot = jnp.dot(pages_bf16(g, SB_HOT_PAGES), qsb_sc[...], preferred_element_type=F32)
        a_hot, carry = sb_weights(z_hot, carry, upper)
        psb_sc[past - hot:past, :] = a_hot

        def more(state):
            page, _, lowest = state
            return jnp.logical_and(page >= 0, lowest < SB_DONE_BITS)

        def older(state):
            page, carry, _ = state
            z = jnp.dot(extra_page(c_sbk, page), qsb_sc[...], preferred_element_type=F32)
            a, carry = sb_weights(z, carry, upper[:PAGE_SIZE, :PAGE_SIZE])
            psb_sc[pl.ds(pl.multiple_of(page * PAGE_SIZE, PAGE_SIZE), PAGE_SIZE), :] = a
            return page - 1, carry, unspent(carry)

        oldest_hot = n_pages - SB_HOT_PAGES
        page, _, _ = lax.while_loop(more, older, (jnp.int32(oldest_hot - 1), carry, unspent(carry)))
        extra_n[0] = oldest_hot - 1 - page

    def _df_weights():
        z_new = jnp.dot(pad_rows_bf16(new_tokens(kbn_ref)), qdf_sc[...], preferred_element_type=F32)[:n_new]
        z_new = z_new + bnew_ref[...]
        z_new = jnp.where(key_n <= qpos_n, z_new, NEG_BIG)
        last0 = past - PAGE_SIZE
        zdf_sc[last0:past, :] = zdf_sc[last0:past, :] + blast_ref[...]
        zp = zdf_sc[...]
        mx = jnp.maximum(jnp.max(zp, axis=0, keepdims=True), jnp.max(z_new, axis=0, keepdims=True))
        e_new = jnp.exp(z_new - mx)
        ep = jnp.exp(zp - mx)
        inv = 1.0 / (jnp.sum(ep, axis=0, keepdims=True) + jnp.sum(e_new, axis=0, keepdims=True))
        pdf_sc[...] = (ep * inv).astype(BF16)
        accdf_sc[...] = lax.dot_general(pad_rows_bf16(e_new * inv), pad_rows_bf16(new_tokens(vbn_ref)),
                                        (((0,), (0,)), ((), ())), preferred_element_type=F32)

    def _values():
        start = pl.multiple_of(first_page * PAGE_SIZE, chunk)
        accdf_sc[...] += lax.dot_general(pdf_sc[pl.ds(start, chunk), :], pages_bf16(0, g),
                                         (((0,), (0,)), ((), ())), preferred_element_type=F32)

    def _sb_values():
        accsb_sc[...] += lax.dot_general(psb_sc[past - hot:past, :], pages_bf16(g, SB_HOT_PAGES),
                                         (((0,), (0,)), ((), ())), preferred_element_type=F32)

        def older(i, carry):
            page = n_pages - SB_HOT_PAGES - 1 - i
            rows = pl.ds(pl.multiple_of(page * PAGE_SIZE, PAGE_SIZE), PAGE_SIZE)
            accsb_sc[...] += lax.dot_general(psb_sc[rows, :], extra_page(c_sbv, page),
                                             (((0,), (0,)), ((), ())), preferred_element_type=F32)
            return carry
        lax.fori_loop(0, extra_n[0], older, 0)

    def _finish():
        lam = _lambda(lq1[...], lk1[...], lq2[...], lk2[...], lam_init)
        for h in range(N_HEADS):
            cols = slice(h * HEAD_DIM, (h + 1) * HEAD_DIM)
            oa_ref[:, cols] = accsb_sc[h * 2 * n_new:h * 2 * n_new + n_new, cols]
            o1 = accdf_sc[h * 2 * n_new:h * 2 * n_new + n_new, cols]
            o2 = accdf_sc[h * 2 * n_new + n_new:(h + 1) * 2 * n_new, cols]
            o = o1 - lam * o2
            ms = jnp.mean(o * o, axis=-1, keepdims=True)
            ob_ref[:, cols] = o * lax.rsqrt(ms + EPS) * g_ref[...] * (1.0 - lam_init)

    for k in range(n_kv):
        @pl.when(s == k)
        def _():
            if k == 0:
                _queries()
            _scores()
            if k == 0:
                _sb_weights()
            if k == n_kv - 1:
                _df_weights()

        @pl.when(s == n_kv + k)
        def _():
            _values()
            if k == 0:
                _sb_values()
            if k == n_kv - 1:
                _finish()


def _sample_attn(page_table, qsb, qdf, ka_n, va_n, kb_n, vb_n, blast, bnew, lams, subln,
                 c_sbk, c_sbv, c_dfk, c_dfv, layer, lam_init):
    n_seq, n_pages = page_table.shape
    n_new = ka_n.shape[0] // (n_seq * N_HEADS)
    assert 2 * N_HEADS * n_new == QCOLS
    g = PAGES_PER_STEP
    n_kv = n_pages // g
    past = n_pages * PAGE_SIZE

    assert n_pages % g == 0 and RING_DEPTH - 1 <= 2 * n_kv and SB_HOT_PAGES <= n_pages
    vec64 = pl.BlockSpec((1, DF_HALF), lambda b, s, pt: (0, 0))
    new_spec = pl.BlockSpec((n_new * N_HEADS, HEAD_DIM), lambda b, s, pt: (b, 0))
    out_spec = pl.BlockSpec((n_new, WIDTH), lambda b, s, pt: (b, 0))
    q_spec = pl.BlockSpec((QROWS, WIDTH), lambda b, s, pt: (b // (QROWS // n_new), 0))
    keep_spec = pl.BlockSpec((WIDTH, QCOLS), lambda b, s, pt: (0, 0))
    in_specs = [q_spec, q_spec, keep_spec, keep_spec,
                new_spec, new_spec, new_spec, new_spec,
                pl.BlockSpec((PAGE_SIZE, QCOLS), lambda b, s, pt: (0, 0)),
                pl.BlockSpec((n_new, QCOLS), lambda b, s, pt: (0, 0)),
                vec64, vec64, vec64, vec64,
                pl.BlockSpec((1, HEAD_DIM), lambda b, s, pt: (0, 0))]
    in_specs += [pl.BlockSpec(memory_space=pl.ANY)] * 4
    grid_spec = pltpu.PrefetchScalarGridSpec(
        num_scalar_prefetch=1,
        grid=(n_seq, 2 * n_kv),
        in_specs=in_specs,
        out_specs=[out_spec, out_spec],
        scratch_shapes=[pltpu.VMEM((past, QCOLS), F32),
                        pltpu.VMEM((past, QCOLS), BF16), pltpu.VMEM((past, QCOLS), BF16),
                        pltpu.VMEM((QCOLS, WIDTH), F32), pltpu.VMEM((QCOLS, WIDTH), F32),
                        pltpu.VMEM((WIDTH, QCOLS), BF16), pltpu.VMEM((WIDTH, QCOLS), BF16),
                        pltpu.VMEM((RING_DEPTH * (g + SB_HOT_PAGES), PAGE_SIZE * N_HEADS, HEAD_DIM), F32),
                        pltpu.VMEM((PAGE_SIZE * N_HEADS, HEAD_DIM), F32),
                        pltpu.SemaphoreType.DMA((RING_DEPTH,)), pltpu.SemaphoreType.DMA((1,)),
                        pltpu.SMEM((1,), jnp.int32)])
    return pl.pallas_call(
        functools.partial(_sample_attn_kernel, n_seq=n_seq, n_pages=n_pages, n_new=n_new, layer=layer,
                          lam_init=lam_init),
        grid_spec=grid_spec,
        out_shape=[jax.ShapeDtypeStruct((n_seq * n_new, WIDTH), F32)] * 2,
        compiler_params=_cparams(("arbitrary", "arbitrary")),
        name="sample_attn",
    )(page_table.reshape(-1), qsb, qdf, _score_pattern(n_new, 1), _score_pattern(n_new, 2),
      ka_n, va_n, kb_n, vb_n, blast, bnew, *lams, subln, c_sbk, c_dfk, c_sbv, c_dfv)


def _merge_kernel(x_ref, oa_ref, ob_ref, ga_ref, gb_ref, wsb_ref, wdf_ref, wout_ref, g2_ref, x1_ref, hn_ref):
    ya = jnp.dot(oa_ref[...].astype(BF16), wsb_ref[...], preferred_element_type=F32)
    yb = jnp.dot(ob_ref[...].astype(BF16), wdf_ref[...], preferred_element_type=F32)
    m = jax.nn.sigmoid(ga_ref[...].astype(F32)) * ya + jax.nn.sigmoid(gb_ref[...].astype(F32)) * yb
    x1 = x_ref[...] + jnp.dot(m.astype(BF16), wout_ref[...], preferred_element_type=F32)
    x1_ref[...] = x1
    ms = jnp.mean(x1 * x1, axis=-1, keepdims=True)
    hn_ref[...] = (x1 * lax.rsqrt(ms + EPS) * g2_ref[...]).astype(BF16)


def _merge(x, oa, ob, gates, wsb, wdf, wout, norm2, tm):
    m, d = x.shape
    const = lambda shape: pl.BlockSpec(shape, lambda i: (0, 0), pipeline_mode=pl.Buffered(1))
    return pl.pallas_call(
        _merge_kernel,
        grid=(m // tm,),
        in_specs=[pl.BlockSpec((tm, d), lambda i: (i, 0)),
                  pl.BlockSpec((tm, WIDTH), lambda i: (i, 0)),
                  pl.BlockSpec((tm, WIDTH), lambda i: (i, 0)),
                  pl.BlockSpec((tm, d), lambda i: (i, 0)),
                  pl.BlockSpec((tm, d), lambda i: (i, 1)),
                  const((WIDTH, d)), const((WIDTH, d)), const((d, d)),
                  pl.BlockSpec((1, d), lambda i: (0, 0))],
        out_specs=[pl.BlockSpec((tm, d), lambda i: (i, 0)), pl.BlockSpec((tm, d), lambda i: (i, 0))],
        out_shape=[jax.ShapeDtypeStruct((m, d), F32), jax.ShapeDtypeStruct((m, d), BF16)],
        compiler_params=_cparams(("parallel",)),
        name="merge",
    )(x, oa, ob, gates, gates, wsb, wdf, wout, norm2.reshape(1, d))


def _mlp_kernel(x1_ref, hn_ref, wup_ref, wdn_ref, o_ref):
    f = pl.program_id(1)

    @pl.when(f == 0)
    def _():
        o_ref[...] = x1_ref[...]

    u = jnp.maximum(jnp.dot(hn_ref[...], wup_ref[...], preferred_element_type=F32), 0.0)
    o_ref[...] += jnp.dot((u * u).astype(BF16), wdn_ref[...], preferred_element_type=F32)


def _mlp(x1, hn, wup, wdn, tm, tf):
    m, d = x1.shape
    dff = wup.shape[1]
    return pl.pallas_call(
        _mlp_kernel,
        grid=(m // tm, dff // tf),
        in_specs=[pl.BlockSpec((tm, d), lambda i, f: (i, 0)),
                  pl.BlockSpec((tm, d), lambda i, f: (i, 0)),
                  pl.BlockSpec((d, tf), lambda i, f: (0, f)),
                  pl.BlockSpec((tf, d), lambda i, f: (f, 0))],
        out_specs=pl.BlockSpec((tm, d), lambda i, f: (i, 0)),
        out_shape=jax.ShapeDtypeStruct((m, d), F32),
        compiler_params=_cparams(("parallel", "arbitrary")),
        name="mlp",
    )(x1, hn, wup, wdn)


def _project_group(x, norm1, w_in, q_norm, k_norm, tm):
    h = _rmsnorm_bf16(x, norm1, min(tm, 512))
    (qa,) = _proj(h, w_in, COL_QA, 1, tm=tm, scale=SB_SCALE * LOG2E)
    ka, ka16 = _proj(h, w_in, COL_KA, 1, tm=tm, want_f32=True)
    va, va16 = _proj(h, w_in, COL_VA, 1, tm=tm, want_f32=True)
    (qb,) = _proj(h, w_in, COL_QB, 1, tm=tm, qknorm=True, gain=q_norm, scale=DF_SCALE)
    kb, kb16 = _proj(h, w_in, COL_KB, 1, tm=tm, qknorm=True, gain=k_norm, want_f32=True)
    vb, vb16 = _proj(h, w_in, COL_VB, 1, tm=tm, want_f32=True)
    (gates,) = _proj(h, w_in, COL_GA, 4, tm=tm)
    return dict(qa=qa, ka=ka, ka16=ka16, va=va, va16=va16, qb=qb, kb=kb, kb16=kb16, vb=vb, vb16=vb16, gates=gates)


def kernel(x_prompt, x_sample, cache_sb_k, cache_sb_v, cache_df_k, cache_df_v, page_table, rel_bias, norm1, w_in, q_norm, k_norm, lambda_q1, lambda_k1, lambda_q2, lambda_k2, subln, w_branch_sb, w_branch_df, w_out, norm2, w_up, w_down):
    depth = norm1.shape[0]
    batch, seq, d = x_prompt.shape
    n_seq, n_new, _ = x_sample.shape
    n_pages = page_table.shape[1]
    past = n_pages * PAGE_SIZE
    xp = x_prompt.reshape(batch * seq, d)
    xs = x_sample.reshape(n_seq * n_new, d)
    bias_p = _bias_prompt(rel_bias, ATT_T)
    bias_last, bias_new = _bias_sample(rel_bias, past, n_new)
    n_pool = cache_sb_k.shape[1]
    leaves = [[] for _ in range(8)]
    for l in range(depth):
        lam_init = 0.8 - 0.6 * math.exp(-0.3 * l)
        w_in16 = w_in[l].astype(BF16)
        wsb16 = w_branch_sb[l].astype(BF16)
        wdf16 = w_branch_df[l].astype(BF16)
        wout16 = w_out[l].astype(BF16)
        wup16 = w_up[l].astype(BF16)
        wdn16 = w_down[l].astype(BF16)
        lams = [v[l].reshape(1, DF_HALF).astype(F32) for v in (lambda_q1, lambda_k1, lambda_q2, lambda_k2)]
        sub = subln[l].reshape(1, HEAD_DIM).astype(F32)

        p = _project_group(xp, norm1[l], w_in16, q_norm[l], k_norm[l], 1024)
        oa = _sb_prompt(p["qa"], p["ka16"], p["va16"], batch, seq, ATT_T)
        ob = _df_prompt(p["qb"], p["kb16"], p["vb16"], bias_p, lams, sub, lam_init, batch, seq, ATT_T)
        x1, hn = _merge(xp, oa, ob, p["gates"], wsb16, wdf16, wout16, norm2[l], 256)
        xp = _mlp(x1, hn, wup16, wdn16, 512, 1024)
        for i, name in enumerate(("ka", "va", "kb", "vb")):
            leaves[i].append(p[name].reshape(batch, seq, N_HEADS, HEAD_DIM))

        s = _project_group(xs, norm1[l], w_in16, q_norm[l], k_norm[l], 1024)
        shape4 = (depth, n_pool, PAGE_SIZE * N_HEADS, HEAD_DIM)
        oa, ob = _sample_attn(page_table, s["qa"], s["qb"], s["ka"], s["va"], s["kb"], s["vb"], bias_last, bias_new,
                              lams, sub, cache_sb_k.reshape(shape4), cache_sb_v.reshape(shape4),
                              cache_df_k.reshape(shape4), cache_df_v.reshape(shape4), l, lam_init)
        x1, hn = _merge(xs, oa, ob, s["gates"], wsb16, wdf16, wout16, norm2[l], 256)
        xs = _mlp(x1, hn, wup16, wdn16, 512, 1024)
        for i, name in enumerate(("ka", "va", "kb", "vb")):
            leaves[4 + i].append(s[name].reshape(n_seq, n_new, N_HEADS, HEAD_DIM))

    return (xp.reshape(batch, seq, d), xs.reshape(n_seq, n_new, d)) + tuple(jnp.stack(v) for v in leaves)
```

```python
import functools
import math

import numpy as np
import jax
import jax.numpy as jnp
from jax import lax
from jax.experimental import pallas as pl
from jax.experimental.pallas import tpu as pltpu

F32 = jnp.float32
BF16 = jnp.bfloat16

D_MODEL = 2048
N_HEADS = 8
HEAD_DIM = 128
DF_HALF = 64
WIDTH = N_HEADS * HEAD_DIM
SB_SCALE = 1.0 / math.sqrt(HEAD_DIM)
DF_SCALE = 1.0 / math.sqrt(DF_HALF)
D_FF = 4 * D_MODEL
N_BUCKETS = 32
MAX_EXACT = N_BUCKETS // 2
MAX_DISTANCE = 128
EPS = 1e-6
PAGE_SIZE = 128
NEG_BIG = -1e30

COL_QA, COL_KA, COL_VA, COL_QB, COL_KB, COL_VB, COL_GA, COL_GB = 0, 1, 2, 3, 4, 5, 6, 8

ATT_T = 512
SB_HEADS = 4
DF_HEADS = 2
ATT_UNIT = 256
SB_DONE_BITS = 160.0
ONES_ROWS = 16
LOG2E = 1.4426950408889634
PAGES_PER_STEP = 16
SB_HOT_PAGES = 2
RING_DEPTH = 3
QCOLS = 128
QROWS = 16

VMEM_LIMIT = 56 * 1024 * 1024


def _cparams(sem):
    return pltpu.CompilerParams(dimension_semantics=sem, vmem_limit_bytes=VMEM_LIMIT)


def _tokens_by_width(ref, n_tok):
    return jnp.concatenate([ref[pl.ds(h, n_tok, stride=N_HEADS), :] for h in range(N_HEADS)], axis=1)


def _softplus2(z):
    return jnp.maximum(z, 0.0) + jnp.log2(1.0 + jnp.exp2(-jnp.abs(z)))


def _emit_skewed(n_units, stages):
    state = [None] * n_units
    for tick in range(n_units + len(stages) - 1):
        for si in range(len(stages) - 1, -1, -1):
            u = tick - si
            if 0 <= u < n_units:
                state[u] = stages[si](u, state[u])
    return state


def _rmsnorm_kernel(x_ref, g_ref, o_ref):
    x = x_ref[...]
    ms = jnp.mean(x * x, axis=-1, keepdims=True)
    o_ref[...] = (x * lax.rsqrt(ms + EPS) * g_ref[...]).astype(o_ref.dtype)


def _rmsnorm_bf16(x, g, tm):
    m, d = x.shape
    return pl.pallas_call(
        _rmsnorm_kernel,
        grid=(m // tm,),
        in_specs=[pl.BlockSpec((tm, d), lambda i: (i, 0)),
                  pl.BlockSpec((1, d), lambda i: (0, 0))],
        out_specs=pl.BlockSpec((tm, d), lambda i: (i, 0)),
        out_shape=jax.ShapeDtypeStruct((m, d), BF16),
        compiler_params=_cparams(("parallel",)),
        name="rmsnorm_bf16",
    )(x, g.reshape(1, d))


def _proj_kernel(*refs, qknorm, scale, want_f32, want_bf16):
    h_ref, w_ref = refs[0], refs[1]
    pos = 2
    if qknorm:
        gain_ref, pmat_ref = refs[2], refs[3]
        pos = 4
    outs = refs[pos:]
    y = jnp.dot(h_ref[...], w_ref[...].astype(BF16), preferred_element_type=F32)
    if qknorm:
        tn = y.shape[1]
        pieces = []
        for c in range(tn // 256):
            yb = y[:, c * 256:(c + 1) * 256]
            ms = jnp.dot((yb * yb).astype(BF16), pmat_ref[...], preferred_element_type=F32)
            pieces.append(yb * lax.rsqrt(ms + EPS))
        y = jnp.concatenate(pieces, axis=1) * gain_ref[...]
    k = 0
    if want_f32:
        for hd in range(N_HEADS):
            outs[k][pl.ds(hd, y.shape[0], stride=N_HEADS), :] = y[:, hd * HEAD_DIM:(hd + 1) * HEAD_DIM]
        k += 1
    if want_bf16:
        outs[k][...] = (y * scale).astype(BF16) if scale != 1.0 else y.astype(BF16)


def _proj(h, w, col0, ncol, *, tm, qknorm=False, gain=None, scale=1.0, want_f32=False, want_bf16=True):
    m, kdim = h.shape
    tn = WIDTH
    in_specs = [pl.BlockSpec((tm, kdim), lambda n, i: (i, 0)),
                pl.BlockSpec((kdim, tn), lambda n, i: (0, col0 + n))]
    args = [h, w]
    if qknorm:
        pmat = np.kron(np.eye(256 // DF_HALF), np.full((DF_HALF, DF_HALF), 1.0 / DF_HALF)).astype(np.float32)
        in_specs += [pl.BlockSpec((1, tn), lambda n, i: (0, 0)),
                     pl.BlockSpec((256, 256), lambda n, i: (0, 0))]
        args += [jnp.tile(gain.astype(F32), tn // DF_HALF).reshape(1, tn), jnp.asarray(pmat, dtype=BF16)]
    out_specs, out_shape = [], []
    if want_f32:
        assert ncol == 1
        out_specs.append(pl.BlockSpec((tm * N_HEADS, HEAD_DIM), lambda n, i: (i, 0)))
        out_shape.append(jax.ShapeDtypeStruct((m * N_HEADS, HEAD_DIM), F32))
    if want_bf16:
        out_specs.append(pl.BlockSpec((tm, tn), lambda n, i: (i, n)))
        out_shape.append(jax.ShapeDtypeStruct((m, ncol * tn), BF16))
    res = pl.pallas_call(
        functools.partial(_proj_kernel, qknorm=qknorm, scale=scale, want_f32=want_f32, want_bf16=want_bf16),
        grid=(ncol, m // tm),
        in_specs=in_specs,
        out_specs=out_specs,
        out_shape=out_shape,
        compiler_params=_cparams(("parallel", "parallel")),
        name="in_proj",
    )(*args)
    return res


def _bucket(rel):
    n = jnp.maximum(rel, 0)
    nf = jnp.maximum(n, 1).astype(F32)
    large = MAX_EXACT + (jnp.log(nf / MAX_EXACT) / math.log(MAX_DISTANCE / MAX_EXACT)
                         * (N_BUCKETS - MAX_EXACT)).astype(jnp.int32)
    large = jnp.minimum(large, N_BUCKETS - 1)
    return jnp.where(n < MAX_EXACT, n, large)


def _bias_prompt_kernel(rb_ref, o_ref, *, t):
    h = pl.program_id(0)
    far = rb_ref[(N_BUCKETS - 1) * N_HEADS + h]

    def shifted_bias(rel):
        b = _bucket(rel)
        acc = jnp.zeros(rel.shape, F32)
        for k in range(N_BUCKETS - 1):
            acc = jnp.where(b == k, rb_ref[k * N_HEADS + h] - far, acc)
        return acc

    keys = lax.broadcasted_iota(jnp.int32, (t, t), 0)
    qrys = lax.broadcasted_iota(jnp.int32, (t, t), 1)
    o_ref[0, 0] = jnp.where(keys <= qrys, shifted_bias(qrys - keys), NEG_BIG)
    c = MAX_DISTANCE
    ki = lax.broadcasted_iota(jnp.int32, (c, c), 0)
    qj = lax.broadcasted_iota(jnp.int32, (c, c), 1)
    corner = jnp.concatenate([shifted_bias(qj - ki + c), jnp.zeros((c, t - c), F32)], axis=1)
    o_ref[0, 1] = jnp.concatenate([jnp.zeros((t - c, t), F32), corner], axis=0)


def _bias_prompt(rel_bias, t):
    return pl.pallas_call(
        functools.partial(_bias_prompt_kernel, t=t),
        grid=(N_HEADS,),
        in_specs=[pl.BlockSpec(memory_space=pltpu.SMEM)],
        out_specs=pl.BlockSpec((1, 2, t, t), lambda h: (h, 0, 0, 0)),
        out_shape=jax.ShapeDtypeStruct((N_HEADS, 2, t, t), F32),
        compiler_params=_cparams(("parallel",)),
        name="bias_prompt",
    )(rel_bias.reshape(-1))


def _bias_sample_kernel(rb_ref, last_ref, new_ref, *, past, n_new):
    def tile(rows, key0):
        lane = lax.broadcasted_iota(jnp.int32, (rows, QCOLS), 1)
        key = lax.broadcasted_iota(jnp.int32, (rows, QCOLS), 0) + key0
        head = lax.shift_right_logical(lane, (2 * n_new).bit_length() - 1)
        rel = past + (lane & (n_new - 1)) - key
        b = _bucket(rel)
        acc = jnp.zeros((rows, QCOLS), F32)
        for hh in range(N_HEADS):
            far = rb_ref[(N_BUCKETS - 1) * N_HEADS + hh]
            for k in range(N_BUCKETS - 1):
                acc = jnp.where((b == k) & (head == hh), rb_ref[k * N_HEADS + hh] - far, acc)
        return acc
    last_ref[...] = tile(PAGE_SIZE, past - PAGE_SIZE)
    new_ref[...] = tile(n_new, past)


def _bias_sample(rel_bias, past, n_new):
    return pl.pallas_call(
        functools.partial(_bias_sample_kernel, past=past, n_new=n_new),
        in_specs=[pl.BlockSpec(memory_space=pltpu.SMEM)],
        out_specs=[pl.BlockSpec((PAGE_SIZE, QCOLS), lambda: (0, 0)),
                   pl.BlockSpec((n_new, QCOLS), lambda: (0, 0))],
        out_shape=[jax.ShapeDtypeStruct((PAGE_SIZE, QCOLS), F32),
                   jax.ShapeDtypeStruct((n_new, QCOLS), F32)],
        name="bias_sample",
    )(rel_bias.reshape(-1))


def _lambda(lq1, lk1, lq2, lk2, lam_init):
    s1 = jnp.sum(lq1 * lk1, axis=-1, keepdims=True)
    s2 = jnp.sum(lq2 * lk2, axis=-1, keepdims=True)
    return jnp.exp(s1) - jnp.exp(s2) + lam_init


def _store_transposed(dst_ref, src_ref, t, row0=0):
    for ci in range(src_ref.shape[0] // t):
        chunk = src_ref[ci * t:(ci + 1) * t, :].astype(F32)
        dst_ref[row0:row0 + HEAD_DIM, ci * t:(ci + 1) * t] = chunk.T.astype(dst_ref.dtype)


def _head_cols(hh):
    return slice(hh * HEAD_DIM, (hh + 1) * HEAD_DIM)


def _sb_prompt_kernel(q_ref, k_ref, v_ref, upper_ref, o_ref, vt_sc, *, t):
    qi = pl.program_id(2)
    heads = range(SB_HEADS)

    @pl.when(qi == 0)
    def _():
        for hh in heads:
            _store_transposed(vt_sc.at[hh], v_ref.at[:, _head_cols(hh)], t)

    nsub = t // ATT_UNIT
    units = [(hh, cb) for hh in heads for cb in range(nsub)]
    qs = [q_ref[cb * ATT_UNIT:(cb + 1) * ATT_UNIT, _head_cols(hh)] for hh, cb in units]
    keys = lax.broadcasted_iota(jnp.int32, (ATT_UNIT, ATT_UNIT), 0)
    qrys = lax.broadcasted_iota(jnp.int32, (ATT_UNIT, ATT_UNIT), 1)
    tri = keys < qrys

    def block(j, carry, masked):
        start = pl.multiple_of(j * t, t)

        def scores(u, _):
            hh, _cb = units[u]
            return lax.dot_general(k_ref[pl.ds(start, t), _head_cols(hh)], qs[u], (((1,), (1,)), ((), ())),
                                   preferred_element_type=F32)

        def softplus(u, z):
            cb = units[u][1]
            c = carry[u][1]
            subs = []
            for kb in range(nsub - 1, -1, -1):
                if masked and kb > cb:
                    continue
                zk = z[kb * ATT_UNIT:(kb + 1) * ATT_UNIT]
                sp = _softplus2(zk)
                diag = masked and kb == cb
                if diag:
                    sp = jnp.where(tri, sp, 0.0)
                subs.append((kb, zk, sp, c, diag))
                c = c + jnp.sum(sp, axis=0, keepdims=True)
            return subs, c

        def suffixes(u, state):
            subs, c = state
            return [sub + (jnp.dot(upper_ref[...], sub[2].astype(BF16), preferred_element_type=F32),)
                    for sub in subs], c

        def weights(u, state):
            subs, c = state
            hh = units[u][0]
            acc = carry[u][0]
            for kb, zk, sp, c_in, diag, suffix in subs:
                log_a = zk - sp - suffix - c_in
                if diag:
                    log_a = jnp.where(tri, log_a, NEG_BIG)
                a = jnp.exp2(log_a).astype(BF16)
                acc = acc + jnp.dot(vt_sc[hh, :, pl.ds(start + kb * ATT_UNIT, ATT_UNIT)], a,
                                    preferred_element_type=F32)
            return acc, c

        return tuple(_emit_skewed(len(units), [scores, softplus, suffixes, weights]))

    carry = tuple((jnp.zeros((HEAD_DIM, ATT_UNIT), F32), jnp.zeros((1, ATT_UNIT), F32)) for _ in units)
    carry = block(qi, carry, True)

    def spent(cr):
        return functools.reduce(jnp.minimum, [jnp.min(c) for _, c in cr])

    def more(state):
        i, _, lowest = state
        return jnp.logical_and(i < qi, lowest < SB_DONE_BITS)

    def step(state):
        i, cr, _ = state
        cr = block(qi - 1 - i, cr, False)
        return i + 1, cr, spent(cr)

    _, carry, _ = lax.while_loop(more, step, (jnp.int32(0), carry, spent(carry)))
    for u, (hh, cb) in enumerate(units):
        o_ref[cb * ATT_UNIT:(cb + 1) * ATT_UNIT, _head_cols(hh)] = carry[u][0].T.astype(o_ref.dtype)


def _sb_prompt(q, k, v, batch, seq, t):
    nq = seq // t
    w = SB_HEADS * HEAD_DIM
    upper =jnp.asarray(np.triu(np.ones((ATT_UNIT, ATT_UNIT), np.float32), 1), dtype=BF16)
    return pl.pallas_call(
        functools.partial(_sb_prompt_kernel, t=t),
        grid=(batch, N_HEADS // SB_HEADS, nq),
        in_specs=[pl.BlockSpec((t, w), lambda b, h, i: (b * nq + i, h)),
                  pl.BlockSpec((seq, w), lambda b, h, i: (b, h)),
                  pl.BlockSpec((seq, w), lambda b, h, i: (b, h)),
                  pl.BlockSpec((ATT_UNIT, ATT_UNIT), lambda b, h, i: (0, 0))],
        out_specs=pl.BlockSpec((t, w), lambda b, h, i: (b * nq + i, h)),
        out_shape=jax.ShapeDtypeStruct((batch * seq, WIDTH), BF16),
        scratch_shapes=[pltpu.VMEM((SB_HEADS, HEAD_DIM, seq), BF16)],
        compiler_params=_cparams(("parallel", "parallel", "arbitrary")),
        name="sb_prompt",
    )(q, k, v, upper)


def _df_prompt_kernel(q_ref, k_ref, v_ref, bias_ref, lq1, lk1, lq2, lk2, g_ref, o_ref,
                      vt_sc, m_sc, l_sc, acc_sc, z_sc, *, t, lam_init):
    qi = pl.program_id(2)
    nq = pl.num_programs(2)
    heads = range(DF_HEADS)

    @pl.when(qi == 0)
    def _():
        for hh in heads:
            _store_transposed(vt_sc.at[hh], v_ref.at[:, _head_cols(hh)], t)
            vt_sc[hh, HEAD_DIM:, :] = jnp.ones((ONES_ROWS, vt_sc.shape[2]), BF16)

    lane = lax.broadcasted_iota(jnp.int32, (t, HEAD_DIM), 1)

    def stacked_queries(block):
        out = []
        for hh in heads:
            q = q_ref[pl.ds(pl.multiple_of(block * t, t), t), _head_cols(hh)]
            zero = jnp.zeros_like(q)
            out.append(jnp.concatenate([jnp.where(lane < DF_HALF, q, zero), jnp.where(lane >= DF_HALF, q, zero)],
                                       axis=0))
        return out

    qs = stacked_queries(qi)
    for hh in heads:
        m_sc[hh] = jnp.full(m_sc.shape[1:], NEG_BIG, F32)
        l_sc[hh] = jnp.zeros(l_sc.shape[1:], F32)
        acc_sc[hh] = jnp.zeros(acc_sc.shape[1:], F32)

    def scores(j, bias_d, slot, qs=qs):
        start = pl.multiple_of(j * t, t)
        for hh in heads:
            z = lax.dot_general(k_ref[pl.ds(start, t), _head_cols(hh)], qs[hh], (((1,), (1,)), ((), ())),
                                preferred_element_type=F32)
            if bias_d is not None:
                bias = bias_ref[hh, bias_d]
                z = z + jnp.concatenate([bias, bias], axis=1)
            z_sc[slot, hh] = z

    def consume(j, slot):
        for hh in heads:
            consume_head(j, z_sc[slot, hh], hh)

    def consume_head(j, z, hh):
        start = pl.multiple_of(j * t, t)
        m_old = m_sc[hh]
        m_new = jnp.maximum(m_old, jnp.max(z, axis=0, keepdims=True))
        p = jnp.exp(z - m_new).astype(BF16)
        alpha = jnp.exp(m_old - m_new)
        pv = jnp.dot(vt_sc[hh, :, pl.ds(start, t)], p, preferred_element_type=F32)
        l_sc[hh] = alpha * l_sc[hh] + pv[HEAD_DIM:HEAD_DIM + 1, :]
        acc_sc[hh] = alpha * acc_sc[hh] + pv[:HEAD_DIM, :]
        m_sc[hh] = m_new

    @pl.when(qi == 0)
    def _():
        scores(qi, 0, 0)

    def pair(p, near_bias):
        j = qi - 2 * p
        scores(j - 1, near_bias, 1)
        consume(j, 0)
        scores(j - 2, None, 0)
        consume(j - 1, 1)

    @pl.when(qi >= 2)
    def _():
        pair(0, 1)

    def far_pair(p, carry):
        pair(p, None)
        return carry
    lax.fori_loop(1, qi // 2, far_pair, 0)

    def odd_tail(near_bias):
        scores(0, near_bias, 1)
        consume(1, 0)
        consume(0, 1)

    @pl.when(qi == 1)
    def _():
        odd_tail(1)

    @pl.when(jnp.logical_and(qi % 2 == 1, qi > 1))
    def _():
        odd_tail(None)

    @pl.when(qi % 2 == 0)
    def _():
        consume(0, 0)

    def write_out():
        lam = _lambda(lq1[...], lk1[...], lq2[...], lk2[...], lam_init)
        for hh in heads:
            o = acc_sc[hh] / l_sc[hh]
            o = (o[:, :t] - lam * o[:, t:]).T
            ms = jnp.mean(o * o, axis=-1, keepdims=True)
            o_ref[:, _head_cols(hh)] = (o * lax.rsqrt(ms + EPS) * g_ref[...] * (1.0 - lam_init)).astype(o_ref.dtype)

    @pl.when(qi + 1 < nq)
    def _():
        scores(qi + 1, 0, 0, stacked_queries(qi + 1))
        write_out()

    @pl.when(qi + 1 == nq)
    def _():
        write_out()


def _df_prompt(q, k, v, bias, lams, subln, lam_init, batch, seq, t):
    nq = seq // t
    w = DF_HEADS * HEAD_DIM
    vec64 = pl.BlockSpec((1, DF_HALF), lambda b, h, i: (0, 0))
    return pl.pallas_call(
        functools.partial(_df_prompt_kernel, t=t, lam_init=lam_init),
        grid=(batch, N_HEADS // DF_HEADS, nq),
        in_specs=[pl.BlockSpec((seq, w), lambda b, h, i: (b, h)),
                  pl.BlockSpec((seq, w), lambda b, h, i: (b, h)),
                  pl.BlockSpec((seq, w), lambda b, h, i: (b, h)),
                  pl.BlockSpec((DF_HEADS, 2, t, t), lambda b, h, i: (h, 0, 0, 0)),
                  vec64, vec64, vec64, vec64,
                  pl.BlockSpec((1, HEAD_DIM), lambda b, h, i: (0, 0))],
        out_specs=pl.BlockSpec((t, w), lambda b, h, i: (b * nq + i, h)),
        out_shape=jax.ShapeDtypeStruct((batch * seq, WIDTH), BF16),
        scratch_shapes=[pltpu.VMEM((DF_HEADS, HEAD_DIM + ONES_ROWS, seq), BF16),
                        pltpu.VMEM((DF_HEADS, 1, 2 * t), F32), pltpu.VMEM((DF_HEADS, 1, 2 * t), F32),
                        pltpu.VMEM((DF_HEADS, HEAD_DIM, 2 * t), F32),
                        pltpu.VMEM((2, DF_HEADS, t, 2 * t), F32)],
        compiler_params=_cparams(("parallel", "parallel", "arbitrary")),
        name="df_prompt",
    )(q, k, v, bias, *lams, subln)


def _score_matrix(q_ref, keep_ref, seq_in_block, n_new, halves):
    row = lax.broadcasted_iota(jnp.int32, (QROWS, QCOLS), 0)
    lane = lax.broadcasted_iota(jnp.int32, (QROWS, QCOLS), 1)
    pick = (lane & (n_new - 1)) + seq_in_block * n_new == row
    if halves == 1:
        pick = pick & ((lane & n_new) == 0)
    spread = lax.dot_general(q_ref[...], jnp.where(pick, 1.0, 0.0).astype(BF16), (((0,), (0,)), ((), ())),
                             preferred_element_type=F32)
    return (spread * keep_ref[...]).astype(BF16)


def _score_pattern(n_new, halves):
    feat = np.arange(WIDTH)[:, None]
    col = np.arange(QCOLS)[None, :]
    keep = feat // HEAD_DIM == col // (2 * n_new)
    if halves == 2:
        keep = keep & ((feat // DF_HALF) % 2 == (col // n_new) % 2)
    return jnp.asarray(keep.astype(np.float32))


def _sample_attn_kernel(pt_ref, qa_ref, qb_ref, keep_sb_ref, keep_df_ref, kan_ref, van_ref, kbn_ref, vbn_ref,
                        blast_ref, bnew_ref, lq1, lk1, lq2, lk2, g_ref, *rest,
                        n_seq, n_pages, n_new, layer, lam_init):
    g = PAGES_PER_STEP
    c_sbk, c_dfk, c_sbv, c_dfv, oa_ref, ob_ref = rest[:6]
    zdf_sc, psb_sc, pdf_sc, accsb_sc, accdf_sc, qsb_sc, qdf_sc, ring, extra_buf, sems, extra_sem, extra_n = rest[6:]
    b = pl.program_id(0)
    s = pl.program_id(1)
    n_kv = n_pages // g
    n_steps = 2 * n_kv
    past = n_pages * PAGE_SIZE
    chunk = g * PAGE_SIZE
    hot = SB_HOT_PAGES * PAGE_SIZE
    group = g + SB_HOT_PAGES

    def page_copy(cache, page, block, slot):
        return pltpu.make_async_copy(cache.at[layer, page], ring.at[block], sems.at[slot])

    def group_copies(seq_i, step, slot):
        k = jnp.where(step < n_kv, step, step - n_kv)
        first = (n_kv - 1 - k) * g
        df = [(seq_i * n_pages + first + i, slot * group + i) for i in range(g)]
        sb = [(seq_i * n_pages + n_pages - SB_HOT_PAGES + i, slot * group + g + i) for i in range(SB_HOT_PAGES)]
        return df, sb, k == 0

    def start_group(seq_i, step, slot):
        df, sb, rides = group_copies(seq_i, step, slot)
        for in_phase, c_df, c_sb in ((step < n_kv, c_dfk, c_sbk), (step >= n_kv, c_dfv, c_sbv)):
            @pl.when(in_phase)
            def _():
                for idx, block in df:
                    page_copy(c_df, pt_ref[idx], block, slot).start()

            @pl.when(jnp.logical_and(in_phase, rides))
            def _():
                for idx, block in sb:
                    page_copy(c_sb, pt_ref[idx], block, slot).start()

    gidx = b * n_steps + s
    slot = lax.rem(gidx, RING_DEPTH)
    k_step = jnp.where(s < n_kv, s, s - n_kv)
    first_page = (n_kv - 1 - k_step) * g

    @pl.when(gidx == 0)
    def _():
        for d in range(RING_DEPTH - 1):
            start_group(jnp.int32(0), jnp.int32(d), jnp.int32(d))

    ahead = s + (RING_DEPTH - 1)
    wraps = ahead >= n_steps
    seq_ahead = jnp.where(wraps, b + 1, b)

    @pl.when(seq_ahead < n_seq)
    def _():
        start_group(seq_ahead, jnp.where(wraps, ahead - n_steps, ahead), lax.rem(gidx + (RING_DEPTH - 1), RING_DEPTH))

    for i in range(g):
        page_copy(c_dfk, 0, slot * group + i, slot).wait()

    @pl.when(k_step == 0)
    def _():
        for i in range(SB_HOT_PAGES):
            page_copy(c_sbk, 0, slot * group + g + i, slot).wait()

    def page_rows_bf16(ref):
        return _tokens_by_width(ref, PAGE_SIZE).astype(BF16)

    def pages_bf16(first, count):
        return jnp.concatenate([page_rows_bf16(ring.at[slot * group + first + i]) for i in range(count)], axis=0)

    def extra_page(cache, page):
        cp = pltpu.make_async_copy(cache.at[layer, pt_ref[b * n_pages + page]], extra_buf, extra_sem.at[0])
        cp.start()
        cp.wait()
        return page_rows_bf16(extra_buf)

    def new_tokens(ref):
        return _tokens_by_width(ref, n_new)

    def _queries():
        seq_in_block = lax.rem(b, QROWS // n_new)
        qsb_sc[...] = _score_matrix(qa_ref, keep_sb_ref, seq_in_block, n_new, 1)
        qdf_sc[...] = _score_matrix(qb_ref, keep_df_ref, seq_in_block, n_new, 2)

    def _scores():
        start = pl.multiple_of(first_page * PAGE_SIZE, chunk)
        zdf_sc[pl.ds(start, chunk), :] = jnp.dot(pages_bf16(0, g), qdf_sc[...], preferred_element_type=F32)

    def pad_rows_bf16(x):
        return jnp.concatenate([x, jnp.zeros_like(x)], axis=0).astype(BF16)

    lane_n = lax.broadcasted_iota(jnp.int32, (n_new, QCOLS), 1)
    key_n = lax.broadcasted_iota(jnp.int32, (n_new, QCOLS), 0)
    qpos_n = lane_n & (n_new - 1)

    def sb_weights(z, carry, upper):
        sp = _softplus2(z)
        suffix = jnp.dot(upper, sp.astype(BF16), preferred_element_type=F32)
        return jnp.exp2(z - sp - suffix - carry).astype(BF16), carry + jnp.sum(sp, axis=0, keepdims=True)

    def unspent(carry):
        real = (lax.broadcasted_iota(jnp.int32, (1, QCOLS), 1) & n_new) == 0
        return jnp.min(jnp.where(real, carry, SB_DONE_BITS))

    def _sb_weights():
        zs_new = jnp.dot(pad_rows_bf16(new_tokens(kan_ref)), qsb_sc[...], preferred_element_type=F32)[:n_new]
        strict = key_n < qpos_n
        sp_new = jnp.where(strict, _softplus2(zs_new), 0.0)
        carry = jnp.zeros((1, QCOLS), F32)
        suffix_rows = [None] * n_new
        for i in range(n_new - 1, -1, -1):
            suffix_rows[i] = carry
            carry = carry + sp_new[i:i + 1]
        suffix_new = jnp.concatenate(suffix_rows, axis=0)
        a_new = jnp.exp2(jnp.where(strict, zs_new - sp_new - suffix_new, NEG_BIG))
        accsb_sc[...] = lax.dot_general(pad_rows_bf16(a_new), pad_rows_bf16(new_tokens(van_ref)),
                                        (((0,), (0,)), ((), ())), preferred_element_type=F32)
        rr = lax.broadcasted_iota(jnp.int32, (hot, hot), 0)
        cc = lax.broadcasted_iota(jnp.int32, (hot, hot), 1)
        upper = jnp.where(cc > rr, 1.0, 0.0).astype(BF16)
        z_hot = jnp.dot(pages_bf16(g, SB_HOT_PAGES), qsb_sc[...], preferred_element_type=F32)
        a_hot, carry = sb_weights(z_hot, carry, upper)
        psb_sc[past - hot:past, :] = a_hot

        def more(state):
            page, _, lowest = state
            return jnp.logical_and(page >= 0, lowest < SB_DONE_BITS)

        def older(state):
            page, carry, _ = state
            z = jnp.dot(extra_page(c_sbk, page), qsb_sc[...], preferred_element_type=F32)
            a, carry = sb_weights(z, carry, upper[:PAGE_SIZE, :PAGE_SIZE])
            psb_sc[pl.ds(pl.multiple_of(page * PAGE_SIZE, PAGE_SIZE), PAGE_SIZE), :] = a
            return page - 1, carry, unspent(carry)

        oldest_hot = n_pages - SB_HOT_PAGES
        page, _, _ = lax.while_loop(more, older, (jnp.int32(oldest_hot - 1), carry, unspent(carry)))
        extra_n[0] = oldest_hot - 1 - page

    def _df_weights():
        z_new = jnp.dot(pad_rows_bf16(new_tokens(kbn_ref)), qdf_sc[...], preferred_element_type=F32)[:n_new]
        z_new = z_new + bnew_ref[...]
        z_new = jnp.where(key_n <= qpos_n, z_new, NEG_BIG)
        last0 = past - PAGE_SIZE
        zdf_sc[last0:past, :] = zdf_sc[last0:past, :] + blast_ref[...]
        zp = zdf_sc[...]
        mx = jnp.maximum(jnp.max(zp, axis=0, keepdims=True), jnp.max(z_new, axis=0, keepdims=True))
        e_new = jnp.exp(z_new - mx)
        ep = jnp.exp(zp - mx)
        inv = 1.0 / (jnp.sum(ep, axis=0, keepdims=True) + jnp.sum(e_new, axis=0, keepdims=True))
        pdf_sc[...] = (ep * inv).astype(BF16)
        accdf_sc[...] = lax.dot_general(pad_rows_bf16(e_new * inv), pad_rows_bf16(new_tokens(vbn_ref)),
                                        (((0,), (0,)), ((), ())), preferred_element_type=F32)

    def _values():
        start = pl.multiple_of(first_page * PAGE_SIZE, chunk)
        accdf_sc[...] += lax.dot_general(pdf_sc[pl.ds(start, chunk), :], pages_bf16(0, g),
                                         (((0,), (0,)), ((), ())), preferred_element_type=F32)

    def _sb_values():
        accsb_sc[...] += lax.dot_general(psb_sc[past - hot:past, :], pages_bf16(g, SB_HOT_PAGES),
                                         (((0,), (0,)), ((), ())), preferred_element_type=F32)

        def older(i, carry):
            page = n_pages - SB_HOT_PAGES - 1 - i
            rows = pl.ds(pl.multiple_of(page * PAGE_SIZE, PAGE_SIZE), PAGE_SIZE)
            accsb_sc[...] += lax.dot_general(psb_sc[rows, :], extra_page(c_sbv, page),
                                             (((0,), (0,)), ((), ())), preferred_element_type=F32)
            return carry
        lax.fori_loop(0, extra_n[0], older, 0)

    def _finish():
        lam = _lambda(lq1[...], lk1[...], lq2[...], lk2[...], lam_init)
        for h in range(N_HEADS):
            cols = slice(h * HEAD_DIM, (h + 1) * HEAD_DIM)
            oa_ref[:, cols] = accsb_sc[h * 2 * n_new:h * 2 * n_new + n_new, cols]
            o1 = accdf_sc[h * 2 * n_new:h * 2 * n_new + n_new, cols]
            o2 = accdf_sc[h * 2 * n_new + n_new:(h + 1) * 2 * n_new, cols]
            o = o1 - lam * o2
            ms = jnp.mean(o * o, axis=-1, keepdims=True)
            ob_ref[:, cols] = o * lax.rsqrt(ms + EPS) * g_ref[...] * (1.0 - lam_init)

    for k in range(n_kv):
        @pl.when(s == k)
        def _():
            if k == 0:
                _queries()
            _scores()
            if k == 0:
                _sb_weights()
            if k == n_kv - 1:
                _df_weights()

        @pl.when(s == n_kv + k)
        def _():
            _values()
            if k == 0:
                _sb_values()
            if k == n_kv - 1:
                _finish()


def _sample_attn(page_table, qsb, qdf, ka_n, va_n, kb_n, vb_n, blast, bnew, lams, subln,
                 c_sbk, c_sbv, c_dfk, c_dfv, layer, lam_init):
    n_seq, n_pages = page_table.shape
    n_new = ka_n.shape[0] // (n_seq * N_HEADS)
    assert 2 * N_HEADS * n_new == QCOLS
    g = PAGES_PER_STEP
    n_kv = n_pages // g
    past = n_pages * PAGE_SIZE

    assert n_pages % g == 0 and RING_DEPTH - 1 <= 2 * n_kv and SB_HOT_PAGES <= n_pages
    vec64 = pl.BlockSpec((1, DF_HALF), lambda b, s, pt: (0, 0))
    new_spec = pl.BlockSpec((n_new * N_HEADS, HEAD_DIM), lambda b, s, pt: (b, 0))
    out_spec = pl.BlockSpec((n_new, WIDTH), lambda b, s, pt: (b, 0))
    q_spec = pl.BlockSpec((QROWS, WIDTH), lambda b, s, pt: (b // (QROWS // n_new), 0))
    keep_spec = pl.BlockSpec((WIDTH, QCOLS), lambda b, s, pt: (0, 0))
    in_specs = [q_spec, q_spec, keep_spec, keep_spec,
                new_spec, new_spec, new_spec, new_spec,
                pl.BlockSpec((PAGE_SIZE, QCOLS), lambda b, s, pt: (0, 0)),
                pl.BlockSpec((n_new, QCOLS), lambda b, s, pt: (0, 0)),
                vec64, vec64, vec64, vec64,
                pl.BlockSpec((1, HEAD_DIM), lambda b, s, pt: (0, 0))]
    in_specs += [pl.BlockSpec(memory_space=pl.ANY)] * 4
    grid_spec = pltpu.PrefetchScalarGridSpec(
        num_scalar_prefetch=1,
        grid=(n_seq, 2 * n_kv),
        in_specs=in_specs,
        out_specs=[out_spec, out_spec],
        scratch_shapes=[pltpu.VMEM((past, QCOLS), F32),
                        pltpu.VMEM((past, QCOLS), BF16), pltpu.VMEM((past, QCOLS), BF16),
                        pltpu.VMEM((QCOLS, WIDTH), F32), pltpu.VMEM((QCOLS, WIDTH), F32),
                        pltpu.VMEM((WIDTH, QCOLS), BF16), pltpu.VMEM((WIDTH, QCOLS), BF16),
                        pltpu.VMEM((RING_DEPTH * (g + SB_HOT_PAGES), PAGE_SIZE * N_HEADS, HEAD_DIM), F32),
                        pltpu.VMEM((PAGE_SIZE * N_HEADS, HEAD_DIM), F32),
                        pltpu.SemaphoreType.DMA((RING_DEPTH,)), pltpu.SemaphoreType.DMA((1,)),
                        pltpu.SMEM((1,), jnp.int32)])
    return pl.pallas_call(
        functools.partial(_sample_attn_kernel, n_seq=n_seq, n_pages=n_pages, n_new=n_new, layer=layer,
                          lam_init=lam_init),
        grid_spec=grid_spec,
        out_shape=[jax.ShapeDtypeStruct((n_seq * n_new, WIDTH), F32)] * 2,
        compiler_params=_cparams(("arbitrary", "arbitrary")),
        name="sample_attn",
    )(page_table.reshape(-1), qsb, qdf, _score_pattern(n_new, 1), _score_pattern(n_new, 2),
      ka_n, va_n, kb_n, vb_n, blast, bnew, *lams, subln, c_sbk, c_dfk, c_sbv, c_dfv)


def _merge_kernel(x_ref, oa_ref, ob_ref, ga_ref, gb_ref, wsb_ref, wdf_ref, wout_ref, g2_ref, x1_ref, hn_ref):
    ya = jnp.dot(oa_ref[...].astype(BF16), wsb_ref[...], preferred_element_type=F32)
    yb = jnp.dot(ob_ref[...].astype(BF16), wdf_ref[...], preferred_element_type=F32)
    m = jax.nn.sigmoid(ga_ref[...].astype(F32)) * ya + jax.nn.sigmoid(gb_ref[...].astype(F32)) * yb
    x1 = x_ref[...] + jnp.dot(m.astype(BF16), wout_ref[...], preferred_element_type=F32)
    x1_ref[...] = x1
    ms = jnp.mean(x1 * x1, axis=-1, keepdims=True)
    hn_ref[...] = (x1 * lax.rsqrt(ms + EPS) * g2_ref[...]).astype(BF16)


def _merge(x, oa, ob, gates, wsb, wdf, wout, norm2, tm):
    m, d = x.shape
    const = lambda shape: pl.BlockSpec(shape, lambda i: (0, 0), pipeline_mode=pl.Buffered(1))
    return pl.pallas_call(
        _merge_kernel,
        grid=(m // tm,),
        in_specs=[pl.BlockSpec((tm, d), lambda i: (i, 0)),
                  pl.BlockSpec((tm, WIDTH), lambda i: (i, 0)),
                  pl.BlockSpec((tm, WIDTH), lambda i: (i, 0)),
                  pl.BlockSpec((tm, d), lambda i: (i, 0)),
                  pl.BlockSpec((tm, d), lambda i: (i, 1)),
                  const((WIDTH, d)), const((WIDTH, d)), const((d, d)),
                  pl.BlockSpec((1, d), lambda i: (0, 0))],
        out_specs=[pl.BlockSpec((tm, d), lambda i: (i, 0)), pl.BlockSpec((tm, d), lambda i: (i, 0))],
        out_shape=[jax.ShapeDtypeStruct((m, d), F32), jax.ShapeDtypeStruct((m, d), BF16)],
        compiler_params=_cparams(("parallel",)),
        name="merge",
    )(x, oa, ob, gates, gates, wsb, wdf, wout, norm2.reshape(1, d))


def _mlp_kernel(x1_ref, hn_ref, wup_ref, wdn_ref, o_ref):
    f = pl.program_id(1)

    @pl.when(f == 0)
    def _():
        o_ref[...] = x1_ref[...]

    u = jnp.maximum(jnp.dot(hn_ref[...], wup_ref[...], preferred_element_type=F32), 0.0)
    o_ref[...] += jnp.dot((u * u).astype(BF16), wdn_ref[...], preferred_element_type=F32)


def _mlp(x1, hn, wup, wdn, tm, tf):
    m, d = x1.shape
    dff = wup.shape[1]
    return pl.pallas_call(
        _mlp_kernel,
        grid=(m // tm, dff // tf),
        in_specs=[pl.BlockSpec((tm, d), lambda i, f: (i, 0)),
                  pl.BlockSpec((tm, d), lambda i, f: (i, 0)),
                  pl.BlockSpec((d, tf), lambda i, f: (0, f)),
                  pl.BlockSpec((tf, d), lambda i, f: (f, 0))],
        out_specs=pl.BlockSpec((tm, d), lambda i, f: (i, 0)),
        out_shape=jax.ShapeDtypeStruct((m, d), F32),
        compiler_params=_cparams(("parallel", "arbitrary")),
        name="mlp",
    )(x1, hn, wup, wdn)


def _project_group(x, norm1, w_in, q_norm, k_norm, tm):
    proj = functools.partial(_proj, _rmsnorm_bf16(x, norm1, min(tm, 512)), w_in, tm=tm)
    (qa,) = proj(COL_QA, 1, scale=SB_SCALE * LOG2E)
    ka, ka16 = proj(COL_KA, 1, want_f32=True)
    va, va16 = proj(COL_VA, 1, want_f32=True)
    (qb,) = proj(COL_QB, 1, qknorm=True, gain=q_norm, scale=DF_SCALE)
    kb, kb16 = proj(COL_KB, 1, qknorm=True, gain=k_norm, want_f32=True)
    vb, vb16 = proj(COL_VB, 1, want_f32=True)
    (gates,) = proj(COL_GA, 4)
    return dict(qa=qa, ka=ka, ka16=ka16, va=va, va16=va16, qb=qb, kb=kb, kb16=kb16, vb=vb, vb16=vb16, gates=gates)


def kernel(x_prompt, x_sample, cache_sb_k, cache_sb_v, cache_df_k, cache_df_v, page_table, rel_bias, norm1, w_in, q_norm, k_norm, lambda_q1, lambda_k1, lambda_q2, lambda_k2, subln, w_branch_sb, w_branch_df, w_out, norm2, w_up, w_down):
    depth = norm1.shape[0]
    batch, seq, d = x_prompt.shape
    n_seq, n_new, _ = x_sample.shape
    n_pages = page_table.shape[1]
    past = n_pages * PAGE_SIZE
    xp = x_prompt.reshape(batch * seq, d)
    xs = x_sample.reshape(n_seq * n_new, d)
    bias_p = _bias_prompt(rel_bias, ATT_T)
    bias_last, bias_new = _bias_sample(rel_bias, past, n_new)
    n_pool = cache_sb_k.shape[1]
    leaves = [[] for _ in range(8)]
    for l in range(depth):
        lam_init = 0.8 - 0.6 * math.exp(-0.3 * l)
        w_in16 = w_in[l]
        wsb16 = w_branch_sb[l].astype(BF16)
        wdf16 = w_branch_df[l].astype(BF16)
        wout16 = w_out[l].astype(BF16)
        wup16 = w_up[l].astype(BF16)
        wdn16 = w_down[l].astype(BF16)
        lams = [v[l].reshape(1, DF_HALF).astype(F32) for v in (lambda_q1, lambda_k1, lambda_q2, lambda_k2)]
        sub = subln[l].reshape(1, HEAD_DIM).astype(F32)

        p = _project_group(xp, norm1[l], w_in16, q_norm[l], k_norm[l], 1024)
        oa = _sb_prompt(p["qa"], p["ka16"], p["va16"], batch, seq, ATT_T)
        ob = _df_prompt(p["qb"], p["kb16"], p["vb16"], bias_p, lams, sub, lam_init, batch, seq, ATT_T)
        x1, hn = _merge(xp, oa, ob, p["gates"], wsb16, wdf16, wout16, norm2[l], 256)
        xp = _mlp(x1, hn, wup16, wdn16, 512, 1024)
        for i, name in enumerate(("ka", "va", "kb", "vb")):
            leaves[i].append(p[name].reshape(batch, seq, N_HEADS, HEAD_DIM))

        s = _project_group(xs, norm1[l], w_in16, q_norm[l], k_norm[l], 1024)
        shape4 = (depth, n_pool, PAGE_SIZE * N_HEADS, HEAD_DIM)
        oa, ob = _sample_attn(page_table, s["qa"], s["qb"], s["ka"], s["va"], s["kb"], s["vb"], bias_last, bias_new,
                              lams, sub, cache_sb_k.reshape(shape4), cache_sb_v.reshape(shape4),
                              cache_df_k.reshape(shape4), cache_df_v.reshape(shape4), l, lam_init)
        x1, hn = _merge(xs, oa, ob, s["gates"], wsb16, wdf16, wout16, norm2[l], 256)
        xs = _mlp(x1, hn, wup16, wdn16, 512, 1024)
        for i, name in enumerate(("ka", "va", "kb", "vb")):
            leaves[4 + i].append(s[name].reshape(n_seq, n_new, N_HEADS, HEAD_DIM))

    return (xp.reshape(batch, seq, d), xs.reshape(n_seq, n_new, d)) + tuple(jnp.stack(v) for v in leaves)
```

```python
import functools
import math

import numpy as np
import jax
import jax.numpy as jnp
from jax import lax
from jax.experimental import pallas as pl
from jax.experimental.pallas import tpu as pltpu

F32 = jnp.float32
BF16 = jnp.bfloat16

D_MODEL = 2048
N_HEADS = 8
HEAD_DIM = 128
DF_HALF = 64
WIDTH = N_HEADS * HEAD_DIM
SB_SCALE = 1.0 / math.sqrt(HEAD_DIM)
DF_SCALE = 1.0 / math.sqrt(DF_HALF)
D_FF = 4 * D_MODEL
N_BUCKETS = 32
MAX_EXACT = N_BUCKETS // 2
MAX_DISTANCE = 128
EPS = 1e-6
PAGE_SIZE = 128
NEG_BIG = -1e30

COL_QA, COL_KA, COL_VA, COL_QB, COL_KB, COL_VB, COL_GA, COL_GB = 0, 1, 2, 3, 4, 5, 6, 8

ATT_T = 512
SB_HEADS = 4
DF_HEADS = 2
ATT_UNIT = 256
SB_DONE_BITS = 160.0
ONES_ROWS = 16
LOG2E = 1.4426950408889634
PAGES_PER_STEP = 16
SB_HOT_PAGES = 2
RING_DEPTH = 4
QCOLS = 128
QROWS = 16

VMEM_LIMIT = 56 * 1024 * 1024


def _cparams(sem):
    return pltpu.CompilerParams(dimension_semantics=sem, vmem_limit_bytes=VMEM_LIMIT)


def _tokens_by_width(ref, n_tok):
    return jnp.concatenate([ref[pl.ds(h, n_tok, stride=N_HEADS), :] for h in range(N_HEADS)], axis=1)


def _softplus2(z):
    return jnp.maximum(z, 0.0) + jnp.log2(1.0 + jnp.exp2(-jnp.abs(z)))


def _emit_skewed(n_units, stages):
    state = [None] * n_units
    for tick in range(n_units + len(stages) - 1):
        for si in range(len(stages) - 1, -1, -1):
            u = tick - si
            if 0 <= u < n_units:
                state[u] = stages[si](u, state[u])
    return state


def _rmsnorm_kernel(x_ref, g_ref, o_ref):
    x = x_ref[...]
    ms = jnp.mean(x * x, axis=-1, keepdims=True)
    o_ref[...] = (x * lax.rsqrt(ms + EPS) * g_ref[...]).astype(o_ref.dtype)


def _rmsnorm_bf16(x, g, tm):
    m, d = x.shape
    return pl.pallas_call(
        _rmsnorm_kernel,
        grid=(m // tm,),
        in_specs=[pl.BlockSpec((tm, d), lambda i: (i, 0)),
                  pl.BlockSpec((1, d), lambda i: (0, 0))],
        out_specs=pl.BlockSpec((tm, d), lambda i: (i, 0)),
        out_shape=jax.ShapeDtypeStruct((m, d), BF16),
        compiler_params=_cparams(("parallel",)),
        name="rmsnorm_bf16",
    )(x, g.reshape(1, d))


def _proj_kernel(*refs, qknorm, scale, want_f32, want_bf16):
    h_ref, w_ref = refs[0], refs[1]
    pos = 2
    if qknorm:
        gain_ref, pmat_ref = refs[2], refs[3]
        pos = 4
    outs = refs[pos:]
    y = jnp.dot(h_ref[...], w_ref[...].astype(BF16), preferred_element_type=F32)
    if qknorm:
        tn = y.shape[1]
        pieces = []
        for c in range(tn // 256):
            yb = y[:, c * 256:(c + 1) * 256]
            ms = jnp.dot((yb * yb).astype(BF16), pmat_ref[...], preferred_element_type=F32)
            pieces.append(yb * lax.rsqrt(ms + EPS))
        y = jnp.concatenate(pieces, axis=1) * gain_ref[...]
    k = 0
    if want_f32:
        for hd in range(N_HEADS):
            outs[k][pl.ds(hd, y.shape[0], stride=N_HEADS), :] = y[:, hd * HEAD_DIM:(hd + 1) * HEAD_DIM]
        k += 1
    if want_bf16:
        outs[k][...] = (y * scale).astype(BF16) if scale != 1.0 else y.astype(BF16)


def _proj(h, w, col0, ncol, *, tm, qknorm=False, gain=None, scale=1.0, want_f32=False, want_bf16=True):
    m, kdim = h.shape
    tn = WIDTH
    in_specs = [pl.BlockSpec((tm, kdim), lambda n, i: (i, 0)),
                pl.BlockSpec((kdim, tn), lambda n, i: (0, col0 + n))]
    args = [h, w]
    if qknorm:
        pmat = np.kron(np.eye(256 // DF_HALF), np.full((DF_HALF, DF_HALF), 1.0 / DF_HALF)).astype(np.float32)
        in_specs += [pl.BlockSpec((1, tn), lambda n, i: (0, 0)),
                     pl.BlockSpec((256, 256), lambda n, i: (0, 0))]
        args += [jnp.tile(gain.astype(F32), tn // DF_HALF).reshape(1, tn), jnp.asarray(pmat, dtype=BF16)]
    out_specs, out_shape = [], []
    if want_f32:
        assert ncol == 1
        out_specs.append(pl.BlockSpec((tm * N_HEADS, HEAD_DIM), lambda n, i: (i, 0)))
        out_shape.append(jax.ShapeDtypeStruct((m * N_HEADS, HEAD_DIM), F32))
    if want_bf16:
        out_specs.append(pl.BlockSpec((tm, tn), lambda n, i: (i, n)))
        out_shape.append(jax.ShapeDtypeStruct((m, ncol * tn), BF16))
    res = pl.pallas_call(
        functools.partial(_proj_kernel, qknorm=qknorm, scale=scale, want_f32=want_f32, want_bf16=want_bf16),
        grid=(ncol, m // tm),
        in_specs=in_specs,
        out_specs=out_specs,
        out_shape=out_shape,
        compiler_params=_cparams(("parallel", "parallel")),
        name="in_proj",
    )(*args)
    return res


def _bucket(rel):
    n = jnp.maximum(rel, 0)
    nf = jnp.maximum(n, 1).astype(F32)
    large = MAX_EXACT + (jnp.log(nf / MAX_EXACT) / math.log(MAX_DISTANCE / MAX_EXACT)
                         * (N_BUCKETS - MAX_EXACT)).astype(jnp.int32)
    large = jnp.minimum(large, N_BUCKETS - 1)
    return jnp.where(n < MAX_EXACT, n, large)


def _bias_prompt_kernel(rb_ref, o_ref, *, t):
    h = pl.program_id(0)
    far = rb_ref[(N_BUCKETS - 1) * N_HEADS + h]

    def shifted_bias(rel):
        b = _bucket(rel)
        acc = jnp.zeros(rel.shape, F32)
        for k in range(N_BUCKETS - 1):
            acc = jnp.where(b == k, rb_ref[k * N_HEADS + h] - far, acc)
        return acc

    keys = lax.broadcasted_iota(jnp.int32, (t, t), 0)
    qrys = lax.broadcasted_iota(jnp.int32, (t, t), 1)
    o_ref[0, 0] = jnp.where(keys <= qrys, shifted_bias(qrys - keys), NEG_BIG)
    c = MAX_DISTANCE
    ki = lax.broadcasted_iota(jnp.int32, (c, c), 0)
    qj = lax.broadcasted_iota(jnp.int32, (c, c), 1)
    corner = jnp.concatenate([shifted_bias(qj - ki + c), jnp.zeros((c, t - c), F32)], axis=1)
    o_ref[0, 1] = jnp.concatenate([jnp.zeros((t - c, t), F32), corner], axis=0)


def _bias_prompt(rel_bias, t):
    return pl.pallas_call(
        functools.partial(_bias_prompt_kernel, t=t),
        grid=(N_HEADS,),
        in_specs=[pl.BlockSpec(memory_space=pltpu.SMEM)],
        out_specs=pl.BlockSpec((1, 2, t, t), lambda h: (h, 0, 0, 0)),
        out_shape=jax.ShapeDtypeStruct((N_HEADS, 2, t, t), F32),
        compiler_params=_cparams(("parallel",)),
        name="bias_prompt",
    )(rel_bias.reshape(-1))


def _bias_sample_kernel(rb_ref, last_ref, new_ref, *, past, n_new):
    def tile(rows, key0):
        lane = lax.broadcasted_iota(jnp.int32, (rows, QCOLS), 1)
        key = lax.broadcasted_iota(jnp.int32, (rows, QCOLS), 0) + key0
        head = lax.shift_right_logical(lane, (2 * n_new).bit_length() - 1)
        rel = past + (lane & (n_new - 1)) - key
        b = _bucket(rel)
        acc = jnp.zeros((rows, QCOLS), F32)
        for hh in range(N_HEADS):
            far = rb_ref[(N_BUCKETS - 1) * N_HEADS + hh]
            for k in range(N_BUCKETS - 1):
                acc = jnp.where((b == k) & (head == hh), rb_ref[k * N_HEADS + hh] - far, acc)
        return acc
    last_ref[...] = tile(PAGE_SIZE, past - PAGE_SIZE)
    new_ref[...] = tile(n_new, past)


def _bias_sample(rel_bias, past, n_new):
    return pl.pallas_call(
        functools.partial(_bias_sample_kernel, past=past, n_new=n_new),
        in_specs=[pl.BlockSpec(memory_space=pltpu.SMEM)],
        out_specs=[pl.BlockSpec((PAGE_SIZE, QCOLS), lambda: (0, 0)),
                   pl.BlockSpec((n_new, QCOLS), lambda: (0, 0))],
        out_shape=[jax.ShapeDtypeStruct((PAGE_SIZE, QCOLS), F32),
                   jax.ShapeDtypeStruct((n_new, QCOLS), F32)],
        name="bias_sample",
    )(rel_bias.reshape(-1))


def _lambda(lq1, lk1, lq2, lk2, lam_init):
    s1 = jnp.sum(lq1 * lk1, axis=-1, keepdims=True)
    s2 = jnp.sum(lq2 * lk2, axis=-1, keepdims=True)
    return jnp.exp(s1) - jnp.exp(s2) + lam_init


def _store_transposed(dst_ref, src_ref, t, row0=0):
    for ci in range(src_ref.shape[0] // t):
        chunk = src_ref[ci * t:(ci + 1) * t, :].astype(F32)
        dst_ref[row0:row0 + HEAD_DIM, ci * t:(ci + 1) * t] = chunk.T.astype(dst_ref.dtype)


def _head_cols(hh):
    return slice(hh * HEAD_DIM, (hh + 1) * HEAD_DIM)


def _sb_prompt_kernel(q_ref, k_ref, v_ref, upper_ref, o_ref, vt_sc, *, t):
    qi = pl.program_id(2)
    heads = range(SB_HEADS)

    @pl.when(qi == 0)
    def _():
        for hh in heads:
            _store_transposed(vt_sc.at[hh], v_ref.at[:, _head_cols(hh)], t)

    nsub = t // ATT_UNIT
    units = [(hh, cb) for hh in heads for cb in range(nsub)]
    qs = [q_ref[cb * ATT_UNIT:(cb + 1) * ATT_UNIT, _head_cols(hh)] for hh, cb in units]
    keys = lax.broadcasted_iota(jnp.int32, (ATT_UNIT, ATT_UNIT), 0)
    qrys = lax.broadcasted_iota(jnp.int32, (ATT_UNIT, ATT_UNIT), 1)
    tri = keys < qrys

    def block(j, carry, masked):
        start = pl.multiple_of(j * t, t)

        def scores(u, _):
            hh, _cb = units[u]
            return lax.dot_general(k_ref[pl.ds(start, t), _head_cols(hh)], qs[u], (((1,), (1,)), ((), ())),
                                   preferred_element_type=F32)

        def softplus(u, z):
            cb = units[u][1]
            c = carry[u][1]
            subs = []
            for kb in range(nsub - 1, -1, -1):
                if masked and kb > cb:
                    continue
                zk = z[kb * ATT_UNIT:(kb + 1) * ATT_UNIT]
                sp = _softplus2(zk)
                diag = masked and kb == cb
                if diag:
                    sp = jnp.where(tri, sp, 0.0)
                subs.append((kb, zk, sp, c, diag))
                c = c + jnp.sum(sp, axis=0, keepdims=True)
            return subs, c

        def suffixes(u, state):
            subs, c = state
            return [sub + (jnp.dot(upper_ref[...], sub[2].astype(BF16), preferred_element_type=F32),)
                    for sub in subs], c

        def weights(u, state):
            subs, c = state
            hh = units[u][0]
            acc = carry[u][0]
            for kb, zk, sp, c_in, diag, suffix in subs:
                log_a = zk - sp - suffix - c_in
                if diag:
                    log_a = jnp.where(tri, log_a, NEG_BIG)
                a = jnp.exp2(log_a).astype(BF16)
                acc = acc + jnp.dot(vt_sc[hh, :, pl.ds(start + kb * ATT_UNIT, ATT_UNIT)], a,
                                    preferred_element_type=F32)
            return acc, c

        return tuple(_emit_skewed(len(units), [scores, softplus, suffixes, weights]))

    carry = tuple((jnp.zeros((HEAD_DIM, ATT_UNIT), F32), jnp.zeros((1, ATT_UNIT), F32)) for _ in units)
    carry = block(qi, carry, True)

    def spent(cr):
        return functools.reduce(jnp.minimum, [jnp.min(c) for _, c in cr])

    def more(state):
        i, _, lowest = state
        return jnp.logical_and(i < qi, lowest < SB_DONE_BITS)

    def step(state):
        i, cr, _ = state
        cr = block(qi - 1 - i, cr, False)
        return i + 1, cr, spent(cr)

    _, carry, _ = lax.while_loop(more, step, (jnp.int32(0), carry, spent(carry)))
    for u, (hh, cb) in enumerate(units):
        o_ref[cb * ATT_UNIT:(cb + 1) * ATT_UNIT, _head_cols(hh)] = carry[u][0].T.astype(o_ref.dtype)


def _sb_prompt(q, k, v, batch, seq, t):
    nq = seq // t
    w = SB_HEADS * HEAD_DIM
    upper = jnp.asarray(np.triu(np.ones((ATT_UNIT, ATT_UNIT), np.float32), 1), dtype=BF16)
    return pl.pallas_call(
        functools.partial(_sb_prompt_kernel, t=t),
        grid=(batch, N_HEADS // SB_HEADS, nq),
        in_specs=[pl.BlockSpec((t, w), lambda b, h, i: (b * nq + i, h)),
                  pl.BlockSpec((seq, w), lambda b, h, i: (b, h)),
                  pl.BlockSpec((seq, w), lambda b, h, i: (b, h)),
                  pl.BlockSpec((ATT_UNIT, ATT_UNIT), lambda b, h, i: (0, 0))],
        out_specs=pl.BlockSpec((t, w), lambda b, h, i: (b * nq + i, h)),
        out_shape=jax.ShapeDtypeStruct((batch * seq, WIDTH), BF16),
        scratch_shapes=[pltpu.VMEM((SB_HEADS, HEAD_DIM, seq), BF16)],
        compiler_params=_cparams(("parallel", "parallel", "arbitrary")),
        name="sb_prompt",
    )(q, k, v, upper)


def _df_prompt_kernel(q_ref, k_ref, v_ref, bias_ref, lq1, lk1, lq2, lk2, g_ref, o_ref,
                      vt_sc, m_sc, l_sc, acc_sc, z_sc, *, t, lam_init):
    qi = pl.program_id(2)
    nq = pl.num_programs(2)
    heads = range(DF_HEADS)

    @pl.when(qi == 0)
    def _():
        for hh in heads:
            _store_transposed(vt_sc.at[hh], v_ref.at[:, _head_cols(hh)], t)
            vt_sc[hh, HEAD_DIM:, :] = jnp.ones((ONES_ROWS, vt_sc.shape[2]), BF16)

    lane = lax.broadcasted_iota(jnp.int32, (t, HEAD_DIM), 1)

    def stacked_queries(block):
        out = []
        for hh in heads:
            q = q_ref[pl.ds(pl.multiple_of(block * t, t), t), _head_cols(hh)]
            zero = jnp.zeros_like(q)
            out.append(jnp.concatenate([jnp.where(lane < DF_HALF, q, zero), jnp.where(lane >= DF_HALF, q, zero)],
                                       axis=0))
        return out

    qs = stacked_queries(qi)
    for hh in heads:
        m_sc[hh] = jnp.full(m_sc.shape[1:], NEG_BIG, F32)
        l_sc[hh] = jnp.zeros(l_sc.shape[1:], F32)
        acc_sc[hh] = jnp.zeros(acc_sc.shape[1:], F32)

    def scores(j, bias_d, slot, qs=qs):
        start = pl.multiple_of(j * t, t)
        for hh in heads:
            z = lax.dot_general(k_ref[pl.ds(start, t), _head_cols(hh)], qs[hh], (((1,), (1,)), ((), ())),
                                preferred_element_type=F32)
            if bias_d is not None:
                bias = bias_ref[hh, bias_d]
                z = z + jnp.concatenate([bias, bias], axis=1)
            z_sc[slot, hh] = z

    def consume(j, slot):
        for hh in heads:
            consume_head(j, z_sc[slot, hh], hh)

    def consume_head(j, z, hh):
        start = pl.multiple_of(j * t, t)
        m_old = m_sc[hh]
        m_new = jnp.maximum(m_old, jnp.max(z, axis=0, keepdims=True))
        p = jnp.exp(z - m_new).astype(BF16)
        alpha = jnp.exp(m_old - m_new)
        pv = jnp.dot(vt_sc[hh, :, pl.ds(start, t)], p, preferred_element_type=F32)
        l_sc[hh] = alpha * l_sc[hh] + pv[HEAD_DIM:HEAD_DIM + 1, :]
        acc_sc[hh] = alpha * acc_sc[hh] + pv[:HEAD_DIM, :]
        m_sc[hh] = m_new

    @pl.when(qi == 0)
    def _():
        scores(qi, 0, 0)

    def pair(p, near_bias):
        j = qi - 2 * p
        scores(j - 1, near_bias, 1)
        consume(j, 0)
        scores(j - 2, None, 0)
        consume(j - 1, 1)

    @pl.when(qi >= 2)
    def _():
        pair(0, 1)

    def far_pair(p, carry):
        pair(p, None)
        return carry
    lax.fori_loop(1, qi // 2, far_pair, 0)

    def odd_tail(near_bias):
        scores(0, near_bias, 1)
        consume(1, 0)
        consume(0, 1)

    @pl.when(qi == 1)
    def _():
        odd_tail(1)

    @pl.when(jnp.logical_and(qi % 2 == 1, qi > 1))
    def _():
        odd_tail(None)

    @pl.when(qi % 2 == 0)
    def _():
        consume(0, 0)

    def write_out():
        lam = _lambda(lq1[...], lk1[...], lq2[...], lk2[...], lam_init)
        for hh in heads:
            o = acc_sc[hh] / l_sc[hh]
            o = (o[:, :t] - lam * o[:, t:]).T
            ms = jnp.mean(o * o, axis=-1, keepdims=True)
            o_ref[:, _head_cols(hh)] = (o * lax.rsqrt(ms + EPS) * g_ref[...] * (1.0 - lam_init)).astype(o_ref.dtype)

    @pl.when(qi + 1 < nq)
    def _():
        scores(qi + 1, 0, 0, stacked_queries(qi + 1))
        write_out()

    @pl.when(qi + 1 == nq)
    def _():
        write_out()


def _df_prompt(q, k, v, bias, lams, subln, lam_init, batch, seq, t):
    nq = seq // t
    w = DF_HEADS * HEAD_DIM
    vec64 = pl.BlockSpec((1, DF_HALF), lambda b, h, i: (0, 0))
    return pl.pallas_call(
        functools.partial(_df_prompt_kernel, t=t, lam_init=lam_init),
        grid=(batch, N_HEADS // DF_HEADS, nq),
        in_specs=[pl.BlockSpec((seq, w), lambda b, h, i: (b, h)),
                  pl.BlockSpec((seq, w), lambda b, h, i: (b, h)),
                  pl.BlockSpec((seq, w), lambda b, h, i: (b, h)),
                  pl.BlockSpec((DF_HEADS, 2, t, t), lambda b, h, i: (h, 0, 0, 0)),
                  vec64, vec64, vec64, vec64,
                  pl.BlockSpec((1, HEAD_DIM), lambda b, h, i: (0, 0))],
        out_specs=pl.BlockSpec((t, w), lambda b, h, i: (b * nq + i, h)),
        out_shape=jax.ShapeDtypeStruct((batch * seq, WIDTH), BF16),
        scratch_shapes=[pltpu.VMEM((DF_HEADS, HEAD_DIM + ONES_ROWS, seq), BF16),
                        pltpu.VMEM((DF_HEADS, 1, 2 * t), F32), pltpu.VMEM((DF_HEADS, 1, 2 * t), F32),
                        pltpu.VMEM((DF_HEADS, HEAD_DIM, 2 * t), F32),
                        pltpu.VMEM((2, DF_HEADS, t, 2 * t), F32)],
        compiler_params=_cparams(("parallel", "parallel", "arbitrary")),
        name="df_prompt",
    )(q, k, v, bias, *lams, subln)


def _score_matrix(q_ref, keep_ref, seq_in_block, n_new, halves):
    row = lax.broadcasted_iota(jnp.int32, (QROWS, QCOLS), 0)
    lane = lax.broadcasted_iota(jnp.int32, (QROWS, QCOLS), 1)
    pick = (lane & (n_new - 1)) + seq_in_block * n_new == row
    if halves == 1:
        pick = pick & ((lane & n_new) == 0)
    spread = lax.dot_general(q_ref[...], jnp.where(pick, 1.0, 0.0).astype(BF16), (((0,), (0,)), ((), ())),
                             preferred_element_type=F32)
    return (spread * keep_ref[...]).astype(BF16)


def _score_pattern(n_new, halves):
    feat = np.arange(WIDTH)[:, None]
    col = np.arange(QCOLS)[None, :]
    keep = feat // HEAD_DIM == col // (2 * n_new)
    if halves == 2:
        keep = keep & ((feat // DF_HALF) % 2 == (col // n_new) % 2)
    return jnp.asarray(keep.astype(np.float32))


def _sample_attn_kernel(pt_ref, qa_ref, qb_ref, keep_sb_ref, keep_df_ref, kan_ref, van_ref, kbn_ref, vbn_ref,
                        blast_ref, bnew_ref, lq1, lk1, lq2, lk2, g_ref, *rest,
                        n_seq, n_pages, n_new, layer, lam_init):
    g = PAGES_PER_STEP
    c_sbk, c_dfk, c_sbv, c_dfv, oa_ref, ob_ref = rest[:6]
    zdf_sc, psb_sc, pdf_sc, accsb_sc, accdf_sc, qsb_sc, qdf_sc, ring, extra_buf, sems, extra_sem, extra_n = rest[6:]
    b = pl.program_id(0)
    s = pl.program_id(1)
    n_kv = n_pages // g
    n_steps = 2 * n_kv
    past = n_pages * PAGE_SIZE
    chunk = g * PAGE_SIZE
    hot = SB_HOT_PAGES * PAGE_SIZE
    group = g + SB_HOT_PAGES

    def page_copy(cache, page, block, slot):
        return pltpu.make_async_copy(cache.at[layer, page], ring.at[block], sems.at[slot])

    def group_copies(seq_i, step, slot):
        k = jnp.where(step < n_kv, step, step - n_kv)
        first = (n_kv - 1 - k) * g
        df = [(seq_i * n_pages + first + i, slot * group + i) for i in range(g)]
        sb = [(seq_i * n_pages + n_pages - SB_HOT_PAGES + i, slot * group + g + i) for i in range(SB_HOT_PAGES)]
        return df, sb, k == 0

    def start_group(seq_i, step, slot):
        df, sb, rides = group_copies(seq_i, step, slot)
        for in_phase, c_df, c_sb in ((step < n_kv, c_dfk, c_sbk), (step >= n_kv, c_dfv, c_sbv)):
            @pl.when(in_phase)
            def _():
                for idx, block in df:
                    page_copy(c_df, pt_ref[idx], block, slot).start()

            @pl.when(jnp.logical_and(in_phase, rides))
            def _():
                for idx, block in sb:
                    page_copy(c_sb, pt_ref[idx], block, slot).start()

    gidx = b * n_steps + s
    slot = lax.rem(gidx, RING_DEPTH)
    k_step = jnp.where(s < n_kv, s, s - n_kv)
    first_page = (n_kv - 1 - k_step) * g

    @pl.when(gidx == 0)
    def _():
        for d in range(RING_DEPTH - 1):
            start_group(jnp.int32(d // n_steps), jnp.int32(d % n_steps), jnp.int32(d))

    seq_ahead, step_ahead = b, s
    for k in range(n_steps):
        seq_off, step_k = divmod(k + RING_DEPTH - 1, n_steps)
        seq_ahead = jnp.where(s == k, b + seq_off, seq_ahead)
        step_ahead = jnp.where(s == k, step_k, step_ahead)

    @pl.when(seq_ahead < n_seq)
    def _():
        start_group(seq_ahead, step_ahead, lax.rem(gidx + (RING_DEPTH - 1), RING_DEPTH))

    for i in range(g):
        page_copy(c_dfk, 0, slot * group + i, slot).wait()

    @pl.when(k_step == 0)
    def _():
        for i in range(SB_HOT_PAGES):
            page_copy(c_sbk, 0, slot * group + g + i, slot).wait()

    def page_rows_bf16(ref):
        return _tokens_by_width(ref, PAGE_SIZE).astype(BF16)

    def pages_bf16(first, count):
        return jnp.concatenate([page_rows_bf16(ring.at[slot * group + first + i]) for i in range(count)], axis=0)

    def extra_page(cache, page):
        cp = pltpu.make_async_copy(cache.at[layer, pt_ref[b * n_pages + page]], extra_buf, extra_sem.at[0])
        cp.start()
        cp.wait()
        return page_rows_bf16(extra_buf)

    def new_tokens(ref):
        return _tokens_by_width(ref, n_new)

    def _queries():
        seq_in_block = lax.rem(b, QROWS // n_new)
        qsb_sc[...] = _score_matrix(qa_ref, keep_sb_ref, seq_in_block, n_new, 1)
        qdf_sc[...] = _score_matrix(qb_ref, keep_df_ref, seq_in_block, n_new, 2)

    def _scores():
        start = pl.multiple_of(first_page * PAGE_SIZE, chunk)
        zdf_sc[pl.ds(start, chunk), :] = jnp.dot(pages_bf16(0, g), qdf_sc[...], preferred_element_type=F32)

    def pad_rows_bf16(x):
        return jnp.concatenate([x, jnp.zeros_like(x)], axis=0).astype(BF16)

    lane_n = lax.broadcasted_iota(jnp.int32, (n_new, QCOLS), 1)
    key_n = lax.broadcasted_iota(jnp.int32, (n_new, QCOLS), 0)
    qpos_n = lane_n & (n_new - 1)

    def sb_weights(z, carry, upper):
        sp = _softplus2(z)
        suffix = jnp.dot(upper, sp.astype(BF16), preferred_element_type=F32)
        return jnp.exp2(z - sp - suffix - carry).astype(BF16), carry + jnp.sum(sp, axis=0, keepdims=True)

    def unspent(carry):
        real = (lax.broadcasted_iota(jnp.int32, (1, QCOLS), 1) & n_new) == 0
        return jnp.min(jnp.where(real, carry, SB_DONE_BITS))

    def _sb_weights():
        zs_new = jnp.dot(pad_rows_bf16(new_tokens(kan_ref)), qsb_sc[...], preferred_element_type=F32)[:n_new]
        strict = key_n < qpos_n
        sp_new = jnp.where(strict, _softplus2(zs_new), 0.0)
        carry = jnp.zeros((1, QCOLS), F32)
        suffix_rows = [None] * n_new
        for i in range(n_new - 1, -1, -1):
            suffix_rows[i] = carry
            carry = carry + sp_new[i:i + 1]
        suffix_new = jnp.concatenate(suffix_rows, axis=0)
        a_new = jnp.exp2(jnp.where(strict, zs_new - sp_new - suffix_new, NEG_BIG))
        accsb_sc[...] = lax.dot_general(pad_rows_bf16(a_new), pad_rows_bf16(new_tokens(van_ref)),
                                        (((0,), (0,)), ((), ())), preferred_element_type=F32)
        rr = lax.broadcasted_iota(jnp.int32, (hot, hot), 0)
        cc = lax.broadcasted_iota(jnp.int32, (hot, hot), 1)
        upper = jnp.where(cc > rr, 1.0, 0.0).astype(BF16)
        z_hot = jnp.dot(pages_bf16(g, SB_HOT_PAGES), qsb_sc[...], preferred_element_type=F32)
        a_hot, carry = sb_weights(z_hot, carry, upper)
        psb_sc[past - hot:past, :] = a_hot

        def more(state):
            page, _, lowest = state
            return jnp.logical_and(page >= 0, lowest < SB_DONE_BITS)

        def older(state):
            page, carry, _ = state
            z = jnp.dot(extra_page(c_sbk, page), qsb_sc[...], preferred_element_type=F32)
            a, carry = sb_weights(z, carry, upper[:PAGE_SIZE, :PAGE_SIZE])
            psb_sc[pl.ds(pl.multiple_of(page * PAGE_SIZE, PAGE_SIZE), PAGE_SIZE), :] = a
            return page - 1, carry, unspent(carry)

        oldest_hot = n_pages - SB_HOT_PAGES
        page, _, _ = lax.while_loop(more, older, (jnp.int32(oldest_hot - 1), carry, unspent(carry)))
        extra_n[0] = oldest_hot - 1 - page

    def _df_weights():
        z_new = jnp.dot(pad_rows_bf16(new_tokens(kbn_ref)), qdf_sc[...], preferred_element_type=F32)[:n_new]
        z_new = z_new + bnew_ref[...]
        z_new = jnp.where(key_n <= qpos_n, z_new, NEG_BIG)
        last0 = past - PAGE_SIZE
        zdf_sc[last0:past, :] = zdf_sc[last0:past, :] + blast_ref[...]
        zp = zdf_sc[...]
        mx = jnp.maximum(jnp.max(zp, axis=0, keepdims=True), jnp.max(z_new, axis=0, keepdims=True))
        e_new = jnp.exp(z_new - mx)
        ep = jnp.exp(zp - mx)
        inv = 1.0 / (jnp.sum(ep, axis=0, keepdims=True) + jnp.sum(e_new, axis=0, keepdims=True))
        pdf_sc[...] = (ep * inv).astype(BF16)
        accdf_sc[...] = lax.dot_general(pad_rows_bf16(e_new * inv), pad_rows_bf16(new_tokens(vbn_ref)),
                                        (((0,), (0,)), ((), ())), preferred_element_type=F32)

    def _values():
        start = pl.multiple_of(first_page * PAGE_SIZE, chunk)
        accdf_sc[...] += lax.dot_general(pdf_sc[pl.ds(start, chunk), :], pages_bf16(0, g),
                                         (((0,), (0,)), ((), ())), preferred_element_type=F32)

    def _sb_values():
        accsb_sc[...] += lax.dot_general(psb_sc[past - hot:past, :], pages_bf16(g, SB_HOT_PAGES),
                                         (((0,), (0,)), ((), ())), preferred_element_type=F32)

        def older(i, carry):
            page = n_pages - SB_HOT_PAGES - 1 - i
            rows = pl.ds(pl.multiple_of(page * PAGE_SIZE, PAGE_SIZE), PAGE_SIZE)
            accsb_sc[...] += lax.dot_general(psb_sc[rows, :], extra_page(c_sbv, page),
                                             (((0,), (0,)), ((), ())), preferred_element_type=F32)
            return carry
        lax.fori_loop(0, extra_n[0], older, 0)

    def _finish():
        lam = _lambda(lq1[...], lk1[...], lq2[...], lk2[...], lam_init)
        for h in range(N_HEADS):
            cols = slice(h * HEAD_DIM, (h + 1) * HEAD_DIM)
            oa_ref[:, cols] = accsb_sc[h * 2 * n_new:h * 2 * n_new + n_new, cols]
            o1 = accdf_sc[h * 2 * n_new:h * 2 * n_new + n_new, cols]
            o2 = accdf_sc[h * 2 * n_new + n_new:(h + 1) * 2 * n_new, cols]
            o = o1 - lam * o2
            ms = jnp.mean(o * o, axis=-1, keepdims=True)
            ob_ref[:, cols] = o * lax.rsqrt(ms + EPS) * g_ref[...] * (1.0 - lam_init)

    for k in range(n_kv):
        @pl.when(s == k)
        def _():
            if k == 0:
                _queries()
            _scores()
            if k == 0:
                _sb_weights()
            if k == n_kv - 1:
                _df_weights()

        @pl.when(s == n_kv + k)
        def _():
            _values()
            if k == 0:
                _sb_values()
            if k == n_kv - 1:
                _finish()


def _sample_attn(page_table, qsb, qdf, ka_n, va_n, kb_n, vb_n, blast, bnew, lams, subln,
                 c_sbk, c_sbv, c_dfk, c_dfv, layer, lam_init):
    n_seq, n_pages = page_table.shape
    n_new = ka_n.shape[0] // (n_seq * N_HEADS)
    assert 2 * N_HEADS * n_new == QCOLS
    g = PAGES_PER_STEP
    n_kv = n_pages // g
    past = n_pages * PAGE_SIZE

    assert n_pages % g == 0 and SB_HOT_PAGES <= n_pages and (RING_DEPTH - 2) // (2 * n_kv) < n_seq
    vec64 = pl.BlockSpec((1, DF_HALF), lambda b, s, pt: (0, 0))
    new_spec = pl.BlockSpec((n_new * N_HEADS, HEAD_DIM), lambda b, s, pt: (b, 0))
    out_spec = pl.BlockSpec((n_new, WIDTH), lambda b, s, pt: (b, 0))
    q_spec = pl.BlockSpec((QROWS, WIDTH), lambda b, s, pt: (b // (QROWS // n_new), 0))
    keep_spec = pl.BlockSpec((WIDTH, QCOLS), lambda b, s, pt: (0, 0))
    in_specs = [q_spec, q_spec, keep_spec, keep_spec,
                new_spec, new_spec, new_spec, new_spec,
                pl.BlockSpec((PAGE_SIZE, QCOLS), lambda b, s, pt: (0, 0)),
                pl.BlockSpec((n_new, QCOLS), lambda b, s, pt: (0, 0)),
                vec64, vec64, vec64, vec64,
                pl.BlockSpec((1, HEAD_DIM), lambda b, s, pt: (0, 0))]
    in_specs += [pl.BlockSpec(memory_space=pl.ANY)] * 4
    grid_spec = pltpu.PrefetchScalarGridSpec(
        num_scalar_prefetch=1,
        grid=(n_seq, 2 * n_kv),
        in_specs=in_specs,
        out_specs=[out_spec, out_spec],
        scratch_shapes=[pltpu.VMEM((past, QCOLS), F32),
                        pltpu.VMEM((past, QCOLS), BF16), pltpu.VMEM((past, QCOLS), BF16),
                        pltpu.VMEM((QCOLS, WIDTH), F32), pltpu.VMEM((QCOLS, WIDTH), F32),
                        pltpu.VMEM((WIDTH, QCOLS), BF16), pltpu.VMEM((WIDTH, QCOLS), BF16),
                        pltpu.VMEM((RING_DEPTH * (g + SB_HOT_PAGES), PAGE_SIZE * N_HEADS, HEAD_DIM), F32),
                        pltpu.VMEM((PAGE_SIZE * N_HEADS, HEAD_DIM), F32),
                        pltpu.SemaphoreType.DMA((RING_DEPTH,)), pltpu.SemaphoreType.DMA((1,)),
                        pltpu.SMEM((1,), jnp.int32)])
    return pl.pallas_call(
        functools.partial(_sample_attn_kernel, n_seq=n_seq, n_pages=n_pages, n_new=n_new, layer=layer,
                          lam_init=lam_init),
        grid_spec=grid_spec,
        out_shape=[jax.ShapeDtypeStruct((n_seq * n_new, WIDTH), F32)] * 2,
        compiler_params=_cparams(("arbitrary", "arbitrary")),
        name="sample_attn",
    )(page_table.reshape(-1), qsb, qdf, _score_pattern(n_new, 1), _score_pattern(n_new, 2),
      ka_n, va_n, kb_n, vb_n, blast, bnew, *lams, subln, c_sbk, c_dfk, c_sbv, c_dfv)


def _merge_kernel(x_ref, oa_ref, ob_ref, ga_ref, gb_ref, wsb_ref, wdf_ref, wout_ref, g2_ref, x1_ref, hn_ref):
    ya = jnp.dot(oa_ref[...].astype(BF16), wsb_ref[...], preferred_element_type=F32)
    yb = jnp.dot(ob_ref[...].astype(BF16), wdf_ref[...], preferred_element_type=F32)
    m = jax.nn.sigmoid(ga_ref[...].astype(F32)) * ya + jax.nn.sigmoid(gb_ref[...].astype(F32)) * yb
    x1 = x_ref[...] + jnp.dot(m.astype(BF16), wout_ref[...], preferred_element_type=F32)
    x1_ref[...] = x1
    ms = jnp.mean(x1 * x1, axis=-1, keepdims=True)
    hn_ref[...] = (x1 * lax.rsqrt(ms + EPS) * g2_ref[...]).astype(BF16)


def _merge(x, oa, ob, gates, wsb, wdf, wout, norm2, tm):
    m, d = x.shape
    const = lambda shape: pl.BlockSpec(shape, lambda i: (0, 0), pipeline_mode=pl.Buffered(1))
    return pl.pallas_call(
        _merge_kernel,
        grid=(m // tm,),
        in_specs=[pl.BlockSpec((tm, d), lambda i: (i, 0)),
                  pl.BlockSpec((tm, WIDTH), lambda i: (i, 0)),
                  pl.BlockSpec((tm, WIDTH), lambda i: (i, 0)),
                  pl.BlockSpec((tm, d), lambda i: (i, 0)),
                  pl.BlockSpec((tm, d), lambda i: (i, 1)),
                  const((WIDTH, d)), const((WIDTH, d)), const((d, d)),
                  pl.BlockSpec((1, d), lambda i: (0, 0))],
        out_specs=[pl.BlockSpec((tm, d), lambda i: (i, 0)), pl.BlockSpec((tm, d), lambda i: (i, 0))],
        out_shape=[jax.ShapeDtypeStruct((m, d), F32), jax.ShapeDtypeStruct((m, d), BF16)],
        compiler_params=_cparams(("parallel",)),
        name="merge",
    )(x, oa, ob, gates, gates, wsb, wdf, wout, norm2.reshape(1, d))


def _mlp_kernel(x1_ref, hn_ref, wup_ref, wdn_ref, o_ref):
    f = pl.program_id(1)

    @pl.when(f == 0)
    def _():
        o_ref[...] = x1_ref[...]

    u = jnp.maximum(jnp.dot(hn_ref[...], wup_ref[...], preferred_element_type=F32), 0.0)
    o_ref[...] += jnp.dot((u * u).astype(BF16), wdn_ref[...], preferred_element_type=F32)


def _mlp(x1, hn, wup, wdn, tm, tf):
    m, d = x1.shape
    dff = wup.shape[1]
    return pl.pallas_call(
        _mlp_kernel,
        grid=(m // tm, dff // tf),
        in_specs=[pl.BlockSpec((tm, d), lambda i, f: (i, 0)),
                  pl.BlockSpec((tm, d), lambda i, f: (i, 0)),
                  pl.BlockSpec((d, tf), lambda i, f: (0, f)),
                  pl.BlockSpec((tf, d), lambda i, f: (f, 0))],
        out_specs=pl.BlockSpec((tm, d), lambda i, f: (i, 0)),
        out_shape=jax.ShapeDtypeStruct((m, d), F32),
        compiler_params=_cparams(("parallel", "arbitrary")),
        name="mlp",
    )(x1, hn, wup, wdn)


def _project_group(x, norm1, w_in, q_norm, k_norm, tm):
    proj = functools.partial(_proj, _rmsnorm_bf16(x, norm1, min(tm, 512)), w_in, tm=tm)
    (qa,) = proj(COL_QA, 1, scale=SB_SCALE * LOG2E)
    ka, ka16 = proj(COL_KA, 1, want_f32=True)
    va, va16 = proj(COL_VA, 1, want_f32=True)
    (qb,) = proj(COL_QB, 1, qknorm=True, gain=q_norm, scale=DF_SCALE)
    kb, kb16 = proj(COL_KB, 1, qknorm=True, gain=k_norm, want_f32=True)
    vb, vb16 = proj(COL_VB, 1, want_f32=True)
    (gates,) = proj(COL_GA, 4)
    return dict(qa=qa, ka=ka, ka16=ka16, va=va, va16=va16, qb=qb, kb=kb, kb16=kb16, vb=vb, vb16=vb16, gates=gates)


def kernel(x_prompt, x_sample, cache_sb_k, cache_sb_v, cache_df_k, cache_df_v, page_table, rel_bias, norm1, w_in, q_norm, k_norm, lambda_q1, lambda_k1, lambda_q2, lambda_k2, subln, w_branch_sb, w_branch_df, w_out, norm2, w_up, w_down):
    depth = norm1.shape[0]
    batch, seq, d = x_prompt.shape
    n_seq, n_new, _ = x_sample.shape
    n_pages = page_table.shape[1]
    past = n_pages * PAGE_SIZE
    xp = x_prompt.reshape(batch * seq, d)
    xs = x_sample.reshape(n_seq * n_new, d)
    bias_p = _bias_prompt(rel_bias, ATT_T)
    bias_last, bias_new = _bias_sample(rel_bias, past, n_new)
    n_pool = cache_sb_k.shape[1]
    leaves = [[] for _ in range(8)]
    for l in range(depth):
        lam_init = 0.8 - 0.6 * math.exp(-0.3 * l)
        w_in16 = w_in[l]
        wsb16 = w_branch_sb[l].astype(BF16)
        wdf16 = w_branch_df[l].astype(BF16)
        wout16 = w_out[l].astype(BF16)
        wup16 = w_up[l].astype(BF16)
        wdn16 = w_down[l].astype(BF16)
        lams = [v[l].reshape(1, DF_HALF).astype(F32) for v in (lambda_q1, lambda_k1, lambda_q2, lambda_k2)]
        sub = subln[l].reshape(1, HEAD_DIM).astype(F32)

        p = _project_group(xp, norm1[l], w_in16, q_norm[l], k_norm[l], 1024)
        oa = _sb_prompt(p["qa"], p["ka16"], p["va16"], batch, seq, ATT_T)
        ob = _df_prompt(p["qb"], p["kb16"], p["vb16"], bias_p, lams, sub, lam_init, batch, seq, ATT_T)
        x1, hn = _merge(xp, oa, ob, p["gates"], wsb16, wdf16, wout16, norm2[l], 256)
        xp = _mlp(x1, hn, wup16, wdn16, 512, 1024)
        for i, name in enumerate(("ka", "va", "kb", "vb")):
            leaves[i].append(p[name].reshape(batch, seq, N_HEADS, HEAD_DIM))

        s = _project_group(xs, norm1[l], w_in16, q_norm[l], k_norm[l], 1024)
        shape4 = (depth, n_pool, PAGE_SIZE * N_HEADS, HEAD_DIM)
        oa, ob = _sample_attn(page_table, s["qa"], s["qb"], s["ka"], s["va"], s["kb"], s["vb"], bias_last, bias_new,
                              lams, sub, cache_sb_k.reshape(shape4), cache_sb_v.reshape(shape4),
                              cache_df_k.reshape(shape4), cache_df_v.reshape(shape4), l, lam_init)
        x1, hn = _merge(xs, oa, ob, s["gates"], wsb16, wdf16, wout16, norm2[l], 256)
        xs = _mlp(x1, hn, wup16, wdn16, 512, 1024)
        for i, name in enumerate(("ka", "va", "kb", "vb")):
            leaves[4 + i].append(s[name].reshape(n_seq, n_new, N_HEADS, HEAD_DIM))

    return (xp.reshape(batch, seq, d), xs.reshape(n_seq, n_new, d)) + tuple(jnp.stack(v) for v in leaves)
```

```python
import functools
import math

import numpy as np
import jax
import jax.numpy as jnp
from jax import lax
from jax.experimental import pallas as pl
from jax.experimental.pallas import tpu as pltpu

F32 = jnp.float32
BF16 = jnp.bfloat16

D_MODEL = 2048
N_HEADS = 8
HEAD_DIM = 128
DF_HALF = 64
WIDTH = N_HEADS * HEAD_DIM
SB_SCALE = 1.0 / math.sqrt(HEAD_DIM)
DF_SCALE = 1.0 / math.sqrt(DF_HALF)
D_FF = 4 * D_MODEL
N_BUCKETS = 32
MAX_EXACT = N_BUCKETS // 2
MAX_DISTANCE = 128
EPS = 1e-6
PAGE_SIZE = 128
NEG_BIG = -1e30

COL_QA, COL_KA, COL_VA, COL_QB, COL_KB, COL_VB, COL_GA, COL_GB = 0, 1, 2, 3, 4, 5, 6, 8

ATT_T = 512
SB_HEADS = 4
DF_HEADS = 2
ATT_UNIT = 256
SB_DONE_BITS = 160.0
ONES_ROWS = 16
LOG2E = 1.4426950408889634
PAGES_PER_STEP = 16
SB_HOT_PAGES = 2
RING_DEPTH = 3
QCOLS = 128
QROWS = 16

VMEM_LIMIT = 56 * 1024 * 1024


def _cparams(sem):
    return pltpu.CompilerParams(dimension_semantics=sem, vmem_limit_bytes=VMEM_LIMIT)


def _tokens_by_width(ref, n_tok):
    return jnp.concatenate([ref[pl.ds(h, n_tok, stride=N_HEADS), :] for h in range(N_HEADS)], axis=1)


def _softplus2(z):
    return jnp.maximum(z, 0.0) + jnp.log2(1.0 + jnp.exp2(-jnp.abs(z)))


def _emit_skewed(n_units, stages):
    state = [None] * n_units
    for tick in range(n_units + len(stages) - 1):
        for si in range(len(stages) - 1, -1, -1):
            u = tick - si
            if 0 <= u < n_units:
                state[u] = stages[si](u, state[u])
    return state


def _rmsnorm_kernel(x_ref, g_ref, o_ref):
    x = x_ref[...]
    ms = jnp.mean(x * x, axis=-1, keepdims=True)
    o_ref[...] = (x * lax.rsqrt(ms + EPS) * g_ref[...]).astype(o_ref.dtype)


def _rmsnorm_bf16(x, g, tm):
    m, d = x.shape
    return pl.pallas_call(
        _rmsnorm_kernel,
        grid=(m // tm,),
        in_specs=[pl.BlockSpec((tm, d), lambda i: (i, 0)),
                  pl.BlockSpec((1, d), lambda i: (0, 0))],
        out_specs=pl.BlockSpec((tm, d), lambda i: (i, 0)),
        out_shape=jax.ShapeDtypeStruct((m, d), BF16),
        compiler_params=_cparams(("parallel",)),
        name="rmsnorm_bf16",
    )(x, g.reshape(1, d))


def _proj_kernel(*refs, qknorm, scale, want_f32, want_bf16):
    h_ref, w_ref = refs[0], refs[1]
    pos = 2
    if qknorm:
        gain_ref, pmat_ref = refs[2], refs[3]
        pos = 4
    outs = refs[pos:]
    y = jnp.dot(h_ref[...], w_ref[...].astype(BF16), preferred_element_type=F32)
    if qknorm:
        tn = y.shape[1]
        pieces = []
        for c in range(tn // 256):
            yb = y[:, c * 256:(c + 1) * 256]
            ms = jnp.dot((yb * yb).astype(BF16), pmat_ref[...], preferred_element_type=F32)
            pieces.append(yb * lax.rsqrt(ms + EPS))
        y = jnp.concatenate(pieces, axis=1) * gain_ref[...]
    k = 0
    if want_f32:
        for hd in range(N_HEADS):
            outs[k][pl.ds(hd, y.shape[0], stride=N_HEADS), :] = y[:, hd * HEAD_DIM:(hd + 1) * HEAD_DIM]
        k += 1
    if want_bf16:
        outs[k][...] = (y * scale).astype(BF16) if scale != 1.0 else y.astype(BF16)


def _proj(h, w, col0, ncol, *, tm, qknorm=False, gain=None, scale=1.0, want_f32=False, want_bf16=True):
    m, kdim = h.shape
    tn = WIDTH
    in_specs = [pl.BlockSpec((tm, kdim), lambda n, i: (i, 0)),
                pl.BlockSpec((kdim, tn), lambda n, i: (0, col0 + n))]
    args = [h, w]
    if qknorm:
        pmat = np.kron(np.eye(256 // DF_HALF), np.full((DF_HALF, DF_HALF), 1.0 / DF_HALF)).astype(np.float32)
        in_specs += [pl.BlockSpec((1, tn), lambda n, i: (0, 0)),
                     pl.BlockSpec((256, 256), lambda n, i: (0, 0))]
        args += [jnp.tile(gain.astype(F32), tn // DF_HALF).reshape(1, tn), jnp.asarray(pmat, dtype=BF16)]
    out_specs, out_shape = [], []
    if want_f32:
        assert ncol == 1
        out_specs.append(pl.BlockSpec((tm * N_HEADS, HEAD_DIM), lambda n, i: (i, 0)))
        out_shape.append(jax.ShapeDtypeStruct((m * N_HEADS, HEAD_DIM), F32))
    if want_bf16:
        out_specs.append(pl.BlockSpec((tm, tn), lambda n, i: (i, n)))
        out_shape.append(jax.ShapeDtypeStruct((m, ncol * tn), BF16))
    res = pl.pallas_call(
        functools.partial(_proj_kernel, qknorm=qknorm, scale=scale, want_f32=want_f32, want_bf16=want_bf16),
        grid=(ncol, m // tm),
        in_specs=in_specs,
        out_specs=out_specs,
        out_shape=out_shape,
        compiler_params=_cparams(("parallel", "parallel")),
        name="in_proj",
    )(*args)
    return res


def _bucket(rel):
    n = jnp.maximum(rel, 0)
    nf = jnp.maximum(n, 1).astype(F32)
    large = MAX_EXACT + (jnp.log(nf / MAX_EXACT) / math.log(MAX_DISTANCE / MAX_EXACT)
                         * (N_BUCKETS - MAX_EXACT)).astype(jnp.int32)
    large = jnp.minimum(large, N_BUCKETS - 1)
    return jnp.where(n < MAX_EXACT, n, large)


def _bias_prompt_kernel(rb_ref, o_ref, *, t):
    h = pl.program_id(0)
    far = rb_ref[(N_BUCKETS - 1) * N_HEADS + h]

    def shifted_bias(rel):
        b = _bucket(rel)
        acc = jnp.zeros(rel.shape, F32)
        for k in range(N_BUCKETS - 1):
            acc = jnp.where(b == k, rb_ref[k * N_HEADS + h] - far, acc)
        return acc

    keys = lax.broadcasted_iota(jnp.int32, (t, t), 0)
    qrys = lax.broadcasted_iota(jnp.int32, (t, t), 1)
    o_ref[0, 0] = jnp.where(keys <= qrys, shifted_bias(qrys - keys), NEG_BIG)
    c = MAX_DISTANCE
    ki = lax.broadcasted_iota(jnp.int32, (c, c), 0)
    qj = lax.broadcasted_iota(jnp.int32, (c, c), 1)
    corner = jnp.concatenate([shifted_bias(qj - ki + c), jnp.zeros((c, t - c), F32)], axis=1)
    o_ref[0, 1] = jnp.concatenate([jnp.zeros((t - c, t), F32), corner], axis=0)


def _bias_prompt(rel_bias, t):
    return pl.pallas_call(
        functools.partial(_bias_prompt_kernel, t=t),
        grid=(N_HEADS,),
        in_specs=[pl.BlockSpec(memory_space=pltpu.SMEM)],
        out_specs=pl.BlockSpec((1, 2, t, t), lambda h: (h, 0, 0, 0)),
        out_shape=jax.ShapeDtypeStruct((N_HEADS, 2, t, t), F32),
        compiler_params=_cparams(("parallel",)),
        name="bias_prompt",
    )(rel_bias.reshape(-1))


def _bias_sample_kernel(rb_ref, last_ref, new_ref, *, past, n_new):
    def tile(rows, key0):
        lane = lax.broadcasted_iota(jnp.int32, (rows, QCOLS), 1)
        key = lax.broadcasted_iota(jnp.int32, (rows, QCOLS), 0) + key0
        head = lax.shift_right_logical(lane, (2 * n_new).bit_length() - 1)
        rel = past + (lane & (n_new - 1)) - key
        b = _bucket(rel)
        acc = jnp.zeros((rows, QCOLS), F32)
        for hh in range(N_HEADS):
            far = rb_ref[(N_BUCKETS - 1) * N_HEADS + hh]
            for k in range(N_BUCKETS - 1):
                acc = jnp.where((b == k) & (head == hh), rb_ref[k * N_HEADS + hh] - far, acc)
        return acc
    last_ref[...] = tile(PAGE_SIZE, past - PAGE_SIZE)
    new_ref[...] = tile(n_new, past)


def _bias_sample(rel_bias, past, n_new):
    return pl.pallas_call(
        functools.partial(_bias_sample_kernel, past=past, n_new=n_new),
        in_specs=[pl.BlockSpec(memory_space=pltpu.SMEM)],
        out_specs=[pl.BlockSpec((PAGE_SIZE, QCOLS), lambda: (0, 0)),
                   pl.BlockSpec((n_new, QCOLS), lambda: (0, 0))],
        out_shape=[jax.ShapeDtypeStruct((PAGE_SIZE, QCOLS), F32),
                   jax.ShapeDtypeStruct((n_new, QCOLS), F32)],
        name="bias_sample",
    )(rel_bias.reshape(-1))


def _lambda(lq1, lk1, lq2, lk2, lam_init):
    s1 = jnp.sum(lq1 * lk1, axis=-1, keepdims=True)
    s2 = jnp.sum(lq2 * lk2, axis=-1, keepdims=True)
    return jnp.exp(s1) - jnp.exp(s2) + lam_init


def _store_transposed(dst_ref, src_ref, t, row0=0):
    for ci in range(src_ref.shape[0] // t):
        chunk = src_ref[ci * t:(ci + 1) * t, :].astype(F32)
        dst_ref[row0:row0 + HEAD_DIM, ci * t:(ci + 1) * t] = chunk.T.astype(dst_ref.dtype)


def _head_cols(hh):
    return slice(hh * HEAD_DIM, (hh + 1) * HEAD_DIM)


def _sb_prompt_kernel(q_ref, k_ref, v_ref, upper_ref, o_ref, vt_sc, *, t):
    qi = pl.program_id(2)
    heads = range(SB_HEADS)

    @pl.when(qi == 0)
    def _():
        for hh in heads:
            _store_transposed(vt_sc.at[hh], v_ref.at[:, _head_cols(hh)], t)

    nsub = t // ATT_UNIT
    units = [(hh, cb) for hh in heads for cb in range(nsub)]
    qs = [q_ref[cb * ATT_UNIT:(cb + 1) * ATT_UNIT, _head_cols(hh)] for hh, cb in units]
    keys = lax.broadcasted_iota(jnp.int32, (ATT_UNIT, ATT_UNIT), 0)
    qrys = lax.broadcasted_iota(jnp.int32, (ATT_UNIT, ATT_UNIT), 1)
    tri = keys < qrys

    def block(j, carry, masked):
        start = pl.multiple_of(j * t, t)

        def scores(u, _):
            hh, _cb = units[u]
            return lax.dot_general(k_ref[pl.ds(start, t), _head_cols(hh)], qs[u], (((1,), (1,)), ((), ())),
                                   preferred_element_type=F32)

        def softplus(u, z):
            cb = units[u][1]
            c = carry[u][1]
            subs = []
            for kb in range(nsub - 1, -1, -1):
                if masked and kb > cb:
                    continue
                zk = z[kb * ATT_UNIT:(kb + 1) * ATT_UNIT]
                sp = _softplus2(zk)
                diag = masked and kb == cb
                if diag:
                    sp = jnp.where(tri, sp, 0.0)
                subs.append((kb, zk, sp, c, diag))
                c = c + jnp.sum(sp, axis=0, keepdims=True)
            return subs, c

        def suffixes(u, state):
            subs, c = state
            return [sub + (jnp.dot(upper_ref[...], sub[2].astype(BF16), preferred_element_type=F32),)
                    for sub in subs], c

        def weights(u, state):
            subs, c = state
            hh = units[u][0]
            acc = carry[u][0]
            for kb, zk, sp, c_in, diag, suffix in subs:
                log_a = zk - sp - suffix - c_in
                if diag:
                    log_a = jnp.where(tri, log_a, NEG_BIG)
                a = jnp.exp2(log_a).astype(BF16)
                acc = acc + jnp.dot(vt_sc[hh, :, pl.ds(start + kb * ATT_UNIT, ATT_UNIT)], a,
                                    preferred_element_type=F32)
            return acc, c

        return tuple(_emit_skewed(len(units), [scores, softplus, suffixes, weights]))

    carry = tuple((jnp.zeros((HEAD_DIM, ATT_UNIT), F32), jnp.zeros((1, ATT_UNIT), F32)) for _ in units)
    carry = block(qi, carry, True)

    def spent(cr):
        return functools.reduce(jnp.minimum, [jnp.min(c) for _, c in cr])

    def more(state):
        i, _, lowest = state
        return jnp.logical_and(i < qi, lowest < SB_DONE_BITS)

    def step(state):
        i, cr, _ = state
        cr = block(qi - 1 - i, cr, False)
        return i + 1, cr, spent(cr)

    _, carry, _ = lax.while_loop(more, step, (jnp.int32(0), carry, spent(carry)))
    for u, (hh, cb) in enumerate(units):
        o_ref[cb * ATT_UNIT:(cb + 1) * ATT_UNIT, _head_cols(hh)] = carry[u][0].T.astype(o_ref.dtype)


def _sb_prompt(q, k, v, batch, seq, t):
    nq = seq // t
    w = SB_HEADS * HEAD_DIM
    upper = jnp.asarray(np.triu(np.ones((ATT_UNIT, ATT_UNIT), np.float32), 1), dtype=BF16)
    return pl.pallas_call(
        functools.partial(_sb_prompt_kernel, t=t),
        grid=(batch, N_HEADS // SB_HEADS, nq),
        in_specs=[pl.BlockSpec((t, w), lambda b, h, i: (b * nq + i, h)),
                  pl.BlockSpec((seq, w), lambda b, h, i: (b, h)),
                  pl.BlockSpec((seq, w), lambda b, h, i: (b, h)),
                  pl.BlockSpec((ATT_UNIT, ATT_UNIT), lambda b, h, i: (0, 0))],
        out_specs=pl.BlockSpec((t, w), lambda b, h, i: (b * nq + i, h)),
        out_shape=jax.ShapeDtypeStruct((batch * seq, WIDTH), BF16),
        scratch_shapes=[pltpu.VMEM((SB_HEADS, HEAD_DIM, seq), BF16)],
        compiler_params=_cparams(("parallel", "parallel", "arbitrary")),
        name="sb_prompt",
    )(q, k, v, upper)


def _df_prompt_kernel(q_ref, k_ref, v_ref, bias_ref, lq1, lk1, lq2, lk2, g_ref, o_ref,
                      vt_sc, m_sc, l_sc, acc_sc, z_sc, *, t, lam_init):
    qi = pl.program_id(2)
    nq = pl.num_programs(2)
    heads = range(DF_HEADS)

    @pl.when(qi == 0)
    def _():
        for hh in heads:
            _store_transposed(vt_sc.at[hh], v_ref.at[:, _head_cols(hh)], t)
            vt_sc[hh, HEAD_DIM:, :] = jnp.ones((ONES_ROWS, vt_sc.shape[2]), BF16)

    lane = lax.broadcasted_iota(jnp.int32, (t, HEAD_DIM), 1)

    def stacked_queries(block):
        out = []
        for hh in heads:
            q = q_ref[pl.ds(pl.multiple_of(block * t, t), t), _head_cols(hh)]
            zero = jnp.zeros_like(q)
            out.append(jnp.concatenate([jnp.where(lane < DF_HALF, q, zero), jnp.where(lane >= DF_HALF, q, zero)],
                                       axis=0))
        return out

    qs = stacked_queries(qi)
    for hh in heads:
        m_sc[hh] = jnp.full(m_sc.shape[1:], NEG_BIG, F32)
        l_sc[hh] = jnp.zeros(l_sc.shape[1:], F32)
        acc_sc[hh] = jnp.zeros(acc_sc.shape[1:], F32)

    def scores(j, bias_d, slot, qs=qs):
        start = pl.multiple_of(j * t, t)
        for hh in heads:
            z = lax.dot_general(k_ref[pl.ds(start, t), _head_cols(hh)], qs[hh], (((1,), (1,)), ((), ())),
                                preferred_element_type=F32)
            if bias_d is not None:
                bias = bias_ref[hh, bias_d]
                z = z + jnp.concatenate([bias, bias], axis=1)
            z_sc[slot, hh] = z

    def consume(j, slot):
        for hh in heads:
            consume_head(j, z_sc[slot, hh], hh)

    def consume_head(j, z, hh):
        start = pl.multiple_of(j * t, t)
        m_old = m_sc[hh]
        m_new = jnp.maximum(m_old, jnp.max(z, axis=0, keepdims=True))
        p = jnp.exp(z - m_new).astype(BF16)
        alpha = jnp.exp(m_old - m_new)
        pv = jnp.dot(vt_sc[hh, :, pl.ds(start, t)], p, preferred_element_type=F32)
        l_sc[hh] = alpha * l_sc[hh] + pv[HEAD_DIM:HEAD_DIM + 1, :]
        acc_sc[hh] = alpha * acc_sc[hh] + pv[:HEAD_DIM, :]
        m_sc[hh] = m_new

    @pl.when(qi == 0)
    def _():
        scores(qi, 0, 0)

    def pair(p, near_bias):
        j = qi - 2 * p
        scores(j - 1, near_bias, 1)
        consume(j, 0)
        scores(j - 2, None, 0)
        consume(j - 1, 1)

    @pl.when(qi >= 2)
    def _():
        pair(0, 1)

    def far_pair(p, carry):
        pair(p, None)
        return carry
    lax.fori_loop(1, qi // 2, far_pair, 0)

    def odd_tail(near_bias):
        scores(0, near_bias, 1)
        consume(1, 0)
        consume(0, 1)

    @pl.when(qi == 1)
    def _():
        odd_tail(1)

    @pl.when(jnp.logical_and(qi % 2 == 1, qi > 1))
    def _():
        odd_tail(None)

    @pl.when(qi % 2 == 0)
    def _():
        consume(0, 0)

    def write_out():
        lam = _lambda(lq1[...], lk1[...], lq2[...], lk2[...], lam_init)
        for hh in heads:
            o = acc_sc[hh] / l_sc[hh]
            o = (o[:, :t] - lam * o[:, t:]).T
            ms = jnp.mean(o * o, axis=-1, keepdims=True)
            o_ref[:, _head_cols(hh)] = (o * lax.rsqrt(ms + EPS) * g_ref[...] * (1.0 - lam_init)).astype(o_ref.dtype)

    @pl.when(qi + 1 < nq)
    def _():
        scores(qi + 1, 0, 0, stacked_queries(qi + 1))
        write_out()

    @pl.when(qi + 1 == nq)
    def _():
        write_out()


def _df_prompt(q, k, v, bias, lams, subln, lam_init, batch, seq, t):
    nq = seq // t
    w = DF_HEADS * HEAD_DIM
    vec64 = pl.BlockSpec((1, DF_HALF), lambda b, h, i: (0, 0))
    return pl.pallas_call(
        functools.partial(_df_prompt_kernel, t=t, lam_init=lam_init),
        grid=(batch, N_HEADS // DF_HEADS, nq),
        in_specs=[pl.BlockSpec((seq, w), lambda b, h, i: (b, h)),
                  pl.BlockSpec((seq, w), lambda b, h, i: (b, h)),
                  pl.BlockSpec((seq, w), lambda b, h, i: (b, h)),
                  pl.BlockSpec((DF_HEADS, 2, t, t), lambda b, h, i: (h, 0, 0, 0)),
                  vec64, vec64, vec64, vec64,
                  pl.BlockSpec((1, HEAD_DIM), lambda b, h, i: (0, 0))],
        out_specs=pl.BlockSpec((t, w), lambda b, h, i: (b * nq + i, h)),
        out_shape=jax.ShapeDtypeStruct((batch * seq, WIDTH), BF16),
        scratch_shapes=[pltpu.VMEM((DF_HEADS, HEAD_DIM + ONES_ROWS, seq), BF16),
                        pltpu.VMEM((DF_HEADS, 1, 2 * t), F32), pltpu.VMEM((DF_HEADS, 1, 2 * t), F32),
                        pltpu.VMEM((DF_HEADS, HEAD_DIM, 2 * t), F32),
                        pltpu.VMEM((2, DF_HEADS, t, 2 * t), F32)],
        compiler_params=_cparams(("parallel", "parallel", "arbitrary")),
        name="df_prompt",
    )(q, k, v, bias, *lams, subln)


def _score_matrix(q_ref, keep_ref, seq_in_block, n_new, halves):
    row = lax.broadcasted_iota(jnp.int32, (QROWS, QCOLS), 0)
    lane = lax.broadcasted_iota(jnp.int32, (QROWS, QCOLS), 1)
    pick = (lane & (n_new - 1)) + seq_in_block * n_new == row
    if halves == 1:
        pick = pick & ((lane & n_new) == 0)
    spread = lax.dot_general(q_ref[...], jnp.where(pick, 1.0, 0.0).astype(BF16), (((0,), (0,)), ((), ())),
                             preferred_element_type=F32)
    return (spread * keep_ref[...]).astype(BF16)


def _score_pattern(n_new, halves):
    feat = np.arange(WIDTH)[:, None]
    col = np.arange(QCOLS)[None, :]
    keep = feat // HEAD_DIM == col // (2 * n_new)
    if halves == 2:
        keep = keep & ((feat // DF_HALF) % 2 == (col // n_new) % 2)
    return jnp.asarray(keep.astype(np.float32))


def _sample_attn_kernel(pt_ref, qa_ref, qb_ref, keep_sb_ref, keep_df_ref, kan_ref, van_ref, kbn_ref, vbn_ref,
                        blast_ref, bnew_ref, lq1, lk1, lq2, lk2, g_ref, *rest,
                        n_seq, n_pages, n_new, layer, lam_init):
    g = PAGES_PER_STEP
    c_sbk, c_dfk, c_sbv, c_dfv, oa_ref, ob_ref = rest[:6]
    zdf_sc, psb_sc, pdf_sc, accsb_sc, accdf_sc, qsb_sc, qdf_sc, ring, extra_buf, sems, extra_sem, extra_n = rest[6:]
    b = pl.program_id(0)
    s = pl.program_id(1)
    n_kv = n_pages // g
    n_steps = 2 * n_kv
    past = n_pages * PAGE_SIZE
    chunk = g * PAGE_SIZE
    hot = SB_HOT_PAGES * PAGE_SIZE
    group = g + SB_HOT_PAGES

    def page_copy(cache, page, block, slot):
        return pltpu.make_async_copy(cache.at[layer, page], ring.at[block], sems.at[slot])

    def group_copies(seq_i, step, slot):
        k = jnp.where(step < n_kv, step, step - n_kv)
        first = (n_kv - 1 - k) * g
        df = [(seq_i * n_pages + first + i, slot * group + i) for i in range(g)]
        sb = [(seq_i * n_pages + n_pages - SB_HOT_PAGES + i, slot * group + g + i) for i in range(SB_HOT_PAGES)]
        return df, sb, k == 0

    def start_group(seq_i, step, slot):
        df, sb, rides = group_copies(seq_i, step, slot)
        for in_phase, c_df, c_sb in ((step < n_kv, c_dfk, c_sbk), (step >= n_kv, c_dfv, c_sbv)):
            @pl.when(in_phase)
            def _():
                for idx, block in df:
                    page_copy(c_df, pt_ref[idx], block, slot).start()

            @pl.when(jnp.logical_and(in_phase, rides))
            def _():
                for idx, block in sb:
                    page_copy(c_sb, pt_ref[idx], block, slot).start()

    gidx = b * n_steps + s
    slot = lax.rem(gidx, RING_DEPTH)
    k_step = jnp.where(s < n_kv, s, s - n_kv)
    first_page = (n_kv - 1 - k_step) * g

    @pl.when(gidx == 0)
    def _():
        for d in range(RING_DEPTH - 1):
            start_group(jnp.int32(d // n_steps), jnp.int32(d % n_steps), jnp.int32(d))

    seq_ahead, step_ahead = b, s
    for k in range(n_steps):
        seq_off, step_k = divmod(k + RING_DEPTH - 1, n_steps)
        seq_ahead = jnp.where(s == k, b + seq_off, seq_ahead)
        step_ahead = jnp.where(s == k, step_k, step_ahead)

    @pl.when(seq_ahead < n_seq)
    def _():
        start_group(seq_ahead, step_ahead, lax.rem(gidx + (RING_DEPTH - 1), RING_DEPTH))

    for i in range(g):
        page_copy(c_dfk, 0, slot * group + i, slot).wait()

    @pl.when(k_step == 0)
    def _():
        for i in range(SB_HOT_PAGES):
            page_copy(c_sbk, 0, slot * group + g + i, slot).wait()

    def page_rows_bf16(ref):
        return _tokens_by_width(ref, PAGE_SIZE).astype(BF16)

    def pages_bf16(first, count):
        return jnp.concatenate([page_rows_bf16(ring.at[slot * group + first + i]) for i in range(count)], axis=0)

    def extra_page(cache, page):
        cp = pltpu.make_async_copy(cache.at[layer, pt_ref[b * n_pages + page]], extra_buf, extra_sem.at[0])
        cp.start()
        cp.wait()
        return page_rows_bf16(extra_buf)

    def new_tokens(ref):
        return _tokens_by_width(ref, n_new)

    def _queries():
        seq_in_block = lax.rem(b, QROWS // n_new)
        qsb_sc[...] = _score_matrix(qa_ref, keep_sb_ref, seq_in_block, n_new, 1)
        qdf_sc[...] = _score_matrix(qb_ref, keep_df_ref, seq_in_block, n_new, 2)

    def _scores():
        start = pl.multiple_of(first_page * PAGE_SIZE, chunk)
        zdf_sc[pl.ds(start, chunk), :] = jnp.dot(pages_bf16(0, g), qdf_sc[...], preferred_element_type=F32)

    def pad_rows_bf16(x):
        return jnp.concatenate([x, jnp.zeros_like(x)], axis=0).astype(BF16)

    lane_n = lax.broadcasted_iota(jnp.int32, (n_new, QCOLS), 1)
    key_n = lax.broadcasted_iota(jnp.int32, (n_new, QCOLS), 0)
    qpos_n = lane_n & (n_new - 1)

    def sb_weights(z, carry, upper):
        sp = _softplus2(z)
        suffix = jnp.dot(upper, sp.astype(BF16), preferred_element_type=F32)
        return jnp.exp2(z - sp - suffix - carry).astype(BF16), carry + jnp.sum(sp, axis=0, keepdims=True)

    def unspent(carry):
        real = (lax.broadcasted_iota(jnp.int32, (1, QCOLS), 1) & n_new) == 0
        return jnp.min(jnp.where(real, carry, SB_DONE_BITS))

    def _sb_weights():
        zs_new = jnp.dot(pad_rows_bf16(new_tokens(kan_ref)), qsb_sc[...], preferred_element_type=F32)[:n_new]
        strict = key_n < qpos_n
        sp_new = jnp.where(strict, _softplus2(zs_new), 0.0)
        carry = jnp.zeros((1, QCOLS), F32)
        suffix_rows = [None] * n_new
        for i in range(n_new - 1, -1, -1):
            suffix_rows[i] = carry
            carry = carry + sp_new[i:i + 1]
        suffix_new = jnp.concatenate(suffix_rows, axis=0)
        a_new = jnp.exp2(jnp.where(strict, zs_new - sp_new - suffix_new, NEG_BIG))
        accsb_sc[...] = lax.dot_general(pad_rows_bf16(a_new), pad_rows_bf16(new_tokens(van_ref)),
                                        (((0,), (0,)), ((), ())), preferred_element_type=F32)
        rr = lax.broadcasted_iota(jnp.int32, (hot, hot), 0)
        cc = lax.broadcasted_iota(jnp.int32, (hot, hot), 1)
        upper = jnp.where(cc > rr, 1.0, 0.0).astype(BF16)
        z_hot = jnp.dot(pages_bf16(g, SB_HOT_PAGES), qsb_sc[...], preferred_element_type=F32)
        a_hot, carry = sb_weights(z_hot, carry, upper)
        psb_sc[past - hot:past, :] = a_hot

        def more(state):
            page, _, lowest = state
            return jnp.logical_and(page >= 0, lowest < SB_DONE_BITS)

        def older(state):
            page, carry, _ = state
            z = jnp.dot(extra_page(c_sbk, page), qsb_sc[...], preferred_element_type=F32)
            a, carry = sb_weights(z, carry, upper[:PAGE_SIZE, :PAGE_SIZE])
            psb_sc[pl.ds(pl.multiple_of(page * PAGE_SIZE, PAGE_SIZE), PAGE_SIZE), :] = a
            return page - 1, carry, unspent(carry)

        oldest_hot = n_pages - SB_HOT_PAGES
        page, _, _ = lax.while_loop(more, older, (jnp.int32(oldest_hot - 1), carry, unspent(carry)))
        extra_n[0] = oldest_hot - 1 - page

    def _df_weights():
        z_new = jnp.dot(pad_rows_bf16(new_tokens(kbn_ref)), qdf_sc[...], preferred_element_type=F32)[:n_new]
        z_new = z_new + bnew_ref[...]
        z_new = jnp.where(key_n <= qpos_n, z_new, NEG_BIG)
        last0 = past - PAGE_SIZE
        zdf_sc[last0:past, :] = zdf_sc[last0:past, :] + blast_ref[...]
        zp = zdf_sc[...]
        mx = jnp.maximum(jnp.max(zp, axis=0, keepdims=True), jnp.max(z_new, axis=0, keepdims=True))
        e_new = jnp.exp(z_new - mx)
        ep = jnp.exp(zp - mx)
        inv = 1.0 / (jnp.sum(ep, axis=0, keepdims=True) + jnp.sum(e_new, axis=0, keepdims=True))
        pdf_sc[...] = (ep * inv).astype(BF16)
        accdf_sc[...] = lax.dot_general(pad_rows_bf16(e_new * inv), pad_rows_bf16(new_tokens(vbn_ref)),
                                        (((0,), (0,)), ((), ())), preferred_element_type=F32)

    def _values():
        start = pl.multiple_of(first_page * PAGE_SIZE, chunk)
        accdf_sc[...] += lax.dot_general(pdf_sc[pl.ds(start, chunk), :], pages_bf16(0, g),
                                         (((0,), (0,)), ((), ())), preferred_element_type=F32)

    def _sb_values():
        accsb_sc[...] += lax.dot_general(psb_sc[past - hot:past, :], pages_bf16(g, SB_HOT_PAGES),
                                         (((0,), (0,)), ((), ())), preferred_element_type=F32)

        def older(i, carry):
            page = n_pages - SB_HOT_PAGES - 1 - i
            rows = pl.ds(pl.multiple_of(page * PAGE_SIZE, PAGE_SIZE), PAGE_SIZE)
            accsb_sc[...] += lax.dot_general(psb_sc[rows, :], extra_page(c_sbv, page),
                                             (((0,), (0,)), ((), ())), preferred_element_type=F32)
            return carry
        lax.fori_loop(0, extra_n[0], older, 0)

    def _finish():
        lam = _lambda(lq1[...], lk1[...], lq2[...], lk2[...], lam_init)
        for h in range(N_HEADS):
            cols = slice(h * HEAD_DIM, (h + 1) * HEAD_DIM)
            oa_ref[:, cols] = accsb_sc[h * 2 * n_new:h * 2 * n_new + n_new, cols]
            o1 = accdf_sc[h * 2 * n_new:h * 2 * n_new + n_new, cols]
            o2 = accdf_sc[h * 2 * n_new + n_new:(h + 1) * 2 * n_new, cols]
            o = o1 - lam * o2
            ms = jnp.mean(o * o, axis=-1, keepdims=True)
            ob_ref[:, cols] = o * lax.rsqrt(ms + EPS) * g_ref[...] * (1.0 - lam_init)

    for k in range(n_kv):
        @pl.when(s == k)
        def _():
            if k == 0:
                _queries()
            _scores()
            if k == 0:
                _sb_weights()
            if k == n_kv - 1:
                _df_weights()

        @pl.when(s == n_kv + k)
        def _():
            _values()
            if k == 0:
                _sb_values()
            if k == n_kv - 1:
                _finish()


def _sample_attn(page_table, qsb, qdf, ka_n, va_n, kb_n, vb_n, blast, bnew, lams, subln,
                 c_sbk, c_sbv, c_dfk, c_dfv, layer, lam_init):
    n_seq, n_pages = page_table.shape
    n_new = ka_n.shape[0] // (n_seq * N_HEADS)
    assert 2 * N_HEADS * n_new == QCOLS
    g = PAGES_PER_STEP
    n_kv = n_pages // g
    past = n_pages * PAGE_SIZE

    assert n_pages % g == 0 and SB_HOT_PAGES <= n_pages and (RING_DEPTH - 2) // (2 * n_kv) < n_seq
    vec64 = pl.BlockSpec((1, DF_HALF), lambda b, s, pt: (0, 0))
    new_spec = pl.BlockSpec((n_new * N_HEADS, HEAD_DIM), lambda b, s, pt: (b, 0))
    out_spec = pl.BlockSpec((n_new, WIDTH), lambda b, s, pt: (b, 0))
    q_spec = pl.BlockSpec((QROWS, WIDTH), lambda b, s, pt: (b // (QROWS // n_new), 0))
    keep_spec = pl.BlockSpec((WIDTH, QCOLS), lambda b, s, pt: (0, 0))
    in_specs = [q_spec, q_spec, keep_spec, keep_spec,
                new_spec, new_spec, new_spec, new_spec,
                pl.BlockSpec((PAGE_SIZE, QCOLS), lambda b, s, pt: (0, 0)),
                pl.BlockSpec((n_new, QCOLS), lambda b, s, pt: (0, 0)),
                vec64, vec64, vec64, vec64,
                pl.BlockSpec((1, HEAD_DIM), lambda b, s, pt: (0, 0))]
    in_specs += [pl.BlockSpec(memory_space=pl.ANY)] * 4
    grid_spec = pltpu.PrefetchScalarGridSpec(
        num_scalar_prefetch=1,
        grid=(n_seq, 2 * n_kv),
        in_specs=in_specs,
        out_specs=[out_spec, out_spec],
        scratch_shapes=[pltpu.VMEM((past, QCOLS), F32),
                        pltpu.VMEM((past, QCOLS), BF16), pltpu.VMEM((past, QCOLS), BF16),
                        pltpu.VMEM((QCOLS, WIDTH), F32), pltpu.VMEM((QCOLS, WIDTH), F32),
                        pltpu.VMEM((WIDTH, QCOLS), BF16), pltpu.VMEM((WIDTH, QCOLS), BF16),
                        pltpu.VMEM((RING_DEPTH * (g + SB_HOT_PAGES), PAGE_SIZE * N_HEADS, HEAD_DIM), F32),
                        pltpu.VMEM((PAGE_SIZE * N_HEADS, HEAD_DIM), F32),
                        pltpu.SemaphoreType.DMA((RING_DEPTH,)), pltpu.SemaphoreType.DMA((1,)),
                        pltpu.SMEM((1,), jnp.int32)])
    return pl.pallas_call(
        functools.partial(_sample_attn_kernel, n_seq=n_seq, n_pages=n_pages, n_new=n_new, layer=layer,
                          lam_init=lam_init),
        grid_spec=grid_spec,
        out_shape=[jax.ShapeDtypeStruct((n_seq * n_new, WIDTH), F32)] * 2,
        compiler_params=_cparams(("arbitrary", "arbitrary")),
        name="sample_attn",
    )(page_table.reshape(-1), qsb, qdf, _score_pattern(n_new, 1), _score_pattern(n_new, 2),
      ka_n, va_n, kb_n, vb_n, blast, bnew, *lams, subln, c_sbk, c_dfk, c_sbv, c_dfv)


def _merge_kernel(x_ref, oa_ref, ob_ref, ga_ref, gb_ref, wsb_ref, wdf_ref, wout_ref, g2_ref, x1_ref, hn_ref):
    ya = jnp.dot(oa_ref[...].astype(BF16), wsb_ref[...], preferred_element_type=F32)
    yb = jnp.dot(ob_ref[...].astype(BF16), wdf_ref[...], preferred_element_type=F32)
    m = jax.nn.sigmoid(ga_ref[...].astype(F32)) * ya + jax.nn.sigmoid(gb_ref[...].astype(F32)) * yb
    x1 = x_ref[...] + jnp.dot(m.astype(BF16), wout_ref[...], preferred_element_type=F32)
    x1_ref[...] = x1
    ms = jnp.mean(x1 * x1, axis=-1, keepdims=True)
    hn_ref[...] = (x1 * lax.rsqrt(ms + EPS) * g2_ref[...]).astype(BF16)


def _merge(x, oa, ob, gates, wsb, wdf, wout, norm2, tm):
    m, d = x.shape
    const = lambda shape: pl.BlockSpec(shape, lambda i: (0, 0), pipeline_mode=pl.Buffered(1))
    return pl.pallas_call(
        _merge_kernel,
        grid=(m // tm,),
        in_specs=[pl.BlockSpec((tm, d), lambda i: (i, 0)),
                  pl.BlockSpec((tm, WIDTH), lambda i: (i, 0)),
                  pl.BlockSpec((tm, WIDTH), lambda i: (i, 0)),
                  pl.BlockSpec((tm, d), lambda i: (i, 0)),
                  pl.BlockSpec((tm, d), lambda i: (i, 1)),
                  const((WIDTH, d)), const((WIDTH, d)), const((d, d)),
                  pl.BlockSpec((1, d), lambda i: (0, 0))],
        out_specs=[pl.BlockSpec((tm, d), lambda i: (i, 0)), pl.BlockSpec((tm, d), lambda i: (i, 0))],
        out_shape=[jax.ShapeDtypeStruct((m, d), F32), jax.ShapeDtypeStruct((m, d), BF16)],
        compiler_params=_cparams(("parallel",)),
        name="merge",
    )(x, oa, ob, gates, gates, wsb, wdf, wout, norm2.reshape(1, d))


def _mlp_kernel(x1_ref, hn_ref, wup_ref, wdn_ref, o_ref):
    f = pl.program_id(1)

    @pl.when(f == 0)
    def _():
        o_ref[...] = x1_ref[...]

    u = jnp.maximum(jnp.dot(hn_ref[...], wup_ref[...], preferred_element_type=F32), 0.0)
    o_ref[...] += jnp.dot((u * u).astype(BF16), wdn_ref[...], preferred_element_type=F32)


def _mlp(x1, hn, wup, wdn, tm, tf):
    m, d = x1.shape
    dff = wup.shape[1]
    return pl.pallas_call(
        _mlp_kernel,
        grid=(m // tm, dff // tf),
        in_specs=[pl.BlockSpec((tm, d), lambda i, f: (i, 0)),
                  pl.BlockSpec((tm, d), lambda i, f: (i, 0)),
                  pl.BlockSpec((d, tf), lambda i, f: (0, f)),
                  pl.BlockSpec((tf, d), lambda i, f: (f, 0))],
        out_specs=pl.BlockSpec((tm, d), lambda i, f: (i, 0)),
        out_shape=jax.ShapeDtypeStruct((m, d), F32),
        compiler_params=_cparams(("parallel", "arbitrary")),
        name="mlp",
    )(x1, hn, wup, wdn)


def _project_group(x, norm1, w_in, q_norm, k_norm, tm):
    proj = functools.partial(_proj, _rmsnorm_bf16(x, norm1, min(tm, 512)), w_in, tm=tm)
    (qa,) = proj(COL_QA, 1, scale=SB_SCALE * LOG2E)
    ka, ka16 = proj(COL_KA, 1, want_f32=True)
    va, va16 = proj(COL_VA, 1, want_f32=True)
    (qb,) = proj(COL_QB, 1, qknorm=True, gain=q_norm, scale=DF_SCALE)
    kb, kb16 = proj(COL_KB, 1, qknorm=True, gain=k_norm, want_f32=True)
    vb, vb16 = proj(COL_VB, 1, want_f32=True)
    (gates,) = proj(COL_GA, 4)
    return dict(qa=qa, ka=ka, ka16=ka16, va=va, va16=va16, qb=qb, kb=kb, kb16=kb16, vb=vb, vb16=vb16, gates=gates)


def kernel(x_prompt, x_sample, cache_sb_k, cache_sb_v, cache_df_k, cache_df_v, page_table, rel_bias, norm1, w_in, q_norm, k_norm, lambda_q1, lambda_k1, lambda_q2, lambda_k2, subln, w_branch_sb, w_branch_df, w_out, norm2, w_up, w_down):
    depth = norm1.shape[0]
    batch, seq, d = x_prompt.shape
    n_seq, n_new, _ = x_sample.shape
    n_pages = page_table.shape[1]
    past = n_pages * PAGE_SIZE
    xp = x_prompt.reshape(batch * seq, d)
    xs = x_sample.reshape(n_seq * n_new, d)
    bias_p = _bias_prompt(rel_bias, ATT_T)
    bias_last, bias_new = _bias_sample(rel_bias, past, n_new)
    n_pool = cache_sb_k.shape[1]
    leaves = [[] for _ in range(8)]
    for l in range(depth):
        lam_init = 0.8 - 0.6 * math.exp(-0.3 * l)
        w_in_f32 = w_in[l]
        wsb16 = w_branch_sb[l].astype(BF16)
        wdf16 = w_branch_df[l].astype(BF16)
        wout16 = w_out[l].astype(BF16)
        wup16 = w_up[l].astype(BF16)
        wdn16 = w_down[l].astype(BF16)
        lams = [v[l].reshape(1, DF_HALF).astype(F32) for v in (lambda_q1, lambda_k1, lambda_q2, lambda_k2)]
        sub = subln[l].reshape(1, HEAD_DIM).astype(F32)

        p = _project_group(xp, norm1[l], w_in_f32, q_norm[l], k_norm[l], 1024)
        oa = _sb_prompt(p["qa"], p["ka16"], p["va16"], batch, seq, ATT_T)
        ob = _df_prompt(p["qb"], p["kb16"], p["vb16"], bias_p, lams, sub, lam_init, batch, seq, ATT_T)
        x1, hn = _merge(xp, oa, ob, p["gates"], wsb16, wdf16, wout16, norm2[l], 256)
        xp = _mlp(x1, hn, wup16, wdn16, 512, 1024)
        for i, name in enumerate(("ka", "va", "kb", "vb")):
            leaves[i].append(p[name].reshape(batch, seq, N_HEADS, HEAD_DIM))

        s = _project_group(xs, norm1[l], w_in_f32, q_norm[l], k_norm[l], 1024)
        shape4 = (depth, n_pool, PAGE_SIZE * N_HEADS, HEAD_DIM)
        oa, ob = _sample_attn(page_table, s["qa"], s["qb"], s["ka"], s["va"], s["kb"], s["vb"], bias_last, bias_new,
                              lams, sub, cache_sb_k.reshape(shape4), cache_sb_v.reshape(shape4),
                              cache_df_k.reshape(shape4), cache_df_v.reshape(shape4), l, lam_init)
        x1, hn = _merge(xs, oa, ob, s["gates"], wsb16, wdf16, wout16, norm2[l], 256)
        xs = _mlp(x1, hn, wup16, wdn16, 512, 1024)
        for i, name in enumerate(("ka", "va", "kb", "vb")):
            leaves[4 + i].append(s[name].reshape(n_seq, n_new, N_HEADS, HEAD_DIM))

    return (xp.reshape(batch, seq, d), xs.reshape(n_seq, n_new, d)) + tuple(jnp.stack(v) for v in leaves)
```
